```python
import jax, jax.numpy as jnp
from jax import lax
import numpy as np

D_MODEL = 2048
BATCH = 1
SEQ = 8192
DEPTH = 1

POOL_GROUPS = 4
POOL_WINDOWS = (2, 4, 8, 16)
POOL_GROUP_W = 192
POOL_W = POOL_GROUPS * POOL_GROUP_W
ATTN_HEADS = 6
HEAD_DIM = 128
ATTN_W = ATTN_HEADS * HEAD_DIM
IDX_HEADS = 4
IDX_DIM = 64
TOPK_MAX = 256
Q_BLOCK = 128
MEM_LEN = 256
XATTN_HEADS = 4
XATTN_W = XATTN_HEADS * HEAD_DIM
N_BRANCH = 3
N_GROUPS = 4
EXPERTS_PER_GROUP = 4
N_EXPERTS = N_GROUPS * EXPERTS_PER_GROUP
EXPERT_FF = 512
TOPK_EXPERT = 2
ROPE_THETA = 10000.0
NORM_EPS = 1e-6
IN_COLS = POOL_W + 3 * ATTN_W + IDX_HEADS * IDX_DIM + IDX_DIM + IDX_HEADS + XATTN_W + N_BRANCH * D_MODEL

kernel_name = "hybrid_pool_dsa_memxattn_hmoe_block"


def rms_norm(x, g):
    xf = x.astype(jnp.float32)
    y = xf * lax.rsqrt(jnp.mean(xf * xf, axis=-1, keepdims=True) + NORM_EPS)
    return (y * g.astype(jnp.float32)).astype(x.dtype)


def rotary(x, positions):
    d = x.shape[-1]
    inv_freq = ROPE_THETA ** (-jnp.arange(0, d, 2, dtype=jnp.float32) / d)
    ang = positions.astype(jnp.float32)[..., None] * inv_freq
    cos = jnp.cos(ang)[:, :, None, :]
    sin = jnp.sin(ang)[:, :, None, :]
    xf = x.astype(jnp.float32)
    x1, x2 = xf[..., : d // 2], xf[..., d // 2:]
    out = jnp.concatenate([x1 * cos - x2 * sin, x2 * cos + x1 * sin], axis=-1)
    return out.astype(x.dtype)


def causal_multiscale_pool(u):
    B, S, G, C = u.shape
    uf = u.astype(jnp.float32)
    cs = jnp.concatenate([jnp.zeros((B, 1, G, C), jnp.float32), jnp.cumsum(uf, axis=1)], axis=1)
    t = jnp.arange(S)
    outs = []
    for g, w in enumerate(POOL_WINDOWS):
        lo = jnp.maximum(t + 1 - w, 0)
        total = cs[:, 1:, g] - cs[:, lo, g]
        cnt = jnp.minimum(t + 1, w).astype(jnp.float32)
        outs.append(total / cnt[None, :, None])
    pooled = jnp.stack(outs, axis=2)
    return (pooled - uf).astype(u.dtype)


def dsa_attention(q, k, v, qi, ki, wi, k_sel):
    B, S, H, dh = q.shape
    n_blocks = S // Q_BLOCK
    key_pos = jnp.arange(S)
    scale = HEAD_DIM ** -0.5

    def one_block(i):
        start = i * Q_BLOCK
        q_b = lax.dynamic_slice_in_dim(q, start, Q_BLOCK, axis=1)
        qi_b = lax.dynamic_slice_in_dim(qi, start, Q_BLOCK, axis=1)
        wi_b = lax.dynamic_slice_in_dim(wi, start, Q_BLOCK, axis=1)
        t = start + jnp.arange(Q_BLOCK)
        dots = jnp.einsum('bqhd,bsd->bqhs', qi_b, ki).astype(jnp.float32)
        score = jnp.einsum('bqhs,bqh->bqs', jax.nn.relu(dots), wi_b.astype(jnp.float32))
        causal = key_pos[None, None, :] <= t[None, :, None]
        score = jnp.where(causal, score, -jnp.inf)
        _, idx = lax.top_k(score, k_sel)
        k_g = jax.vmap(lambda kb, ib: kb[ib])(k, idx)
        v_g = jax.vmap(lambda vb, ib: vb[ib])(v, idx)
        logits = jnp.einsum('bqhd,bqkhd->bhqk', q_b, k_g).astype(jnp.float32) * scale
        valid = idx <= t[None, :, None]
        logits = jnp.where(valid[:, None], logits, -jnp.inf)
        p = jax.nn.softmax(logits, axis=-1).astype(v.dtype)
        return jnp.einsum('bhqk,bqkhd->bqhd', p, v_g)

    outs = lax.map(one_block, jnp.arange(n_blocks))
    return jnp.transpose(outs, (1, 0, 2, 3, 4)).reshape(B, S, H * dh)


def hierarchical_moe(h, w_router_group, w_router_expert, w_e_gate, w_e_up, w_e_down):
    B, S, _ = h.shape
    lg = jnp.einsum('bsd,dg->bsg', h, w_router_group).astype(jnp.float32)
    pg = jax.nn.softmax(lg, axis=-1)
    g_top = jnp.argmax(lg, axis=-1)
    pg_top = jnp.take_along_axis(pg, g_top[..., None], axis=-1)
    le = jnp.einsum('bsd,de->bse', h, w_router_expert).astype(jnp.float32)
    le = le.reshape(B, S, N_GROUPS, EXPERTS_PER_GROUP)
    le_sel = jnp.take_along_axis(le, g_top[..., None, None], axis=2)[:, :, 0]
    pe = jax.nn.softmax(le_sel, axis=-1)
    pe_top, e_local = lax.top_k(pe, TOPK_EXPERT)
    w = pg_top * pe_top / jnp.sum(pe_top, axis=-1, keepdims=True)
    e_id = g_top[..., None] * EXPERTS_PER_GROUP + e_local
    combine = jnp.einsum('bsk,bske->bse', w, jax.nn.one_hot(e_id, N_EXPERTS, dtype=jnp.float32))
    a = jnp.einsum('bsd,edf->bsef', h, w_e_gate)
    b = jnp.einsum('bsd,edf->bsef', h, w_e_up)
    act = jax.nn.silu(a) * b * combine[..., None].astype(h.dtype)
    return jnp.einsum('bsef,efd->bsd', act, w_e_down)


def hybrid_layer(x, mem, positions, g_mix, w_in, b_gate, w_pool_grp, pool_scale,
                 q_norm_g, k_norm_g, g_mem, w_mem_kv, xq_norm_g, xk_norm_g,
                 w_pool_out, w_attn_out, w_cross_out, w_o, g_ffn,
                 w_router_group, w_router_expert, w_e_gate, w_e_up, w_e_down):
    B, S, D = x.shape
    k_sel = min(TOPK_MAX, S // 4)
    h = rms_norm(x, g_mix)
    proj = h @ w_in
    sizes = (POOL_W, ATTN_W, ATTN_W, ATTN_W, IDX_HEADS * IDX_DIM, IDX_DIM, IDX_HEADS, XATTN_W)
    split_pts = [int(p) for p in np.cumsum(sizes)]
    u_pool, q, k, v, qi, ki, wi, xq, gate_logits = jnp.split(proj, split_pts, axis=-1)

    u = u_pool.reshape(B, S, POOL_GROUPS, POOL_GROUP_W)
    p = causal_multiscale_pool(u)
    p = jnp.einsum('bsgc,gcd->bsgd', p, w_pool_grp).reshape(B, S, POOL_W) * pool_scale
    pool_out = p @ w_pool_out

    q = rotary(rms_norm(q.reshape(B, S, ATTN_HEADS, HEAD_DIM), q_norm_g), positions)
    k = rotary(rms_norm(k.reshape(B, S, ATTN_HEADS, HEAD_DIM), k_norm_g), positions)
    v = v.reshape(B, S, ATTN_HEADS, HEAD_DIM)
    qi = rotary(qi.reshape(B, S, IDX_HEADS, IDX_DIM), positions)
    ki = rotary(ki[:, :, None, :], positions)[:, :, 0]
    wi = wi * (IDX_HEADS ** -0.5) * (IDX_DIM ** -0.5)
    attn = dsa_attention(q, k, v, qi, ki, wi, k_sel)
    attn_out = attn @ w_attn_out

    mem_h = rms_norm(mem, g_mem)
    kv_m = mem_h @ w_mem_kv
    M = mem.shape[1]
    k_m = rms_norm(kv_m[..., :XATTN_W].reshape(B, M, XATTN_HEADS, HEAD_DIM), xk_norm_g)
    v_m = kv_m[..., XATTN_W:].reshape(B, M, XATTN_HEADS, HEAD_DIM)
    xq = rms_norm(xq.reshape(B, S, XATTN_HEADS, HEAD_DIM), xq_norm_g)
    xl = jnp.einsum('bshd,bmhd->bhsm', xq, k_m).astype(jnp.float32) * (HEAD_DIM ** -0.5)
    xp = jax.nn.softmax(xl, axis=-1).astype(v_m.dtype)
    cross = jnp.einsum('bhsm,bmhd->bshd', xp, v_m).reshape(B, S, XATTN_W)
    cross_out = cross @ w_cross_out

    gates = jax.nn.sigmoid((gate_logits + b_gate).astype(jnp.float32)).astype(x.dtype)
    gates = gates.reshape(B, S, N_BRANCH, D)
    merged = gates[:, :, 0] * pool_out + gates[:, :, 1] * attn_out + gates[:, :, 2] * cross_out
    x = x + merged @ w_o

    x = x + hierarchical_moe(rms_norm(x, g_ffn), w_router_group, w_router_expert, w_e_gate, w_e_up, w_e_down)
    return x


def setup_inputs(seed: int = 0) -> dict:
    key = jax.random.key(seed)
    ks = jax.random.split(key, 24)
    f32 = jnp.float32
    L = DEPTH

    def nrm(k, shape, fan_in):
        return jax.random.normal(k, shape, f32) * (fan_in ** -0.5)

    def gain(k, shape):
        return 1.0 + 0.05 * jax.random.normal(k, shape, f32)

    return {
        "x": jax.random.normal(ks[0], (BATCH, SEQ, D_MODEL), f32),
        "mem": jax.random.normal(ks[1], (BATCH, MEM_LEN, D_MODEL), f32),
        "positions": jnp.broadcast_to(jnp.arange(SEQ, dtype=jnp.int32), (BATCH, SEQ)),
        "g_mix": gain(ks[2], (L, D_MODEL)),
        "w_in": nrm(ks[3], (L, D_MODEL, IN_COLS), D_MODEL),
        "b_gate": 0.02 * jax.random.normal(ks[4], (L, N_BRANCH * D_MODEL), f32),
        "w_pool_grp": nrm(ks[5], (L, POOL_GROUPS, POOL_GROUP_W, POOL_GROUP_W), POOL_GROUP_W),
        "pool_scale": gain(ks[6], (L, POOL_W)),
        "q_norm_g": gain(ks[7], (L, HEAD_DIM)),
        "k_norm_g": gain(ks[8], (L, HEAD_DIM)),
        "g_mem": gain(ks[9], (L, D_MODEL)),
        "w_mem_kv": nrm(ks[10], (L, D_MODEL, 2 * XATTN_W), D_MODEL),
        "xq_norm_g": gain(ks[11], (L, HEAD_DIM)),
        "xk_norm_g": gain(ks[12], (L, HEAD_DIM)),
        "w_pool_out": nrm(ks[13], (L, POOL_W, D_MODEL), POOL_W),
        "w_attn_out": nrm(ks[14], (L, ATTN_W, D_MODEL), ATTN_W),
        "w_cross_out": nrm(ks[15], (L, XATTN_W, D_MODEL), XATTN_W),
        "w_o": nrm(ks[16], (L, D_MODEL, D_MODEL), D_MODEL),
        "g_ffn": gain(ks[17], (L, D_MODEL)),
        "w_router_group": nrm(ks[18], (L, D_MODEL, N_GROUPS), D_MODEL),
        "w_router_expert": nrm(ks[19], (L, D_MODEL, N_EXPERTS), D_MODEL),
        "w_e_gate": nrm(ks[20], (L, N_EXPERTS, D_MODEL, EXPERT_FF), D_MODEL),
        "w_e_up": nrm(ks[21], (L, N_EXPERTS, D_MODEL, EXPERT_FF), D_MODEL),
        "w_e_down": nrm(ks[22], (L, N_EXPERTS, EXPERT_FF, D_MODEL), EXPERT_FF),
    }


def reference(x, mem, positions, g_mix, w_in, b_gate, w_pool_grp, pool_scale,
              q_norm_g, k_norm_g, g_mem, w_mem_kv, xq_norm_g, xk_norm_g,
              w_pool_out, w_attn_out, w_cross_out, w_o, g_ffn,
              w_router_group, w_router_expert, w_e_gate, w_e_up, w_e_down):
    for l in range(DEPTH):
        x = hybrid_layer(x, mem, positions, g_mix[l], w_in[l], b_gate[l], w_pool_grp[l], pool_scale[l],
                         q_norm_g[l], k_norm_g[l], g_mem[l], w_mem_kv[l], xq_norm_g[l], xk_norm_g[l],
                         w_pool_out[l], w_attn_out[l], w_cross_out[l], w_o[l], g_ffn[l],
                         w_router_group[l], w_router_expert[l], w_e_gate[l], w_e_up[l], w_e_down[l])
    return x
```

```python
import functools
import math

import jax
import jax.numpy as jnp
from jax import lax
from jax.experimental import pallas as pl
from jax.experimental.pallas import tpu as pltpu

F32 = jnp.float32
BF16 = jnp.bfloat16
I32 = jnp.int32

NORM_EPS = 1e-6
ROPE_THETA = 10000.0
HEAD_DIM = 128
ATTN_HEADS = 6
ATTN_W = ATTN_HEADS * HEAD_DIM
IDX_HEADS = 4
IDX_DIM = 64
XATTN_HEADS = 4
XATTN_W = XATTN_HEADS * HEAD_DIM
POOL_WINDOWS = (2, 4, 8, 16)
POOL_GROUP_W = 192
POOL_W = len(POOL_WINDOWS) * POOL_GROUP_W
POOL_HALO = 16
TOPK_MAX = 256
N_GROUPS = 4
EXPERTS_PER_GROUP = 4
N_EXPERTS = N_GROUPS * EXPERTS_PER_GROUP
N_BRANCH = 3

LANES = 128
V7X_VMEM_LIMIT_BYTES = 56 * 1024 * 1024

COL_POOL = 0
COL_Q = COL_POOL + POOL_W
COL_K = COL_Q + ATTN_W
COL_V = COL_K + ATTN_W
COL_XQ = COL_V + ATTN_W
COL_QI = COL_XQ + XATTN_W
COL_KI = COL_QI + IDX_HEADS * LANES
COL_WI = COL_KI + LANES
FRONT_W = COL_WI + LANES

NEG_BIG = -1e30
INT_MIN = -2147483648
KEY_NEG_INF = -2139095041
NT_DIMS = (((1,), (1,)), ((), ()))


def _cparams(semantics):
    return pltpu.CompilerParams(dimension_semantics=semantics, vmem_limit_bytes=V7X_VMEM_LIMIT_BYTES)


def _resident(shape):
    return pl.BlockSpec(shape, lambda *_: (0,) * len(shape), pipeline_mode=pl.Buffered(1))


def _rms_rows(x, g):
    ms = jnp.mean(x * x, axis=-1, keepdims=True)
    return x * lax.rsqrt(ms + NORM_EPS) * g


def _sigmoid(x):
    return 1.0 / (1.0 + jnp.exp(-x))


def _proj_kernel(x_ref, g_ref, w_ref, o_ref, h_ref):
    @pl.when(pl.program_id(1) == 0)
    def _():
        h_ref[...] = _rms_rows(x_ref[...], g_ref[...]).astype(BF16)

    o_ref[...] = jnp.dot(h_ref[...], w_ref[...], preferred_element_type=F32).astype(o_ref.dtype)


def _gate_proj_kernel(x_ref, g_ref, w_ref, b_ref, o_ref, h_ref):
    @pl.when(pl.program_id(1) == 0)
    def _():
        h_ref[...] = _rms_rows(x_ref[...], g_ref[...]).astype(BF16)

    acc = jnp.dot(h_ref[...], w_ref[...], preferred_element_type=F32)
    o_ref[...] = _sigmoid(acc + b_ref[...]).astype(o_ref.dtype)


def _norm_proj(x, g, w, bias, *, tm, tn, out_dtype):
    s, d = x.shape
    n = w.shape[1]
    in_specs = [
        pl.BlockSpec((tm, d), lambda i, j: (i, 0)),
        pl.BlockSpec((1, d), lambda i, j: (0, 0)),
        pl.BlockSpec((d, tn), lambda i, j: (0, j)),
    ]
    args = [x, g, w]
    body = _proj_kernel
    if bias is not None:
        in_specs.append(pl.BlockSpec((1, tn), lambda i, j: (0, j)))
        args.append(bias)
        body = _gate_proj_kernel
    return pl.pallas_call(
        body,
        grid=(s // tm, n // tn),
        in_specs=in_specs,
        out_specs=pl.BlockSpec((tm, tn), lambda i, j: (i, j)),
        out_shape=jax.ShapeDtypeStruct((s, n), out_dtype),
        scratch_shapes=[pltpu.VMEM((tm, d), BF16)],
        compiler_params=_cparams(("parallel", "arbitrary")),
        name="norm_proj_gate" if bias is not None else "norm_proj_front",
    )(*args)


def _prep_kernel(f_ref, pos_ref, invf_ref, qg_ref, kg_ref, xg_ref,
                 q_ref, k_ref, v_ref, xq_ref, qi_ref, ki_ref, wib_ref):
    tm = f_ref.shape[0]
    pos = pos_ref[...].astype(F32)
    lane = lax.broadcasted_iota(I32, (1, LANES), 1)
    ang = pos * invf_ref[0:1, :]
    cos_a = jnp.cos(ang)
    sin_a = jnp.where(lane < HEAD_DIM // 2, -jnp.sin(ang), jnp.sin(ang))
    ang_i = pos * invf_ref[1:2, :]
    low_i = (lane & (IDX_DIM - 1)) < IDX_DIM // 2
    cos_i = jnp.cos(ang_i)
    sin_i = jnp.where(low_i, -jnp.sin(ang_i), jnp.sin(ang_i))

    def rope_head(y):
        return y * cos_a + pltpu.roll(y, HEAD_DIM // 2, 1) * sin_a

    def rope_idx(y):
        partner = jnp.where(low_i, pltpu.roll(y, LANES - IDX_DIM // 2, 1), pltpu.roll(y, IDX_DIM // 2, 1))
        return y * cos_i + partner * sin_i

    for h in range(ATTN_HEADS):
        sl = slice(h * HEAD_DIM, (h + 1) * HEAD_DIM)
        qh = f_ref[:, COL_Q + h * HEAD_DIM:COL_Q + (h + 1) * HEAD_DIM]
        kh = f_ref[:, COL_K + h * HEAD_DIM:COL_K + (h + 1) * HEAD_DIM]
        q_ref[:, sl] = rope_head(_rms_rows(qh, qg_ref[...])).astype(BF16)
        k_ref[:, sl] = rope_head(_rms_rows(kh, kg_ref[...])).astype(BF16)
    v_ref[...] = f_ref[:, COL_V:COL_V + ATTN_W].astype(BF16)
    for h in range(XATTN_HEADS):
        sl = slice(h * HEAD_DIM, (h + 1) * HEAD_DIM)
        xh = f_ref[:, COL_XQ + h * HEAD_DIM:COL_XQ + (h + 1) * HEAD_DIM]
        xq_ref[:, sl] = _rms_rows(xh, xg_ref[...]).astype(BF16)
    for h in range(IDX_HEADS):
        sl = slice(h * LANES, (h + 1) * LANES)
        qi_ref[:, sl] = rope_idx(f_ref[:, COL_QI + h * LANES:COL_QI + (h + 1) * LANES]).astype(BF16)
    ki_ref[...] = rope_idx(f_ref[:, COL_KI:COL_KI + LANES]).astype(BF16)
    wi = f_ref[:, COL_WI:COL_WI + LANES] * (IDX_HEADS ** -0.5) * (IDX_DIM ** -0.5)
    for h in range(IDX_HEADS):
        wib_ref[:, h * LANES:(h + 1) * LANES] = jnp.broadcast_to(wi[:, h:h + 1], (tm, LANES))


def _prep(front, pos_col, invf, qg, kg, xg, *, tm):
    s = front.shape[0]
    row = lambda w: pl.BlockSpec((tm, w), lambda i: (i, 0))
    small = lambda shape: pl.BlockSpec(shape, lambda i: (0, 0))
    widths_dtypes = [(ATTN_W, BF16), (ATTN_W, BF16), (ATTN_W, BF16), (XATTN_W, BF16),
                     (IDX_HEADS * LANES, BF16), (LANES, BF16), (IDX_HEADS * LANES, F32)]
    return pl.pallas_call(
        _prep_kernel,
        grid=(s // tm,),
        in_specs=[row(FRONT_W), row(1), small((2, LANES)), small((1, LANES)), small((1, LANES)),
                  small((1, LANES))],
        out_specs=[row(w) for w, _ in widths_dtypes],
        out_shape=[jax.ShapeDtypeStruct((s, w), dt) for w, dt in widths_dtypes],
        compiler_params=_cparams(("parallel",)),
        name="qknorm_rope",
    )(front, pos_col, invf, qg, kg, xg)


def _pool_kernel(u_ref, halo_ref, bd_ref, sc_ref, o_ref, buf_ref):
    i = pl.program_id(0)
    tm = u_ref.shape[0]
    u = u_ref[...]
    buf_ref[0:POOL_HALO, :] = jnp.where(i == 0, 0.0, halo_ref[...])
    buf_ref[POOL_HALO:POOL_HALO + tm, :] = u
    t = i * tm + lax.broadcasted_iota(I32, (tm, 1), 0)
    lane = lax.broadcasted_iota(I32, (1, POOL_W), 1)
    acc = u
    pooled = None
    for d in range(1, max(POOL_WINDOWS)):
        acc = acc + buf_ref[POOL_HALO - d:POOL_HALO - d + tm, :]
        if d + 1 in POOL_WINDOWS:
            g = POOL_WINDOWS.index(d + 1)
            mean = acc / jnp.minimum(t + 1, d + 1).astype(F32)
            in_group = (lane >= g * POOL_GROUP_W) & (lane < (g + 1) * POOL_GROUP_W)
            pooled = jnp.where(in_group, mean, 0.0 if pooled is None else pooled)
    p = (pooled - u).astype(BF16)
    y = jnp.dot(p, bd_ref[...], preferred_element_type=F32) * sc_ref[...]
    o_ref[...] = y.astype(BF16)


def _pool(front, bd, scale, *, tm):
    s = front.shape[0]
    halo_blocks = tm // POOL_HALO
    return pl.pallas_call(
        _pool_kernel,
        grid=(s // tm,),
        in_specs=[
            pl.BlockSpec((tm, POOL_W), lambda i: (i, 0)),
            pl.BlockSpec((POOL_HALO, POOL_W), lambda i: (jnp.maximum(i * halo_blocks - 1, 0), 0)),
            pl.BlockSpec((POOL_W, POOL_W), lambda i: (0, 0)),
            pl.BlockSpec((1, POOL_W), lambda i: (0, 0)),
        ],
        out_specs=pl.BlockSpec((tm, POOL_W), lambda i: (i, 0)),
        out_shape=jax.ShapeDtypeStruct((s, POOL_W), BF16),
        scratch_shapes=[pltpu.VMEM((tm + POOL_HALO, POOL_W), F32)],
        compiler_params=_cparams(("parallel",)),
        name="pool_mixer",
    )(front, front, bd, scale)


def _memkv_kernel(mem_ref, g_ref, w_ref, kg_ref, k_ref, v_ref):
    h = _rms_rows(mem_ref[...], g_ref[...]).astype(BF16)
    kv = jnp.dot(h, w_ref[...], preferred_element_type=F32)
    for hd in range(XATTN_HEADS):
        sl = slice(hd * HEAD_DIM, (hd + 1) * HEAD_DIM)
        k_ref[:, sl] = _rms_rows(kv[:, sl], kg_ref[...]).astype(BF16)
    v_ref[...] = kv[:, XATTN_W:].astype(BF16)


def _memkv(mem, g, w, kg):
    m = mem.shape[0]
    return pl.pallas_call(
        _memkv_kernel,
        out_shape=[jax.ShapeDtypeStruct((m, XATTN_W), BF16), jax.ShapeDtypeStruct((m, XATTN_W), BF16)],
        compiler_params=pltpu.CompilerParams(vmem_limit_bytes=V7X_VMEM_LIMIT_BYTES),
        name="mem_kv",
    )(mem, g, w, kg)


def _cross_kernel(xq_ref, k_ref, v_ref, o_ref):
    for h in range(XATTN_HEADS):
        sl = slice(h * HEAD_DIM, (h + 1) * HEAD_DIM)
        logits = lax.dot_general(xq_ref[:, sl], k_ref[:, sl], NT_DIMS,
                                 preferred_element_type=F32) * (HEAD_DIM ** -0.5)
        e = jnp.exp(logits - jnp.max(logits, axis=-1, keepdims=True))
        p = e / jnp.sum(e, axis=-1, keepdims=True)
        o_ref[:, sl] = jnp.dot(p.astype(BF16), v_ref[:, sl], preferred_element_type=F32).astype(BF16)


def _cross(xq, k_m, v_m, *, tm):
    s = xq.shape[0]
    m = k_m.shape[0]
    return pl.pallas_call(
        _cross_kernel,
        grid=(s // tm,),
        in_specs=[pl.BlockSpec((tm, XATTN_W), lambda i: (i, 0)),
                  pl.BlockSpec((m, XATTN_W), lambda i: (0, 0)),
                  pl.BlockSpec((m, XATTN_W), lambda i: (0, 0))],
        out_specs=pl.BlockSpec((tm, XATTN_W), lambda i: (i, 0)),
        out_shape=jax.ShapeDtypeStruct((s, XATTN_W), BF16),
        compiler_params=_cparams(("parallel",)),
        name="mem_cross_attn",
    )(xq, k_m, v_m)


def _key_to_f32(key):
    bits = key ^ ((key >> 31) & 0x7FFFFFFF)
    return lax.bitcast_convert_type(bits, F32)


def _dsa_kernel(qi_ref, wib_ref, q_ref, ki_ref, k_ref, v_ref, tri_ref, o_ref, sc_ref, *, tq, kc, k_sel):
    i = pl.program_id(0)
    n_chunks = ((i + 1) * tq + kc - 1) // kc
    t = i * tq + lax.broadcasted_iota(I32, (tq, 1), 0)
    slabs = kc // LANES

    def score_chunk(c, carry):
        off = pl.multiple_of(c * kc, kc)
        ki_c = ki_ref[pl.ds(off, kc), :]
        dots = [lax.dot_general(qi_ref[:, h * LANES:(h + 1) * LANES], ki_c, NT_DIMS,
                                preferred_element_type=F32) for h in range(IDX_HEADS)]
        for j in range(slabs):
            s = jnp.zeros((tq, LANES), F32)
            for h in range(IDX_HEADS):
                s = s + jnp.maximum(dots[h][:, j * LANES:(j + 1) * LANES], 0.0) * wib_ref[:, h * LANES:(h + 1) * LANES]
            kpos = off + j * LANES + lax.broadcasted_iota(I32, (1, LANES), 1)
            s = jnp.where(s == 0.0, 0.0, s)
            sc_ref[:, pl.ds(off + j * LANES, LANES)] = jnp.where(kpos <= t, s, -jnp.inf)
        return carry

    lax.fori_loop(0, n_chunks, score_chunk, 0)

    def count(pred):
        def body(c, acc):
            off = pl.multiple_of(c * kc, kc)
            for j in range(slabs):
                acc = acc + pred(sc_ref[:, pl.ds(off + j * LANES, LANES)]).astype(I32)
            return acc
        acc = lax.fori_loop(0, n_chunks, body, jnp.zeros((tq, LANES), I32))
        return jnp.sum(acc, axis=-1, keepdims=True)

    def bisect(b, thr_key):
        cand = thr_key + jnp.left_shift(jnp.int32(1), 31 - b)
        cand_f = _key_to_f32(cand)
        enough = (count(lambda s: s >= cand_f) >= k_sel) | (cand < KEY_NEG_INF)
        return jnp.where(enough, cand, thr_key)

    thr_key = lax.fori_loop(0, 32, bisect, jnp.full((tq, LANES), INT_MIN, I32))
    thr = _key_to_f32(thr_key)
    need = (k_sel - count(lambda s: s > thr)).astype(F32)
    thr_col = thr[:, 0:1]

    def mask_chunk(c, ties_before):
        off = pl.multiple_of(c * kc, kc)
        s = sc_ref[:, pl.ds(off, kc)]
        tie = s == thr_col
        prefix = jnp.dot(jnp.where(tie, 1.0, 0.0).astype(BF16), tri_ref[...], preferred_element_type=F32)
        kpos = off + lax.broadcasted_iota(I32, (1, kc), 1)
        take_tie = tie & (prefix + ties_before <= need)
        sel = ((s > thr_col) | take_tie) & (kpos <= t)
        sc_ref[:, pl.ds(off, kc)] = jnp.where(sel, 0.0, NEG_BIG)
        return ties_before + prefix[:, kc - 1:kc]

    lax.fori_loop(0, n_chunks, mask_chunk, jnp.zeros((tq, 1), F32))

    scale = HEAD_DIM ** -0.5

    def attn_chunk(c, carry):
        off = pl.multiple_of(c * kc, kc)
        bias = sc_ref[:, pl.ds(off, kc)]
        out = []
        for h in range(ATTN_HEADS):
            m_prev, l_prev, acc_prev = carry[h]
            sl = slice(h * HEAD_DIM, (h + 1) * HEAD_DIM)
            s = lax.dot_general(q_ref[:, sl], k_ref[pl.ds(off, kc), sl], NT_DIMS,
                                preferred_element_type=F32) * scale + bias
            m_new = jnp.maximum(m_prev, jnp.max(s, axis=-1, keepdims=True))
            alpha = jnp.exp(m_prev - m_new)
            p = jnp.exp(s - m_new)
            l_new = alpha * l_prev + jnp.sum(p, axis=-1, keepdims=True)
            acc_new = alpha * acc_prev + jnp.dot(p.astype(BF16), v_ref[pl.ds(off, kc), sl],
                                                 preferred_element_type=F32)
            out.append((m_new, l_new, acc_new))
        return tuple(out)

    init = tuple((jnp.full((tq, 1), NEG_BIG, F32), jnp.zeros((tq, 1), F32), jnp.zeros((tq, HEAD_DIM), F32))
                 for _ in range(ATTN_HEADS))
    final = lax.fori_loop(0, n_chunks, attn_chunk, init)
    for h in range(ATTN_HEADS):
        _, l_fin, acc_fin = final[h]
        o_ref[:, h * HEAD_DIM:(h + 1) * HEAD_DIM] = (acc_fin / l_fin).astype(BF16)


def _dsa(qi, wib, q, ki, k, v, *, tq, kc, k_sel):
    s = q.shape[0]
    tri = jnp.triu(jnp.ones((kc, kc), BF16))
    row = lambda w: pl.BlockSpec((tq, w), lambda i: (i, 0))
    return pl.pallas_call(
        functools.partial(_dsa_kernel, tq=tq, kc=kc, k_sel=k_sel),
        grid=(s // tq,),
        in_specs=[row(IDX_HEADS * LANES), row(IDX_HEADS * LANES), row(ATTN_W),
                  _resident((s, LANES)), _resident((s, ATTN_W)), _resident((s, ATTN_W)),
                  _resident((kc, kc))],
        out_specs=row(ATTN_W),
        out_shape=jax.ShapeDtypeStruct((s, ATTN_W), BF16),
        scratch_shapes=[pltpu.VMEM((tq, s), F32)],
        compiler_params=_cparams(("parallel",)),
        name="dsa_attention",
    )(qi, wib, q, ki, k, v, tri)


def _route(logits):
    lane = lax.broadcasted_iota(I32, (1, LANES), 1).astype(F32)
    far = float(LANES)
    is_group = lane < N_GROUPS
    g_max = jnp.max(jnp.where(is_group, logits, -jnp.inf), axis=-1, keepdims=True)
    g_top = jnp.min(jnp.where(is_group & (logits == g_max), lane, far), axis=-1, keepdims=True)
    pg_top = 1.0 / jnp.sum(jnp.where(is_group, jnp.exp(logits - g_max), 0.0), axis=-1, keepdims=True)
    first = N_GROUPS + EXPERTS_PER_GROUP * g_top
    in_grp = (lane >= first) & (lane < first + EXPERTS_PER_GROUP)
    e_max = jnp.max(jnp.where(in_grp, logits, -jnp.inf), axis=-1, keepdims=True)
    e_exp = jnp.where(in_grp, jnp.exp(logits - e_max), 0.0)
    pe = jnp.where(in_grp, e_exp / jnp.sum(e_exp, axis=-1, keepdims=True), -1.0)
    p1 = jnp.max(pe, axis=-1, keepdims=True)
    e1 = jnp.min(jnp.where(pe == p1, lane, far), axis=-1, keepdims=True)
    pe_rest = jnp.where(lane == e1, -1.0, pe)
    p2 = jnp.max(pe_rest, axis=-1, keepdims=True)
    e2 = jnp.min(jnp.where(pe_rest == p2, lane, far), axis=-1, keepdims=True)
    den = p1 + p2
    return jnp.where(lane == e1, pg_top * p1 / den, 0.0) + jnp.where(lane == e2, pg_top * p2 / den, 0.0)


def _merge_kernel(x_ref, p_ref, a_ref, c_ref, gt_ref, wpo_ref, wao_ref, wco_ref, wo_ref, gf_ref, wr_ref,
                  x1_ref, h2_ref, comb_ref):
    d = x_ref.shape[1]
    merged = gt_ref[:, 0:d].astype(F32) * jnp.dot(p_ref[...], wpo_ref[...], preferred_element_type=F32)
    merged += gt_ref[:, d:2 * d].astype(F32) * jnp.dot(a_ref[...], wao_ref[...], preferred_element_type=F32)
    merged += gt_ref[:, 2 * d:3 * d].astype(F32) * jnp.dot(c_ref[...], wco_ref[...], preferred_element_type=F32)
    x1 = x_ref[...] + jnp.dot(merged.astype(BF16), wo_ref[...], preferred_element_type=F32)
    x1_ref[...] = x1
    h2 = _rms_rows(x1, gf_ref[...]).astype(BF16)
    h2_ref[...] = h2
    comb_ref[...] = _route(jnp.dot(h2, wr_ref[...], preferred_element_type=F32))


def _merge(x, p, a, c, gates, wpo, wao, wco, wo, gf, wr, *, tm):
    s, d = x.shape
    row = lambda w: pl.BlockSpec((tm, w), lambda i: (i, 0))
    return pl.pallas_call(
        _merge_kernel,
        grid=(s // tm,),
        in_specs=[row(d), row(POOL_W), row(ATTN_W), row(XATTN_W), row(N_BRANCH * d),
                  _resident(wpo.shape), _resident(wao.shape), _resident(wco.shape), _resident(wo.shape),
                  _resident(gf.shape), _resident(wr.shape)],
        out_specs=[row(d), row(d), row(LANES)],
        out_shape=[jax.ShapeDtypeStruct((s, d), F32), jax.ShapeDtypeStruct((s, d), BF16),
                   jax.ShapeDtypeStruct((s, LANES), F32)],
        compiler_params=_cparams(("parallel",)),
        name="gated_merge_router",
    )(x, p, a, c, gates, wpo, wao, wco, wo, gf, wr)


def _moe_kernel(h_ref, x1_ref, comb_ref, wg_ref, wu_ref, wd_ref, o_ref):
    e = pl.program_id(1)

    @pl.when(e == 0)
    def _():
        o_ref[...] = x1_ref[...]

    h = h_ref[...]
    a = jnp.dot(h, wg_ref[0], preferred_element_type=F32)
    b = jnp.dot(h, wu_ref[0], preferred_element_type=F32)
    lane = lax.broadcasted_iota(I32, (1, LANES), 1)
    cw = jnp.sum(jnp.where(lane == e + N_GROUPS, comb_ref[...], 0.0), axis=-1, keepdims=True)
    act = (a * _sigmoid(a)) * b * cw
    o_ref[...] += jnp.dot(act.astype(BF16), wd_ref[0], preferred_element_type=F32)


def _moe(h2, x1, comb, wg, wu, wd, *, tm):
    s, d = x1.shape
    n_e, _, ff = wg.shape
    row = lambda w: pl.BlockSpec((tm, w), lambda i, e: (i, 0))
    return pl.pallas_call(
        _moe_kernel,
        grid=(s // tm, n_e),
        in_specs=[row(d), row(d), row(LANES),
                  pl.BlockSpec((1, d, ff), lambda i, e: (e, 0, 0)),
                  pl.BlockSpec((1, d, ff), lambda i, e: (e, 0, 0)),
                  pl.BlockSpec((1, ff, d), lambda i, e: (e, 0, 0))],
        out_specs=row(d),
        out_shape=jax.ShapeDtypeStruct((s, d), F32),
        compiler_params=_cparams(("parallel", "arbitrary")),
        name="moe_experts",
    )(h2, x1, comb, wg, wu, wd)


def _front_weight(w_in):
    d = w_in.shape[0]
    sizes = (POOL_W, ATTN_W, ATTN_W, ATTN_W, IDX_HEADS * IDX_DIM, IDX_DIM, IDX_HEADS, XATTN_W)
    offs = [0]
    for sz in sizes:
        offs.append(offs[-1] + sz)
    w_pool, w_q, w_k, w_v, w_qi, w_ki, w_wi, w_xq = (w_in[:, offs[n]:offs[n + 1]] for n in range(len(sizes)))
    pad = lambda w, to: jnp.pad(w, ((0, 0), (0, to - w.shape[1])))
    w_qi = jnp.pad(w_qi.reshape(d, IDX_HEADS, IDX_DIM), ((0, 0), (0, 0), (0, LANES - IDX_DIM)))
    front = jnp.concatenate([w_pool, w_q, w_k, w_v, w_xq, w_qi.reshape(d, IDX_HEADS * LANES),
                             pad(w_ki, LANES), pad(w_wi, LANES)], axis=1)
    return front, w_in[:, offs[-1]:]


def _rope_inv_freq():
    def inv(dim):
        return ROPE_THETA ** (-jnp.arange(0, dim, 2, dtype=F32) / dim)
    return jnp.stack([jnp.tile(inv(HEAD_DIM), LANES // (HEAD_DIM // 2)),
                      jnp.tile(inv(IDX_DIM), LANES // (IDX_DIM // 2))])


def _layer(x, mem, pos, g_mix, w_in, b_gate, w_pool_grp, pool_scale, q_norm_g, k_norm_g, g_mem, w_mem_kv,
           xq_norm_g, xk_norm_g, w_pool_out, w_attn_out, w_cross_out, w_o, g_ffn, w_router_group,
           w_router_expert, w_e_gate, w_e_up, w_e_down):
    s, d = x.shape
    row2 = lambda v: v.reshape(1, -1)
    k_sel = min(TOPK_MAX, s // 4)

    w_front, w_gate = _front_weight(w_in)
    front = _norm_proj(x, row2(g_mix), w_front.astype(BF16), None, tm=512, tn=FRONT_W // 2, out_dtype=F32)
    gates = _norm_proj(x, row2(g_mix), w_gate.astype(BF16), row2(b_gate), tm=1024, tn=1024, out_dtype=BF16)

    q, k, v, xq, qi, ki, wib = _prep(front, pos.reshape(s, 1), _rope_inv_freq(), row2(q_norm_g),
                                     row2(k_norm_g), row2(xq_norm_g), tm=512)

    bd = jax.scipy.linalg.block_diag(*[w_pool_grp[g] for g in range(len(POOL_WINDOWS))])
    pool_pre = _pool(front, bd.astype(BF16), row2(pool_scale), tm=512)

    k_m, v_m = _memkv(mem, row2(g_mem), w_mem_kv.astype(BF16), row2(xk_norm_g))
    cross = _cross(xq, k_m, v_m, tm=512)

    attn = _dsa(qi, wib, q, ki, k, v, tq=128, kc=512, k_sel=k_sel)

    w_router = jnp.pad(jnp.concatenate([w_router_group, w_router_expert], axis=1),
                       ((0, 0), (0, LANES - N_GROUPS - N_EXPERTS)))
    x1, h2, comb = _merge(x, pool_pre, attn, cross, gates, w_pool_out.astype(BF16), w_attn_out.astype(BF16),
                          w_cross_out.astype(BF16), w_o.astype(BF16), row2(g_ffn), w_router.astype(BF16),
                          tm=256)
    return _moe(h2, x1, comb, w_e_gate.astype(BF16), w_e_up.astype(BF16), w_e_down.astype(BF16), tm=512)


def kernel(x, mem, positions, g_mix, w_in, b_gate, w_pool_grp, pool_scale, q_norm_g, k_norm_g, g_mem, w_mem_kv,
           xq_norm_g, xk_norm_g, w_pool_out, w_attn_out, w_cross_out, w_o, g_ffn, w_router_group,
           w_router_expert, w_e_gate, w_e_up, w_e_down):
    depth = g_mix.shape[0]
    outs = []
    for b in range(x.shape[0]):
        xb = x[b]
        for l in range(depth):
            xb = _layer(xb, mem[b], positions[b], g_mix[l], w_in[l], b_gate[l], w_pool_grp[l], pool_scale[l],
                        q_norm_g[l], k_norm_g[l], g_mem[l], w_mem_kv[l], xq_norm_g[l], xk_norm_g[l],
                        w_pool_out[l], w_attn_out[l], w_cross_out[l], w_o[l], g_ffn[l], w_router_group[l],
                        w_router_expert[l], w_e_gate[l], w_e_up[l], w_e_down[l])
        outs.append(xb)
    return jnp.stack(outs)
```

```python
import functools
import math

import jax
import jax.numpy as jnp
from jax import lax
from jax.experimental import pallas as pl
from jax.experimental.pallas import tpu as pltpu

F32 = jnp.float32
BF16 = jnp.bfloat16
I32 = jnp.int32

NORM_EPS = 1e-6
ROPE_THETA = 10000.0
HEAD_DIM = 128
ATTN_HEADS = 6
ATTN_W = ATTN_HEADS * HEAD_DIM
IDX_HEADS = 4
IDX_DIM = 64
XATTN_HEADS = 4
XATTN_W = XATTN_HEADS * HEAD_DIM
POOL_WINDOWS = (2, 4, 8, 16)
POOL_GROUP_W = 192
POOL_W = len(POOL_WINDOWS) * POOL_GROUP_W
POOL_HALO = 16
TOPK_MAX = 256
N_GROUPS = 4
EXPERTS_PER_GROUP = 4
N_EXPERTS = N_GROUPS * EXPERTS_PER_GROUP
N_BRANCH = 3

LANES = 128
V7X_VMEM_LIMIT_BYTES = 56 * 1024 * 1024

COL_POOL = 0
COL_Q = COL_POOL + POOL_W
COL_K = COL_Q + ATTN_W
COL_V = COL_K + ATTN_W
COL_XQ = COL_V + ATTN_W
COL_QI = COL_XQ + XATTN_W
COL_KI = COL_QI + IDX_HEADS * LANES
COL_WI = COL_KI + LANES
FRONT_W = COL_WI + LANES

NEG_BIG = -1e30
INT_MIN = -2147483648
KEY_NEG_INF = -2139095041
SOFTMAX_SUM_MIN = 2.0 ** -64
SOFTMAX_SUM_MAX = 2.0 ** 100
NT_DIMS = (((1,), (1,)), ((), ()))


def _cparams(semantics):
    return pltpu.CompilerParams(dimension_semantics=semantics, vmem_limit_bytes=V7X_VMEM_LIMIT_BYTES)


def _resident(shape):
    return pl.BlockSpec(shape, lambda *_: (0,) * len(shape), pipeline_mode=pl.Buffered(1))


def _rms_rows(x, g):
    ms = jnp.mean(x * x, axis=-1, keepdims=True)
    return x * lax.rsqrt(ms + NORM_EPS) * g


def _sigmoid(x):
    return 1.0 / (1.0 + jnp.exp(-x))


def _proj_kernel(x_ref, g_ref, w_ref, o_ref, h_ref):
    @pl.when(pl.program_id(1) == 0)
    def _():
        h_ref[...] = _rms_rows(x_ref[...], g_ref[...]).astype(BF16)

    o_ref[...] = jnp.dot(h_ref[...], w_ref[...], preferred_element_type=F32).astype(o_ref.dtype)


def _gate_proj_kernel(x_ref, g_ref, w_ref, b_ref, o_ref, h_ref):
    @pl.when(pl.program_id(1) == 0)
    def _():
        h_ref[...] = _rms_rows(x_ref[...], g_ref[...]).astype(BF16)

    acc = jnp.dot(h_ref[...], w_ref[...], preferred_element_type=F32)
    o_ref[...] = _sigmoid(acc + b_ref[...]).astype(o_ref.dtype)


def _norm_proj(x, g, w, bias, *, tm, tn, out_dtype):
    s, d = x.shape
    n = w.shape[1]
    in_specs = [
        pl.BlockSpec((tm, d), lambda i, j: (i, 0)),
        pl.BlockSpec((1, d), lambda i, j: (0, 0)),
        pl.BlockSpec((d, tn), lambda i, j: (0, j)),
    ]
    args = [x, g, w]
    body = _proj_kernel
    if bias is not None:
        in_specs.append(pl.BlockSpec((1, tn), lambda i, j: (0, j)))
        args.append(bias)
        body = _gate_proj_kernel
    return pl.pallas_call(
        body,
        grid=(s // tm, n // tn),
        in_specs=in_specs,
        out_specs=pl.BlockSpec((tm, tn), lambda i, j: (i, j)),
        out_shape=jax.ShapeDtypeStruct((s, n), out_dtype),
        scratch_shapes=[pltpu.VMEM((tm, d), BF16)],
        compiler_params=_cparams(("parallel", "arbitrary")),
        name="norm_proj_gate" if bias is not None else "norm_proj_front",
    )(*args)


def _prep_kernel(f_ref, pos_ref, invf_ref, qg_ref, kg_ref, xg_ref,
                 q_ref, k_ref, v_ref, xq_ref, qi_ref, ki_ref, wib_ref):
    tm = f_ref.shape[0]
    pos = pos_ref[...].astype(F32)
    lane = lax.broadcasted_iota(I32, (1, LANES), 1)
    ang = pos * invf_ref[0:1, :]
    cos_a = jnp.cos(ang)
    sin_a = jnp.where(lane < HEAD_DIM // 2, -jnp.sin(ang), jnp.sin(ang))
    ang_i = pos * invf_ref[1:2, :]
    low_i = (lane & (IDX_DIM - 1)) < IDX_DIM // 2
    cos_i = jnp.cos(ang_i)
    sin_i = jnp.where(low_i, -jnp.sin(ang_i), jnp.sin(ang_i))

    def rope_head(y):
        return y * cos_a + pltpu.roll(y, HEAD_DIM // 2, 1) * sin_a

    def rope_idx(y):
        partner = jnp.where(low_i, pltpu.roll(y, LANES - IDX_DIM // 2, 1), pltpu.roll(y, IDX_DIM // 2, 1))
        return y * cos_i + partner * sin_i

    for h in range(ATTN_HEADS):
        sl = slice(h * HEAD_DIM, (h + 1) * HEAD_DIM)
        qh = f_ref[:, COL_Q + h * HEAD_DIM:COL_Q + (h + 1) * HEAD_DIM]
        kh = f_ref[:, COL_K + h * HEAD_DIM:COL_K + (h + 1) * HEAD_DIM]
        q_ref[:, sl] = rope_head(_rms_rows(qh, qg_ref[...])).astype(BF16)
        k_ref[:, sl] = rope_head(_rms_rows(kh, kg_ref[...])).astype(BF16)
    v_ref[...] = f_ref[:, COL_V:COL_V + ATTN_W].astype(BF16)
    for h in range(XATTN_HEADS):
        sl = slice(h * HEAD_DIM, (h + 1) * HEAD_DIM)
        xh = f_ref[:, COL_XQ + h * HEAD_DIM:COL_XQ + (h + 1) * HEAD_DIM]
        xq_ref[:, sl] = _rms_rows(xh, xg_ref[...]).astype(BF16)
    for h in range(IDX_HEADS):
        sl = slice(h * LANES, (h + 1) * LANES)
        qi_ref[:, sl] = rope_idx(f_ref[:, COL_QI + h * LANES:COL_QI + (h + 1) * LANES]).astype(BF16)
    ki_ref[...] = rope_idx(f_ref[:, COL_KI:COL_KI + LANES]).astype(BF16)
    wi = f_ref[:, COL_WI:COL_WI + LANES] * (IDX_HEADS ** -0.5) * (IDX_DIM ** -0.5)
    for h in range(IDX_HEADS):
        wib_ref[:, h * LANES:(h + 1) * LANES] = jnp.broadcast_to(wi[:, h:h + 1], (tm, LANES))


def _prep(front, pos_col, invf, qg, kg, xg, *, tm):
    s = front.shape[0]
    row = lambda w: pl.BlockSpec((tm, w), lambda i: (i, 0))
    small = lambda shape: pl.BlockSpec(shape, lambda i: (0, 0))
    widths_dtypes = [(ATTN_W, BF16), (ATTN_W, BF16), (ATTN_W, BF16), (XATTN_W, BF16),
                     (IDX_HEADS * LANES, BF16), (LANES, BF16), (IDX_HEADS * LANES, F32)]
    return pl.pallas_call(
        _prep_kernel,
        grid=(s // tm,),
        in_specs=[row(FRONT_W), row(1), small((2, LANES)), small((1, LANES)), small((1, LANES)),
                  small((1, LANES))],
        out_specs=[row(w) for w, _ in widths_dtypes],
        out_shape=[jax.ShapeDtypeStruct((s, w), dt) for w, dt in widths_dtypes],
        compiler_params=_cparams(("parallel",)),
        name="qknorm_rope",
    )(front, pos_col, invf, qg, kg, xg)


def _pool_kernel(u_ref, halo_ref, bd_ref, sc_ref, o_ref, buf_ref):
    i = pl.program_id(0)
    tm = u_ref.shape[0]
    u = u_ref[...]
    buf_ref[0:POOL_HALO, :] = jnp.where(i == 0, 0.0, halo_ref[...])
    buf_ref[POOL_HALO:POOL_HALO + tm, :] = u
    t = i * tm + lax.broadcasted_iota(I32, (tm, 1), 0)
    lane = lax.broadcasted_iota(I32, (1, POOL_W), 1)
    acc = u
    pooled = None
    for d in range(1, max(POOL_WINDOWS)):
        acc = acc + buf_ref[POOL_HALO - d:POOL_HALO - d + tm, :]
        if d + 1 in POOL_WINDOWS:
            g = POOL_WINDOWS.index(d + 1)
            mean = acc / jnp.minimum(t + 1, d + 1).astype(F32)
            in_group = (lane >= g * POOL_GROUP_W) & (lane < (g + 1) * POOL_GROUP_W)
            pooled = jnp.where(in_group, mean, 0.0 if pooled is None else pooled)
    p = (pooled - u).astype(BF16)
    y = jnp.dot(p, bd_ref[...], preferred_element_type=F32) * sc_ref[...]
    o_ref[...] = y.astype(BF16)


def _pool(front, bd, scale, *, tm):
    s = front.shape[0]
    halo_blocks = tm // POOL_HALO
    return pl.pallas_call(
        _pool_kernel,
        grid=(s // tm,),
        in_specs=[
            pl.BlockSpec((tm, POOL_W), lambda i: (i, 0)),
            pl.BlockSpec((POOL_HALO, POOL_W), lambda i: (jnp.maximum(i * halo_blocks - 1, 0), 0)),
            pl.BlockSpec((POOL_W, POOL_W), lambda i: (0, 0)),
            pl.BlockSpec((1, POOL_W), lambda i: (0, 0)),
        ],
        out_specs=pl.BlockSpec((tm, POOL_W), lambda i: (i, 0)),
        out_shape=jax.ShapeDtypeStruct((s, POOL_W), BF16),
        scratch_shapes=[pltpu.VMEM((tm + POOL_HALO, POOL_W), F32)],
        compiler_params=_cparams(("parallel",)),
        name="pool_mixer",
    )(front, front, bd, scale)


def _memkv_kernel(mem_ref, g_ref, w_ref, kg_ref, k_ref, v_ref):
    h = _rms_rows(mem_ref[...], g_ref[...]).astype(BF16)
    kv = jnp.dot(h, w_ref[...], preferred_element_type=F32)
    for hd in range(XATTN_HEADS):
        sl = slice(hd * HEAD_DIM, (hd + 1) * HEAD_DIM)
        k_ref[:, sl] = _rms_rows(kv[:, sl], kg_ref[...]).astype(BF16)
    v_ref[...] = kv[:, XATTN_W:].astype(BF16)


def _memkv(mem, g, w, kg):
    m = mem.shape[0]
    return pl.pallas_call(
        _memkv_kernel,
        out_shape=[jax.ShapeDtypeStruct((m, XATTN_W), BF16), jax.ShapeDtypeStruct((m, XATTN_W), BF16)],
        compiler_params=pltpu.CompilerParams(vmem_limit_bytes=V7X_VMEM_LIMIT_BYTES),
        name="mem_kv",
    )(mem, g, w, kg)


def _cross_kernel(xq_ref, k_ref, v_ref, o_ref):
    for h in range(XATTN_HEADS):
        sl = slice(h * HEAD_DIM, (h + 1) * HEAD_DIM)
        logits = lax.dot_general(xq_ref[:, sl], k_ref[:, sl], NT_DIMS,
                                 preferred_element_type=F32) * (HEAD_DIM ** -0.5)
        e = jnp.exp(logits - jnp.max(logits, axis=-1, keepdims=True))
        p = e / jnp.sum(e, axis=-1, keepdims=True)
        o_ref[:, sl] = jnp.dot(p.astype(BF16), v_ref[:, sl], preferred_element_type=F32).astype(BF16)


def _cross(xq, k_m, v_m, *, tm):
    s = xq.shape[0]
    m = k_m.shape[0]
    return pl.pallas_call(
        _cross_kernel,
        grid=(s // tm,),
        in_specs=[pl.BlockSpec((tm, XATTN_W), lambda i: (i, 0)),
                  pl.BlockSpec((m, XATTN_W), lambda i: (0, 0)),
                  pl.BlockSpec((m, XATTN_W), lambda i: (0, 0))],
        out_specs=pl.BlockSpec((tm, XATTN_W), lambda i: (i, 0)),
        out_shape=jax.ShapeDtypeStruct((s, XATTN_W), BF16),
        compiler_params=_cparams(("parallel",)),
        name="mem_cross_attn",
    )(xq, k_m, v_m)


def _key_to_f32(key):
    bits = key ^ ((key >> 31) & 0x7FFFFFFF)
    return lax.bitcast_convert_type(bits, F32)


def _dsa_kernel(qi_ref, wib_ref, q_ref, ki_ref, k_ref, v_ref, tri_ref, o_ref, sc_ref, lg_ref, *, tq, kc, k_sel):
    i = pl.program_id(0)
    n_chunks = ((i + 1) * tq + kc - 1) // kc
    t = i * tq + lax.broadcasted_iota(I32, (tq, 1), 0)
    slabs = kc // LANES

    def score_chunk(c, carry):
        off = pl.multiple_of(c * kc, kc)
        ki_c = ki_ref[pl.ds(off, kc), :]
        dots = [lax.dot_general(qi_ref[:, h * LANES:(h + 1) * LANES], ki_c, NT_DIMS,
                                preferred_element_type=F32) for h in range(IDX_HEADS)]
        for j in range(slabs):
            s = jnp.zeros((tq, LANES), F32)
            for h in range(IDX_HEADS):
                s = s + jnp.maximum(dots[h][:, j * LANES:(j + 1) * LANES], 0.0) * wib_ref[:, h * LANES:(h + 1) * LANES]
            kpos = off + j * LANES + lax.broadcasted_iota(I32, (1, LANES), 1)
            s = jnp.where(s == 0.0, 0.0, s)
            sc_ref[:, pl.ds(off + j * LANES, LANES)] = jnp.where(kpos <= t, s, -jnp.inf)
        return carry

    lax.fori_loop(0, n_chunks, score_chunk, 0)

    def count(pred):
        def body(c, acc):
            off = pl.multiple_of(c * kc, kc)
            for j in range(slabs):
                acc = acc + pred(sc_ref[:, pl.ds(off + j * LANES, LANES)]).astype(I32)
            return acc
        acc = lax.fori_loop(0, n_chunks, body, jnp.zeros((tq, LANES), I32))
        return jnp.sum(acc, axis=-1, keepdims=True)

    def bisect(b, thr_key):
        cand = thr_key + jnp.left_shift(jnp.int32(1), 31 - b)
        cand_f = _key_to_f32(cand)
        enough = (count(lambda s: s >= cand_f) >= k_sel) | (cand < KEY_NEG_INF)
        return jnp.where(enough, cand, thr_key)

    thr_key = lax.fori_loop(0, 32, bisect, jnp.full((tq, LANES), INT_MIN, I32))
    thr = _key_to_f32(thr_key)
    need = (k_sel - count(lambda s: s > thr)).astype(F32)
    thr_col = thr[:, 0:1]

    def mask_chunk(c, ties_before):
        off = pl.multiple_of(c * kc, kc)
        s = sc_ref[:, pl.ds(off, kc)]
        tie = s == thr_col
        prefix = jnp.dot(jnp.where(tie, 1.0, 0.0).astype(BF16), tri_ref[...], preferred_element_type=F32)
        kpos = off + lax.broadcasted_iota(I32, (1, kc), 1)
        take_tie = tie & (prefix + ties_before <= need)
        sel = ((s > thr_col) | take_tie) & (kpos <= t)
        sc_ref[:, pl.ds(off, kc)] = jnp.where(sel, 0.0, NEG_BIG)
        return ties_before + prefix[:, kc - 1:kc]

    lax.fori_loop(0, n_chunks, mask_chunk, jnp.zeros((tq, 1), F32))

    scale2 = (HEAD_DIM ** -0.5) * math.log2(math.e)
    heads = [slice(h * HEAD_DIM, (h + 1) * HEAD_DIM) for h in range(ATTN_HEADS)]

    def unshifted_chunk(c, carry):
        off = pl.multiple_of(c * kc, kc)
        bias = sc_ref[:, pl.ds(off, kc)]
        out = []
        for sl, (lane_sum, acc) in zip(heads, carry):
            p = jnp.exp2(lax.dot_general(q_ref[:, sl], k_ref[pl.ds(off, kc), sl], NT_DIMS,
                                         preferred_element_type=F32) * scale2 + bias)
            for j in range(slabs):
                lane_sum = lane_sum + p[:, j * LANES:(j + 1) * LANES]
            acc = acc + jnp.dot(p.astype(BF16), v_ref[pl.ds(off, kc), sl], preferred_element_type=F32)
            out.append((lane_sum, acc))
        return tuple(out)

    zeros = jnp.zeros((tq, LANES), F32)
    result = lax.fori_loop(0, n_chunks, unshifted_chunk, tuple((zeros, zeros) for _ in heads))
    in_range = None
    for sl, (lane_sum, acc) in zip(heads, result):
        row_sum = jnp.sum(lane_sum, axis=-1, keepdims=True)
        o_ref[:, sl] = (acc / row_sum).astype(BF16)
        ok = (row_sum >= SOFTMAX_SUM_MIN) & (row_sum <= SOFTMAX_SUM_MAX)
        in_range = ok if in_range is None else (in_range & ok)
    n_bad = jnp.sum(jnp.where(in_range, 0.0, 1.0))

    @pl.when(n_bad > 0.0)
    def _():
        _shifted_attention(q_ref, k_ref, v_ref, sc_ref, lg_ref, o_ref, n_chunks, tq=tq, kc=kc, scale2=scale2)


def _shifted_attention(q_ref, k_ref, v_ref, sc_ref, lg_ref, o_ref, n_chunks, *, tq, kc, scale2):
    slabs = kc // LANES
    for h in range(ATTN_HEADS):
        sl = slice(h * HEAD_DIM, (h + 1) * HEAD_DIM)

        def logits_chunk(c, lane_max, sl=sl):
            off = pl.multiple_of(c * kc, kc)
            s = lax.dot_general(q_ref[:, sl], k_ref[pl.ds(off, kc), sl], NT_DIMS,
                                preferred_element_type=F32) * scale2 + sc_ref[:, pl.ds(off, kc)]
            lg_ref[:, pl.ds(off, kc)] = s
            for j in range(slabs):
                lane_max = jnp.maximum(lane_max, s[:, j * LANES:(j + 1) * LANES])
            return lane_max

        lane_max = lax.fori_loop(0, n_chunks, logits_chunk, jnp.full((tq, LANES), NEG_BIG, F32))
        row_max = jnp.max(lane_max, axis=-1, keepdims=True)

        def pv_chunk(c, carry, sl=sl, row_max=row_max):
            lane_sum, acc = carry
            off = pl.multiple_of(c * kc, kc)
            p = jnp.exp2(lg_ref[:, pl.ds(off, kc)] - row_max)
            for j in range(slabs):
                lane_sum = lane_sum + p[:, j * LANES:(j + 1) * LANES]
            acc = acc + jnp.dot(p.astype(BF16), v_ref[pl.ds(off, kc), sl], preferred_element_type=F32)
            return lane_sum, acc

        lane_sum, acc = lax.fori_loop(0, n_chunks, pv_chunk,
                                      (jnp.zeros((tq, LANES), F32), jnp.zeros((tq, HEAD_DIM), F32)))
        o_ref[:, sl] = (acc / jnp.sum(lane_sum, axis=-1, keepdims=True)).astype(BF16)


def _dsa(qi, wib, q, ki, k, v, *, tq, kc, k_sel):
    s = q.shape[0]
    tri = jnp.triu(jnp.ones((kc, kc), BF16))
    row = lambda w: pl.BlockSpec((tq, w), lambda i: (i, 0))
    return pl.pallas_call(
        functools.partial(_dsa_kernel, tq=tq, kc=kc, k_sel=k_sel),
        grid=(s // tq,),
        in_specs=[row(IDX_HEADS * LANES), row(IDX_HEADS * LANES), row(ATTN_W),
                  _resident((s, LANES)), _resident((s, ATTN_W)), _resident((s, ATTN_W)),
                  _resident((kc, kc))],
        out_specs=row(ATTN_W),
        out_shape=jax.ShapeDtypeStruct((s, ATTN_W), BF16),
        scratch_shapes=[pltpu.VMEM((tq, s), F32), pltpu.VMEM((tq, s), F32)],
        compiler_params=_cparams(("parallel",)),
        name="dsa_attention",
    )(qi, wib, q, ki, k, v, tri)


def _route(logits):
    lane = lax.broadcasted_iota(I32, (1, LANES), 1).astype(F32)
    far = float(LANES)
    is_group = lane < N_GROUPS
    g_max = jnp.max(jnp.where(is_group, logits, -jnp.inf), axis=-1, keepdims=True)
    g_top = jnp.min(jnp.where(is_group & (logits == g_max), lane, far), axis=-1, keepdims=True)
    pg_top = 1.0 / jnp.sum(jnp.where(is_group, jnp.exp(logits - g_max), 0.0), axis=-1, keepdims=True)
    first = N_GROUPS + EXPERTS_PER_GROUP * g_top
    in_grp = (lane >= first) & (lane < first + EXPERTS_PER_GROUP)
    e_max = jnp.max(jnp.where(in_grp, logits, -jnp.inf), axis=-1, keepdims=True)
    e_exp = jnp.where(in_grp, jnp.exp(logits - e_max), 0.0)
    pe = jnp.where(in_grp, e_exp / jnp.sum(e_exp, axis=-1, keepdims=True), -1.0)
    p1 = jnp.max(pe, axis=-1, keepdims=True)
    e1 = jnp.min(jnp.where(pe == p1, lane, far), axis=-1, keepdims=True)
    pe_rest = jnp.where(lane == e1, -1.0, pe)
    p2 = jnp.max(pe_rest, axis=-1, keepdims=True)
    e2 = jnp.min(jnp.where(pe_rest == p2, lane, far), axis=-1, keepdims=True)
    den = p1 + p2
    return jnp.where(lane == e1, pg_top * p1 / den, 0.0) + jnp.where(lane == e2, pg_top * p2 / den, 0.0)


def _merge_kernel(x_ref, p_ref, a_ref, c_ref, gt_ref, wpo_ref, wao_ref, wco_ref, wo_ref, gf_ref, wr_ref,
                  x1_ref, h2_ref, comb_ref):
    d = x_ref.shape[1]
    merged = gt_ref[:, 0:d].astype(F32) * jnp.dot(p_ref[...], wpo_ref[...], preferred_element_type=F32)
    merged += gt_ref[:, d:2 * d].astype(F32) * jnp.dot(a_ref[...], wao_ref[...], preferred_element_type=F32)
    merged += gt_ref[:, 2 * d:3 * d].astype(F32) * jnp.dot(c_ref[...], wco_ref[...], preferred_element_type=F32)
    x1 = x_ref[...] + jnp.dot(merged.astype(BF16), wo_ref[...], preferred_element_type=F32)
    x1_ref[...] = x1
    h2 = _rms_rows(x1, gf_ref[...]).astype(BF16)
    h2_ref[...] = h2
    comb_ref[...] = _route(jnp.dot(h2, wr_ref[...], preferred_element_type=F32))


def _merge(x, p, a, c, gates, wpo, wao, wco, wo, gf, wr, *, tm):
    s, d = x.shape
    row = lambda w: pl.BlockSpec((tm, w), lambda i: (i, 0))
    return pl.pallas_call(
        _merge_kernel,
        grid=(s // tm,),
        in_specs=[row(d), row(POOL_W), row(ATTN_W), row(XATTN_W), row(N_BRANCH * d),
                  _resident(wpo.shape), _resident(wao.shape), _resident(wco.shape), _resident(wo.shape),
                  _resident(gf.shape), _resident(wr.shape)],
        out_specs=[row(d), row(d), row(LANES)],
        out_shape=[jax.ShapeDtypeStruct((s, d), F32), jax.ShapeDtypeStruct((s, d), BF16),
                   jax.ShapeDtypeStruct((s, LANES), F32)],
        compiler_params=_cparams(("parallel",)),
        name="gated_merge_router",
    )(x, p, a, c, gates, wpo, wao, wco, wo, gf, wr)


def _moe_kernel(h_ref, x1_ref, comb_ref, wg_ref, wu_ref, wd_ref, o_ref):
    e = pl.program_id(1)

    @pl.when(e == 0)
    def _():
        o_ref[...] = x1_ref[...]

    h = h_ref[...]
    a = jnp.dot(h, wg_ref[0], preferred_element_type=F32)
    b = jnp.dot(h, wu_ref[0], preferred_element_type=F32)
    lane = lax.broadcasted_iota(I32, (1, LANES), 1)
    cw = jnp.sum(jnp.where(lane == e + N_GROUPS, comb_ref[...], 0.0), axis=-1, keepdims=True)
    act = (a * _sigmoid(a)) * b * cw
    o_ref[...] += jnp.dot(act.astype(BF16), wd_ref[0], preferred_element_type=F32)


def _moe(h2, x1, comb, wg, wu, wd, *, tm):
    s, d = x1.shape
    n_e, _, ff = wg.shape
    row = lambda w: pl.BlockSpec((tm, w), lambda i, e: (i, 0))
    return pl.pallas_call(
        _moe_kernel,
        grid=(s // tm, n_e),
        in_specs=[row(d), row(d), row(LANES),
                  pl.BlockSpec((1, d, ff), lambda i, e: (e, 0, 0)),
                  pl.BlockSpec((1, d, ff), lambda i, e: (e, 0, 0)),
                  pl.BlockSpec((1, ff, d), lambda i, e: (e, 0, 0))],
        out_specs=row(d),
        out_shape=jax.ShapeDtypeStruct((s, d), F32),
        compiler_params=_cparams(("parallel", "arbitrary")),
        name="moe_experts",
    )(h2, x1, comb, wg, wu, wd)


def _front_weight(w_in):
    d = w_in.shape[0]
    sizes = (POOL_W, ATTN_W, ATTN_W, ATTN_W, IDX_HEADS * IDX_DIM, IDX_DIM, IDX_HEADS, XATTN_W)
    offs = [0]
    for sz in sizes:
        offs.append(offs[-1] + sz)
    w_pool, w_q, w_k, w_v, w_qi, w_ki, w_wi, w_xq = (w_in[:, offs[n]:offs[n + 1]] for n in range(len(sizes)))
    pad = lambda w, to: jnp.pad(w, ((0, 0), (0, to - w.shape[1])))
    w_qi = jnp.pad(w_qi.reshape(d, IDX_HEADS, IDX_DIM), ((0, 0), (0, 0), (0, LANES - IDX_DIM)))
    front = jnp.concatenate([w_pool, w_q, w_k, w_v, w_xq, w_qi.reshape(d, IDX_HEADS * LANES),
                             pad(w_ki, LANES), pad(w_wi, LANES)], axis=1)
    return front, w_in[:, offs[-1]:]


def _rope_inv_freq():
    def inv(dim):
        return ROPE_THETA ** (-jnp.arange(0, dim, 2, dtype=F32) / dim)
    return jnp.stack([jnp.tile(inv(HEAD_DIM), LANES // (HEAD_DIM // 2)),
                      jnp.tile(inv(IDX_DIM), LANES // (IDX_DIM // 2))])


def _layer(x, mem, pos, g_mix, w_in, b_gate, w_pool_grp, pool_scale, q_norm_g, k_norm_g, g_mem, w_mem_kv,
           xq_norm_g, xk_norm_g, w_pool_out, w_attn_out, w_cross_out, w_o, g_ffn, w_router_group,
           w_router_expert, w_e_gate, w_e_up, w_e_down):
    s, d = x.shape
    row2 = lambda v: v.reshape(1, -1)
    k_sel = min(TOPK_MAX, s // 4)

    w_front, w_gate = _front_weight(w_in)
    front = _norm_proj(x, row2(g_mix), w_front.astype(BF16), None, tm=512, tn=FRONT_W // 2, out_dtype=F32)
    gates = _norm_proj(x, row2(g_mix), w_gate.astype(BF16), row2(b_gate), tm=1024, tn=1024, out_dtype=BF16)

    q, k, v, xq, qi, ki, wib = _prep(front, pos.reshape(s, 1), _rope_inv_freq(), row2(q_norm_g),
                                     row2(k_norm_g), row2(xq_norm_g), tm=512)

    bd = jax.scipy.linalg.block_diag(*[w_pool_grp[g] for g in range(len(POOL_WINDOWS))])
    pool_pre = _pool(front, bd.astype(BF16), row2(pool_scale), tm=512)

    k_m, v_m = _memkv(mem, row2(g_mem), w_mem_kv.astype(BF16), row2(xk_norm_g))
    cross = _cross(xq, k_m, v_m, tm=512)

    attn = _dsa(qi, wib, q, ki, k, v, tq=128, kc=512, k_sel=k_sel)

    w_router = jnp.pad(jnp.concatenate([w_router_group, w_router_expert], axis=1),
                       ((0, 0), (0, LANES - N_GROUPS - N_EXPERTS)))
    x1, h2, comb = _merge(x, pool_pre, attn, cross, gates, w_pool_out.astype(BF16), w_attn_out.astype(BF16),
                          w_cross_out.astype(BF16), w_o.astype(BF16), row2(g_ffn), w_router.astype(BF16),
                          tm=256)
    return _moe(h2, x1, comb, w_e_gate.astype(BF16), w_e_up.astype(BF16), w_e_down.astype(BF16), tm=512)


def kernel(x, mem, positions, g_mix, w_in, b_gate, w_pool_grp, pool_scale, q_norm_g, k_norm_g, g_mem, w_mem_kv,
           xq_norm_g, xk_norm_g, w_pool_out, w_attn_out, w_cross_out, w_o, g_ffn, w_router_group,
           w_router_expert, w_e_gate, w_e_up, w_e_down):
    depth = g_mix.shape[0]
    outs = []
    for b in range(x.shape[0]):
        xb = x[b]
        for l in range(depth):
            xb = _layer(xb, mem[b], positions[b], g_mix[l], w_in[l], b_gate[l], w_pool_grp[l], pool_scale[l],
                        q_norm_g[l], k_norm_g[l], g_mem[l], w_mem_kv[l], xq_norm_g[l], xk_norm_g[l],
                        w_pool_out[l], w_attn_out[l], w_cross_out[l], w_o[l], g_ffn[l], w_router_group[l],
                        w_router_expert[l], w_e_gate[l], w_e_up[l], w_e_down[l])
        outs.append(xb)
    return jnp.stack(outs)
```

```python
import functools
import math

import jax
import jax.numpy as jnp
from jax import lax
from jax.experimental import pallas as pl
from jax.experimental.pallas import tpu as pltpu

F32 = jnp.float32
BF16 = jnp.bfloat16
I32 = jnp.int32

NORM_EPS = 1e-6
ROPE_THETA = 10000.0
HEAD_DIM = 128
ATTN_HEADS = 6
ATTN_W = ATTN_HEADS * HEAD_DIM
IDX_HEADS = 4
IDX_DIM = 64
XATTN_HEADS = 4
XATTN_W = XATTN_HEADS * HEAD_DIM
POOL_WINDOWS = (2, 4, 8, 16)
POOL_GROUP_W = 192
POOL_W = len(POOL_WINDOWS) * POOL_GROUP_W
POOL_HALO = 16
TOPK_MAX = 256
N_GROUPS = 4
EXPERTS_PER_GROUP = 4
N_EXPERTS = N_GROUPS * EXPERTS_PER_GROUP
N_BRANCH = 3
PAIRS_PER_GROUP = EXPERTS_PER_GROUP * (EXPERTS_PER_GROUP - 1) // 2
ROUTE_BUCKET, ROUTE_RANK, ROUTE_W_LO, ROUTE_W_HI, ROUTE_COLS = 0, 1, 2, 3, 4

LANES = 128
SUBLANES = 8
V7X_VMEM_LIMIT_BYTES = 56 * 1024 * 1024

COL_POOL = 0
COL_Q = COL_POOL + POOL_W
COL_K = COL_Q + ATTN_W
COL_V = COL_K + ATTN_W
COL_XQ = COL_V + ATTN_W
COL_QI = COL_XQ + XATTN_W
COL_KI = COL_QI + IDX_HEADS * LANES
COL_WI = COL_KI + LANES
FRONT_W = COL_WI + LANES

NEG_BIG = -1e30
INT_MIN = -2147483648
KEY_NEG_INF = -2139095041
SOFTMAX_SUM_MIN = 2.0 ** -64
SOFTMAX_SUM_MAX = 2.0 ** 100
NT_DIMS = (((1,), (1,)), ((), ()))


def _cparams(semantics):
    return pltpu.CompilerParams(dimension_semantics=semantics, vmem_limit_bytes=V7X_VMEM_LIMIT_BYTES)


def _resident(shape):
    return pl.BlockSpec(shape, lambda *_: (0,) * len(shape), pipeline_mode=pl.Buffered(1))


def _rms_rows(x, g):
    ms = jnp.mean(x * x, axis=-1, keepdims=True)
    return x * lax.rsqrt(ms + NORM_EPS) * g


def _sigmoid(x):
    return 1.0 / (1.0 + jnp.exp(-x))


def _proj_kernel(x_ref, g_ref, w_ref, o_ref, h_ref):
    @pl.when(pl.program_id(1) == 0)
    def _():
        h_ref[...] = _rms_rows(x_ref[...], g_ref[...]).astype(BF16)

    o_ref[...] = jnp.dot(h_ref[...], w_ref[...], preferred_element_type=F32).astype(o_ref.dtype)


def _gate_proj_kernel(x_ref, g_ref, w_ref, b_ref, o_ref, h_ref):
    @pl.when(pl.program_id(1) == 0)
    def _():
        h_ref[...] = _rms_rows(x_ref[...], g_ref[...]).astype(BF16)

    acc = jnp.dot(h_ref[...], w_ref[...], preferred_element_type=F32)
    o_ref[...] = _sigmoid(acc + b_ref[...]).astype(o_ref.dtype)


def _norm_proj(x, g, w, bias, *, tm, tn, out_dtype):
    s, d = x.shape
    n = w.shape[1]
    in_specs = [
        pl.BlockSpec((tm, d), lambda i, j: (i, 0)),
        pl.BlockSpec((1, d), lambda i, j: (0, 0)),
        pl.BlockSpec((d, tn), lambda i, j: (0, j)),
    ]
    args = [x, g, w]
    body = _proj_kernel
    if bias is not None:
        in_specs.append(pl.BlockSpec((1, tn), lambda i, j: (0, j)))
        args.append(bias)
        body = _gate_proj_kernel
    return pl.pallas_call(
        body,
        grid=(s // tm, n // tn),
        in_specs=in_specs,
        out_specs=pl.BlockSpec((tm, tn), lambda i, j: (i, j)),
        out_shape=jax.ShapeDtypeStruct((s, n), out_dtype),
        scratch_shapes=[pltpu.VMEM((tm, d), BF16)],
        compiler_params=_cparams(("parallel", "arbitrary")),
        name="norm_proj_gate" if bias is not None else "norm_proj_front",
    )(*args)


def _prep_kernel(f_ref, pos_ref, invf_ref, qg_ref, kg_ref, xg_ref,
                 q_ref, k_ref, v_ref, xq_ref, qi_ref, ki_ref, wib_ref):
    tm = f_ref.shape[0]
    pos = pos_ref[...].astype(F32)
    lane = lax.broadcasted_iota(I32, (1, LANES), 1)
    ang = pos * invf_ref[0:1, :]
    cos_a = jnp.cos(ang)
    sin_a = jnp.where(lane < HEAD_DIM // 2, -jnp.sin(ang), jnp.sin(ang))
    ang_i = pos * invf_ref[1:2, :]
    low_i = (lane & (IDX_DIM - 1)) < IDX_DIM // 2
    cos_i = jnp.cos(ang_i)
    sin_i = jnp.where(low_i, -jnp.sin(ang_i), jnp.sin(ang_i))

    def rope_head(y):
        return y * cos_a + pltpu.roll(y, HEAD_DIM // 2, 1) * sin_a

    def rope_idx(y):
        partner = jnp.where(low_i, pltpu.roll(y, LANES - IDX_DIM // 2, 1), pltpu.roll(y, IDX_DIM // 2, 1))
        return y * cos_i + partner * sin_i

    for h in range(ATTN_HEADS):
        sl = slice(h * HEAD_DIM, (h + 1) * HEAD_DIM)
        qh = f_ref[:, COL_Q + h * HEAD_DIM:COL_Q + (h + 1) * HEAD_DIM]
        kh = f_ref[:, COL_K + h * HEAD_DIM:COL_K + (h + 1) * HEAD_DIM]
        q_ref[:, sl] = rope_head(_rms_rows(qh, qg_ref[...])).astype(BF16)
        k_ref[:, sl] = rope_head(_rms_rows(kh, kg_ref[...])).astype(BF16)
    v_ref[...] = f_ref[:, COL_V:COL_V + ATTN_W].astype(BF16)
    for h in range(XATTN_HEADS):
        sl = slice(h * HEAD_DIM, (h + 1) * HEAD_DIM)
        xh = f_ref[:, COL_XQ + h * HEAD_DIM:COL_XQ + (h + 1) * HEAD_DIM]
        xq_ref[:, sl] = _rms_rows(xh, xg_ref[...]).astype(BF16)
    for h in range(IDX_HEADS):
        sl = slice(h * LANES, (h + 1) * LANES)
        qi_ref[:, sl] = rope_idx(f_ref[:, COL_QI + h * LANES:COL_QI + (h + 1) * LANES]).astype(BF16)
    ki_ref[...] = rope_idx(f_ref[:, COL_KI:COL_KI + LANES]).astype(BF16)
    wi = f_ref[:, COL_WI:COL_WI + LANES] * (IDX_HEADS ** -0.5) * (IDX_DIM ** -0.5)
    for h in range(IDX_HEADS):
        wib_ref[:, h * LANES:(h + 1) * LANES] = jnp.broadcast_to(wi[:, h:h + 1], (tm, LANES))


def _prep(front, pos_col, invf, qg, kg, xg, *, tm):
    s = front.shape[0]
    row = lambda w: pl.BlockSpec((tm, w), lambda i: (i, 0))
    small = lambda shape: pl.BlockSpec(shape, lambda i: (0, 0))
    widths_dtypes = [(ATTN_W, BF16), (ATTN_W, BF16), (ATTN_W, BF16), (XATTN_W, BF16),
                     (IDX_HEADS * LANES, BF16), (LANES, BF16), (IDX_HEADS * LANES, F32)]
    return pl.pallas_call(
        _prep_kernel,
        grid=(s // tm,),
        in_specs=[row(FRONT_W), row(1), small((2, LANES)), small((1, LANES)), small((1, LANES)),
                  small((1, LANES))],
        out_specs=[row(w) for w, _ in widths_dtypes],
        out_shape=[jax.ShapeDtypeStruct((s, w), dt) for w, dt in widths_dtypes],
        compiler_params=_cparams(("parallel",)),
        name="qknorm_rope",
    )(front, pos_col, invf, qg, kg, xg)


def _pool_kernel(u_ref, halo_ref, bd_ref, sc_ref, o_ref, buf_ref):
    i = pl.program_id(0)
    tm = u_ref.shape[0]
    u = u_ref[...]
    buf_ref[0:POOL_HALO, :] = jnp.where(i == 0, 0.0, halo_ref[...])
    buf_ref[POOL_HALO:POOL_HALO + tm, :] = u
    t = i * tm + lax.broadcasted_iota(I32, (tm, 1), 0)
    lane = lax.broadcasted_iota(I32, (1, POOL_W), 1)
    acc = u
    pooled = None
    for d in range(1, max(POOL_WINDOWS)):
        acc = acc + buf_ref[POOL_HALO - d:POOL_HALO - d + tm, :]
        if d + 1 in POOL_WINDOWS:
            g = POOL_WINDOWS.index(d + 1)
            mean = acc / jnp.minimum(t + 1, d + 1).astype(F32)
            in_group = (lane >= g * POOL_GROUP_W) & (lane < (g + 1) * POOL_GROUP_W)
            pooled = jnp.where(in_group, mean, 0.0 if pooled is None else pooled)
    p = (pooled - u).astype(BF16)
    y = jnp.dot(p, bd_ref[...], preferred_element_type=F32) * sc_ref[...]
    o_ref[...] = y.astype(BF16)


def _pool(front, bd, scale, *, tm):
    s = front.shape[0]
    halo_blocks = tm // POOL_HALO
    return pl.pallas_call(
        _pool_kernel,
        grid=(s // tm,),
        in_specs=[
            pl.BlockSpec((tm, POOL_W), lambda i: (i, 0)),
            pl.BlockSpec((POOL_HALO, POOL_W), lambda i: (jnp.maximum(i * halo_blocks - 1, 0), 0)),
            pl.BlockSpec((POOL_W, POOL_W), lambda i: (0, 0)),
            pl.BlockSpec((1, POOL_W), lambda i: (0, 0)),
        ],
        out_specs=pl.BlockSpec((tm, POOL_W), lambda i: (i, 0)),
        out_shape=jax.ShapeDtypeStruct((s, POOL_W), BF16),
        scratch_shapes=[pltpu.VMEM((tm + POOL_HALO, POOL_W), F32)],
        compiler_params=_cparams(("parallel",)),
        name="pool_mixer",
    )(front, front, bd, scale)


def _memkv_kernel(mem_ref, g_ref, w_ref, kg_ref, k_ref, v_ref):
    h = _rms_rows(mem_ref[...], g_ref[...]).astype(BF16)
    kv = jnp.dot(h, w_ref[...], preferred_element_type=F32)
    for hd in range(XATTN_HEADS):
        sl = slice(hd * HEAD_DIM, (hd + 1) * HEAD_DIM)
        k_ref[:, sl] = _rms_rows(kv[:, sl], kg_ref[...]).astype(BF16)
    v_ref[...] = kv[:, XATTN_W:].astype(BF16)


def _memkv(mem, g, w, kg):
    m = mem.shape[0]
    return pl.pallas_call(
        _memkv_kernel,
        out_shape=[jax.ShapeDtypeStruct((m, XATTN_W), BF16), jax.ShapeDtypeStruct((m, XATTN_W), BF16)],
        compiler_params=pltpu.CompilerParams(vmem_limit_bytes=V7X_VMEM_LIMIT_BYTES),
        name="mem_kv",
    )(mem, g, w, kg)


def _cross_kernel(xq_ref, k_ref, v_ref, o_ref):
    for h in range(XATTN_HEADS):
        sl = slice(h * HEAD_DIM, (h + 1) * HEAD_DIM)
        logits = lax.dot_general(xq_ref[:, sl], k_ref[:, sl], NT_DIMS,
                                 preferred_element_type=F32) * (HEAD_DIM ** -0.5)
        e = jnp.exp(logits - jnp.max(logits, axis=-1, keepdims=True))
        p = e / jnp.sum(e, axis=-1, keepdims=True)
        o_ref[:, sl] = jnp.dot(p.astype(BF16), v_ref[:, sl], preferred_element_type=F32).astype(BF16)


def _cross(xq, k_m, v_m, *, tm):
    s = xq.shape[0]
    m = k_m.shape[0]
    return pl.pallas_call(
        _cross_kernel,
        grid=(s // tm,),
        in_specs=[pl.BlockSpec((tm, XATTN_W), lambda i: (i, 0)),
                  pl.BlockSpec((m, XATTN_W), lambda i: (0, 0)),
                  pl.BlockSpec((m, XATTN_W), lambda i: (0, 0))],
        out_specs=pl.BlockSpec((tm, XATTN_W), lambda i: (i, 0)),
        out_shape=jax.ShapeDtypeStruct((s, XATTN_W), BF16),
        compiler_params=_cparams(("parallel",)),
        name="mem_cross_attn",
    )(xq, k_m, v_m)


def _key_to_f32(key):
    bits = key ^ ((key >> 31) & 0x7FFFFFFF)
    return lax.bitcast_convert_type(bits, F32)


def _dsa_kernel(qi_ref, wib_ref, q_ref, ki_ref, k_ref, v_ref, tri_ref, o_ref, sc_ref, lg_ref, *, tq, kc, k_sel):
    i = pl.program_id(0)
    n_chunks = ((i + 1) * tq + kc - 1) // kc
    t = i * tq + lax.broadcasted_iota(I32, (tq, 1), 0)
    slabs = kc // LANES

    def score_chunk(c, carry):
        off = pl.multiple_of(c * kc, kc)
        ki_c = ki_ref[pl.ds(off, kc), :]
        dots = [lax.dot_general(qi_ref[:, h * LANES:(h + 1) * LANES], ki_c, NT_DIMS,
                                preferred_element_type=F32) for h in range(IDX_HEADS)]
        for j in range(slabs):
            s = jnp.zeros((tq, LANES), F32)
            for h in range(IDX_HEADS):
                s = s + jnp.maximum(dots[h][:, j * LANES:(j + 1) * LANES], 0.0) * wib_ref[:, h * LANES:(h + 1) * LANES]
            kpos = off + j * LANES + lax.broadcasted_iota(I32, (1, LANES), 1)
            s = jnp.where(s == 0.0, 0.0, s)
            sc_ref[:, pl.ds(off + j * LANES, LANES)] = jnp.where(kpos <= t, s, -jnp.inf)
        return carry

    lax.fori_loop(0, n_chunks, score_chunk, 0)

    def count(pred):
        def body(c, acc):
            off = pl.multiple_of(c * kc, kc)
            for j in range(slabs):
                acc = acc + pred(sc_ref[:, pl.ds(off + j * LANES, LANES)]).astype(I32)
            return acc
        acc = lax.fori_loop(0, n_chunks, body, jnp.zeros((tq, LANES), I32))
        return jnp.sum(acc, axis=-1, keepdims=True)

    def bisect(b, thr_key):
        cand = thr_key + jnp.left_shift(jnp.int32(1), 31 - b)
        cand_f = _key_to_f32(cand)
        enough = (count(lambda s: s >= cand_f) >= k_sel) | (cand < KEY_NEG_INF)
        return jnp.where(enough, cand, thr_key)

    thr_key = lax.fori_loop(0, 32, bisect, jnp.full((tq, LANES), INT_MIN, I32))
    thr = _key_to_f32(thr_key)
    need = (k_sel - count(lambda s: s > thr)).astype(F32)
    thr_col = thr[:, 0:1]

    def mask_chunk(c, ties_before):
        off = pl.multiple_of(c * kc, kc)
        s = sc_ref[:, pl.ds(off, kc)]
        tie = s == thr_col
        prefix = jnp.dot(jnp.where(tie, 1.0, 0.0).astype(BF16), tri_ref[...], preferred_element_type=F32)
        kpos = off + lax.broadcasted_iota(I32, (1, kc), 1)
        take_tie = tie & (prefix + ties_before <= need)
        sel = ((s > thr_col) | take_tie) & (kpos <= t)
        sc_ref[:, pl.ds(off, kc)] = jnp.where(sel, 0.0, NEG_BIG)
        return ties_before + prefix[:, kc - 1:kc]

    lax.fori_loop(0, n_chunks, mask_chunk, jnp.zeros((tq, 1), F32))

    scale2 = (HEAD_DIM ** -0.5) * math.log2(math.e)
    heads = [slice(h * HEAD_DIM, (h + 1) * HEAD_DIM) for h in range(ATTN_HEADS)]

    def unshifted_chunk(c, carry):
        off = pl.multiple_of(c * kc, kc)
        bias = sc_ref[:, pl.ds(off, kc)]
        out = []
        for sl, (lane_sum, acc) in zip(heads, carry):
            p = jnp.exp2(lax.dot_general(q_ref[:, sl], k_ref[pl.ds(off, kc), sl], NT_DIMS,
                                         preferred_element_type=F32) * scale2 + bias)
            for j in range(slabs):
                lane_sum = lane_sum + p[:, j * LANES:(j + 1) * LANES]
            acc = acc + jnp.dot(p.astype(BF16), v_ref[pl.ds(off, kc), sl], preferred_element_type=F32)
            out.append((lane_sum, acc))
        return tuple(out)

    zeros = jnp.zeros((tq, LANES), F32)
    result = lax.fori_loop(0, n_chunks, unshifted_chunk, tuple((zeros, zeros) for _ in heads))
    in_range = None
    for sl, (lane_sum, acc) in zip(heads, result):
        row_sum = jnp.sum(lane_sum, axis=-1, keepdims=True)
        o_ref[:, sl] = (acc / row_sum).astype(BF16)
        ok = (row_sum >= SOFTMAX_SUM_MIN) & (row_sum <= SOFTMAX_SUM_MAX)
        in_range = ok if in_range is None else (in_range & ok)
    n_bad = jnp.sum(jnp.where(in_range, 0.0, 1.0))

    @pl.when(n_bad > 0.0)
    def _():
        _shifted_attention(q_ref, k_ref, v_ref, sc_ref, lg_ref, o_ref, n_chunks, tq=tq, kc=kc, scale2=scale2)


def _shifted_attention(q_ref, k_ref, v_ref, sc_ref, lg_ref, o_ref, n_chunks, *, tq, kc, scale2):
    slabs = kc // LANES
    for h in range(ATTN_HEADS):
        sl = slice(h * HEAD_DIM, (h + 1) * HEAD_DIM)

        def logits_chunk(c, lane_max, sl=sl):
            off = pl.multiple_of(c * kc, kc)
            s = lax.dot_general(q_ref[:, sl], k_ref[pl.ds(off, kc), sl], NT_DIMS,
                                preferred_element_type=F32) * scale2 + sc_ref[:, pl.ds(off, kc)]
            lg_ref[:, pl.ds(off, kc)] = s
            for j in range(slabs):
                lane_max = jnp.maximum(lane_max, s[:, j * LANES:(j + 1) * LANES])
            return lane_max

        lane_max = lax.fori_loop(0, n_chunks, logits_chunk, jnp.full((tq, LANES), NEG_BIG, F32))
        row_max = jnp.max(lane_max, axis=-1, keepdims=True)

        def pv_chunk(c, carry, sl=sl, row_max=row_max):
            lane_sum, acc = carry
            off = pl.multiple_of(c * kc, kc)
            p = jnp.exp2(lg_ref[:, pl.ds(off, kc)] - row_max)
            for j in range(slabs):
                lane_sum = lane_sum + p[:, j * LANES:(j + 1) * LANES]
            acc = acc + jnp.dot(p.astype(BF16), v_ref[pl.ds(off, kc), sl], preferred_element_type=F32)
            return lane_sum, acc

        lane_sum, acc = lax.fori_loop(0, n_chunks, pv_chunk,
                                      (jnp.zeros((tq, LANES), F32), jnp.zeros((tq, HEAD_DIM), F32)))
        o_ref[:, sl] = (acc / jnp.sum(lane_sum, axis=-1, keepdims=True)).astype(BF16)


def _dsa(qi, wib, q, ki, k, v, *, tq, kc, k_sel):
    s = q.shape[0]
    tri = jnp.triu(jnp.ones((kc, kc), BF16))
    row = lambda w: pl.BlockSpec((tq, w), lambda i: (i, 0))
    return pl.pallas_call(
        functools.partial(_dsa_kernel, tq=tq, kc=kc, k_sel=k_sel),
        grid=(s // tq,),
        in_specs=[row(IDX_HEADS * LANES), row(IDX_HEADS * LANES), row(ATTN_W),
                  _resident((s, LANES)), _resident((s, ATTN_W)), _resident((s, ATTN_W)),
                  _resident((kc, kc))],
        out_specs=row(ATTN_W),
        out_shape=jax.ShapeDtypeStruct((s, ATTN_W), BF16),
        scratch_shapes=[pltpu.VMEM((tq, s), F32), pltpu.VMEM((tq, s), F32)],
        compiler_params=_cparams(("parallel",)),
        name="dsa_attention",
    )(qi, wib, q, ki, k, v, tri)


def _route(logits):
    lane = lax.broadcasted_iota(I32, (1, LANES), 1).astype(F32)
    far = float(LANES)
    is_group = lane < N_GROUPS
    g_max = jnp.max(jnp.where(is_group, logits, -jnp.inf), axis=-1, keepdims=True)
    g_top = jnp.min(jnp.where(is_group & (logits == g_max), lane, far), axis=-1, keepdims=True)
    pg_top = 1.0 / jnp.sum(jnp.where(is_group, jnp.exp(logits - g_max), 0.0), axis=-1, keepdims=True)
    first = N_GROUPS + EXPERTS_PER_GROUP * g_top
    in_grp = (lane >= first) & (lane < first + EXPERTS_PER_GROUP)
    e_max = jnp.max(jnp.where(in_grp, logits, -jnp.inf), axis=-1, keepdims=True)
    e_exp = jnp.where(in_grp, jnp.exp(logits - e_max), 0.0)
    pe = jnp.where(in_grp, e_exp / jnp.sum(e_exp, axis=-1, keepdims=True), -1.0)
    p1 = jnp.max(pe, axis=-1, keepdims=True)
    e1 = jnp.min(jnp.where(pe == p1, lane, far), axis=-1, keepdims=True)
    pe_rest = jnp.where(lane == e1, -1.0, pe)
    p2 = jnp.max(pe_rest, axis=-1, keepdims=True)
    e2 = jnp.min(jnp.where(pe_rest == p2, lane, far), axis=-1, keepdims=True)
    den = p1 + p2
    w1 = pg_top * p1 / den
    w2 = pg_top * p2 / den
    lo = jnp.minimum(e1, e2) - first
    hi = jnp.maximum(e1, e2) - first
    pair = lo * (2 * EXPERTS_PER_GROUP - 1 - lo) * 0.5 + (hi - lo - 1.0)
    first_is_lo = e1 < e2
    return (g_top * PAIRS_PER_GROUP + pair, jnp.where(first_is_lo, w1, w2), jnp.where(first_is_lo, w2, w1))


def _merge_kernel(x_ref, p_ref, a_ref, c_ref, gt_ref, wpo_ref, wao_ref, wco_ref, wo_ref, gf_ref, wr_ref,
                  tri_ref, x1_ref, route_ref, cnt_ref):
    @pl.when(pl.program_id(0) == 0)
    def _():
        cnt_ref[...] = jnp.zeros_like(cnt_ref)

    d = x_ref.shape[1]
    merged = gt_ref[:, 0:d].astype(F32) * jnp.dot(p_ref[...], wpo_ref[...], preferred_element_type=F32)
    merged += gt_ref[:, d:2 * d].astype(F32) * jnp.dot(a_ref[...], wao_ref[...], preferred_element_type=F32)
    merged += gt_ref[:, 2 * d:3 * d].astype(F32) * jnp.dot(c_ref[...], wco_ref[...], preferred_element_type=F32)
    x1 = x_ref[...] + jnp.dot(merged.astype(BF16), wo_ref[...], preferred_element_type=F32)
    x1_ref[...] = x1
    h2 = _rms_rows(x1, gf_ref[...]).astype(BF16)
    bucket, w_lo, w_hi = _route(jnp.dot(h2, wr_ref[...], preferred_element_type=F32))
    lane = lax.broadcasted_iota(I32, (1, LANES), 1).astype(F32)
    onehot = lane == bucket
    before = jnp.dot(tri_ref[...], jnp.where(onehot, 1.0, 0.0).astype(BF16), preferred_element_type=F32)
    rank = jnp.sum(jnp.where(onehot, before + cnt_ref[...], 0.0), axis=-1, keepdims=True)
    cnt_ref[...] += jnp.sum(jnp.where(onehot, 1.0, 0.0), axis=0, keepdims=True)
    route_ref[...] = jnp.where(lane == ROUTE_BUCKET, bucket,
                               jnp.where(lane == ROUTE_RANK, rank,
                                         jnp.where(lane == ROUTE_W_LO, w_lo,
                                                   jnp.where(lane == ROUTE_W_HI, w_hi, 0.0))))


def _merge(x, p, a, c, gates, wpo, wao, wco, wo, gf, wr, *, tm):
    s, d = x.shape
    row = lambda w: pl.BlockSpec((tm, w), lambda i: (i, 0))
    tri = jnp.tril(jnp.ones((tm, tm), BF16), -1)
    return pl.pallas_call(
        _merge_kernel,
        grid=(s // tm,),
        in_specs=[row(d), row(POOL_W), row(ATTN_W), row(XATTN_W), row(N_BRANCH * d),
                  _resident(wpo.shape), _resident(wao.shape), _resident(wco.shape), _resident(wo.shape),
                  _resident(gf.shape), _resident(wr.shape), _resident(tri.shape)],
        out_specs=[row(d), row(LANES)],
        out_shape=[jax.ShapeDtypeStruct((s, d), F32), jax.ShapeDtypeStruct((s, LANES), F32)],
        scratch_shapes=[pltpu.VMEM((1, LANES), F32)],
        compiler_params=_cparams(("arbitrary",)),
        name="gated_merge_router",
    )(x, p, a, c, gates, wpo, wao, wco, wo, gf, wr, tri)


def _moe_kernel(src_ref, elo_ref, ehi_ref, nvalid_ref, ntiles_ref,
                x1_hbm, w_ref, gf_ref, wg_lo, wu_lo, wd_lo, wg_hi, wu_hi, wd_hi,
                out_hbm, xbuf, obuf, gsem, ssem, *, tile):
    j = pl.program_id(0)
    n_tiles = ntiles_ref[0]
    slot = lax.rem(j, 2)

    def start_gather(t, s):
        def body(r, carry):
            tok = src_ref[t * tile + r]
            pltpu.make_async_copy(x1_hbm.at[pl.ds(tok, 1)], xbuf.at[s, pl.ds(r, 1)], gsem.at[s]).start()
            return carry
        lax.fori_loop(0, tile, body, 0, unroll=8)

    def wait_scatter(t, s):
        n = nvalid_ref[t]
        n_whole = pl.multiple_of(lax.shift_left(lax.shift_right_logical(n, 3), 3), SUBLANES)

        @pl.when(n_whole > 0)
        def _():
            pltpu.make_async_copy(obuf.at[s, pl.ds(0, n_whole)], out_hbm.at[pl.ds(0, n_whole)], ssem.at[s]).wait()

        def wait_row(r, carry):
            pltpu.make_async_copy(obuf.at[s, pl.ds(0, 1)], out_hbm.at[pl.ds(0, 1)], ssem.at[s]).wait()
            return carry
        lax.fori_loop(0, n - n_whole, wait_row, 0)

    @pl.when(j == 0)
    def _():
        start_gather(0, 0)

    @pl.when(j + 1 < n_tiles)
    def _():
        start_gather(j + 1, 1 - slot)

    @pl.when(j < n_tiles)
    def _():
        pltpu.make_async_copy(x1_hbm.at[pl.ds(0, tile)], xbuf.at[slot], gsem.at[slot]).wait()
        x1 = xbuf[slot]
        h = _rms_rows(x1, gf_ref[...]).astype(BF16)
        y = x1
        for w_col, wg, wu, wd in ((ROUTE_W_LO, wg_lo, wu_lo, wd_lo), (ROUTE_W_HI, wg_hi, wu_hi, wd_hi)):
            a = jnp.dot(h, wg[0], preferred_element_type=F32)
            b = jnp.dot(h, wu[0], preferred_element_type=F32)
            act = (a * _sigmoid(a)) * b * w_ref[:, w_col:w_col + 1]
            y = y + jnp.dot(act.astype(BF16), wd[0], preferred_element_type=F32)
        obuf[slot] = y

        def scatter_row(r, carry):
            tok = src_ref[j * tile + r]
            pltpu.make_async_copy(obuf.at[slot, pl.ds(r, 1)], out_hbm.at[pl.ds(tok, 1)], ssem.at[slot]).start()
            return carry
        lax.fori_loop(0, nvalid_ref[j], scatter_row, 0)

        @pl.when(j >= 1)
        def _():
            wait_scatter(j - 1, 1 - slot)

        @pl.when(j == n_tiles - 1)
        def _():
            wait_scatter(j, slot)


def _moe(x1, route, gf, wg, wu, wd, *, tile):
    s, d = x1.shape
    ff = wg.shape[2]
    n_buckets = N_GROUPS * PAIRS_PER_GROUP
    max_tiles = (s + n_buckets * (tile - 1)) // tile
    bucket = route[:, ROUTE_BUCKET].astype(I32)
    rank = route[:, ROUTE_RANK].astype(I32)
    counts = jnp.zeros((n_buckets,), I32).at[bucket].add(1)
    padded = (counts + tile - 1) // tile * tile
    ends = jnp.cumsum(padded)
    starts = ends - padded
    dest = starts[bucket] + rank
    src = jnp.zeros((max_tiles * tile,), I32).at[dest].set(jnp.arange(s, dtype=I32))
    w_sorted = jnp.zeros((max_tiles * tile, ROUTE_COLS), F32).at[dest].set(route[:, :ROUTE_COLS])
    n_tiles = ends[-1] // tile
    tile_start = jnp.arange(max_tiles, dtype=I32) * tile
    tile_bucket = jnp.searchsorted(ends, jnp.minimum(tile_start, ends[-1] - tile), side="right").astype(I32)
    nvalid = jnp.where(tile_start < ends[-1],
                       jnp.clip(counts[tile_bucket] - (tile_start - starts[tile_bucket]), 0, tile), 0).astype(I32)
    pair_lo = jnp.array([0, 0, 0, 1, 1, 2], I32)
    pair_hi = jnp.array([1, 2, 3, 2, 3, 3], I32)
    group = tile_bucket // PAIRS_PER_GROUP
    e_lo = group * EXPERTS_PER_GROUP + pair_lo[tile_bucket % PAIRS_PER_GROUP]
    e_hi = group * EXPERTS_PER_GROUP + pair_hi[tile_bucket % PAIRS_PER_GROUP]

    lo_w = lambda shape: pl.BlockSpec(shape, lambda j, src, elo, ehi, nv, nt: (elo[j], 0, 0))
    hi_w = lambda shape: pl.BlockSpec(shape, lambda j, src, elo, ehi, nv, nt: (ehi[j], 0, 0))
    grid_spec = pltpu.PrefetchScalarGridSpec(
        num_scalar_prefetch=5,
        grid=(max_tiles,),
        in_specs=[pl.BlockSpec(memory_space=pl.ANY),
                  pl.BlockSpec((tile, ROUTE_COLS), lambda j, *_: (j, 0)),
                  pl.BlockSpec((1, d), lambda j, *_: (0, 0)),
                  lo_w((1, d, ff)), lo_w((1, d, ff)), lo_w((1, ff, d)),
                  hi_w((1, d, ff)), hi_w((1, d, ff)), hi_w((1, ff, d))],
        out_specs=pl.BlockSpec(memory_space=pl.ANY),
        scratch_shapes=[pltpu.VMEM((2, tile, d), F32), pltpu.VMEM((2, tile, d), F32),
                        pltpu.SemaphoreType.DMA((2,)), pltpu.SemaphoreType.DMA((2,))],
    )
    return pl.pallas_call(
        functools.partial(_moe_kernel, tile=tile),
        grid_spec=grid_spec,
        out_shape=jax.ShapeDtypeStruct((s, d), F32),
        compiler_params=_cparams(("arbitrary",)),
        name="moe_experts",
    )(src, e_lo, e_hi, nvalid, n_tiles.reshape(1), x1, w_sorted, gf, wg, wu, wd, wg, wu, wd)


def _front_weight(w_in):
    d = w_in.shape[0]
    sizes = (POOL_W, ATTN_W, ATTN_W, ATTN_W, IDX_HEADS * IDX_DIM, IDX_DIM, IDX_HEADS, XATTN_W)
    offs = [0]
    for sz in sizes:
        offs.append(offs[-1] + sz)
    w_pool, w_q, w_k, w_v, w_qi, w_ki, w_wi, w_xq = (w_in[:, offs[n]:offs[n + 1]] for n in range(len(sizes)))
    pad = lambda w, to: jnp.pad(w, ((0, 0), (0, to - w.shape[1])))
    w_qi = jnp.pad(w_qi.reshape(d, IDX_HEADS, IDX_DIM), ((0, 0), (0, 0), (0, LANES - IDX_DIM)))
    front = jnp.concatenate([w_pool, w_q, w_k, w_v, w_xq, w_qi.reshape(d, IDX_HEADS * LANES),
                             pad(w_ki, LANES), pad(w_wi, LANES)], axis=1)
    return front, w_in[:, offs[-1]:]


def _rope_inv_freq():
    def inv(dim):
        return ROPE_THETA ** (-jnp.arange(0, dim, 2, dtype=F32) / dim)
    return jnp.stack([jnp.tile(inv(HEAD_DIM), LANES // (HEAD_DIM // 2)),
                      jnp.tile(inv(IDX_DIM), LANES // (IDX_DIM // 2))])


def _layer(x, mem, pos, g_mix, w_in, b_gate, w_pool_grp, pool_scale, q_norm_g, k_norm_g, g_mem, w_mem_kv,
           xq_norm_g, xk_norm_g, w_pool_out, w_attn_out, w_cross_out, w_o, g_ffn, w_router_group,
           w_router_expert, w_e_gate, w_e_up, w_e_down):
    s, d = x.shape
    row2 = lambda v: v.reshape(1, -1)
    k_sel = min(TOPK_MAX, s // 4)

    w_front, w_gate = _front_weight(w_in)
    front = _norm_proj(x, row2(g_mix), w_front.astype(BF16), None, tm=512, tn=FRONT_W // 2, out_dtype=F32)
    gates = _norm_proj(x, row2(g_mix), w_gate.astype(BF16), row2(b_gate), tm=1024, tn=1024, out_dtype=BF16)

    q, k, v, xq, qi, ki, wib = _prep(front, pos.reshape(s, 1), _rope_inv_freq(), row2(q_norm_g),
                                     row2(k_norm_g), row2(xq_norm_g), tm=512)

    bd = jax.scipy.linalg.block_diag(*[w_pool_grp[g] for g in range(len(POOL_WINDOWS))])
    pool_pre = _pool(front, bd.astype(BF16), row2(pool_scale), tm=512)

    k_m, v_m = _memkv(mem, row2(g_mem), w_mem_kv.astype(BF16), row2(xk_norm_g))
    cross = _cross(xq, k_m, v_m, tm=512)

    attn = _dsa(qi, wib, q, ki, k, v, tq=128, kc=512, k_sel=k_sel)

    w_router = jnp.pad(jnp.concatenate([w_router_group, w_router_expert], axis=1),
                       ((0, 0), (0, LANES - N_GROUPS - N_EXPERTS)))
    x1, route = _merge(x, pool_pre, attn, cross, gates, w_pool_out.astype(BF16), w_attn_out.astype(BF16),
                       w_cross_out.astype(BF16), w_o.astype(BF16), row2(g_ffn), w_router.astype(BF16), tm=256)
    return _moe(x1, route, row2(g_ffn), w_e_gate.astype(BF16), w_e_up.astype(BF16), w_e_down.astype(BF16),
                tile=256)


def kernel(x, mem, positions, g_mix, w_in, b_gate, w_pool_grp, pool_scale, q_norm_g, k_norm_g, g_mem, w_mem_kv,
           xq_norm_g, xk_norm_g, w_pool_out, w_attn_out, w_cross_out, w_o, g_ffn, w_router_group,
           w_router_expert, w_e_gate, w_e_up, w_e_down):
    depth = g_mix.shape[0]
    outs = []
    for b in range(x.shape[0]):
        xb = x[b]
        for l in range(depth):
            xb = _layer(xb, mem[b], positions[b], g_mix[l], w_in[l], b_gate[l], w_pool_grp[l], pool_scale[l],
                        q_norm_g[l], k_norm_g[l], g_mem[l], w_mem_kv[l], xq_norm_g[l], xk_norm_g[l],
                        w_pool_out[l], w_attn_out[l], w_cross_out[l], w_o[l], g_ffn[l], w_router_group[l],
                        w_router_expert[l], w_e_gate[l], w_e_up[l], w_e_down[l])
        outs.append(xb)
    return jnp.stack(outs)
```

```python
import functools
import math

import jax
import jax.numpy as jnp
from jax import lax
from jax.experimental import pallas as pl
from jax.experimental.pallas import tpu as pltpu

F32 = jnp.float32
BF16 = jnp.bfloat16
I32 = jnp.int32
I16 = jnp.int16
I16_MIN = -32768

NORM_EPS = 1e-6
ROPE_THETA = 10000.0
HEAD_DIM = 128
ATTN_HEADS = 6
ATTN_W = ATTN_HEADS * HEAD_DIM
IDX_HEADS = 4
IDX_DIM = 64
XATTN_HEADS = 4
XATTN_W = XATTN_HEADS * HEAD_DIM
POOL_WINDOWS = (2, 4, 8, 16)
POOL_GROUP_W = 192
POOL_W = len(POOL_WINDOWS) * POOL_GROUP_W
POOL_HALO = 16
TOPK_MAX = 256
N_GROUPS = 4
EXPERTS_PER_GROUP = 4
N_EXPERTS = N_GROUPS * EXPERTS_PER_GROUP
N_BRANCH = 3
PAIRS_PER_GROUP = EXPERTS_PER_GROUP * (EXPERTS_PER_GROUP - 1) // 2
ROUTE_BUCKET, ROUTE_RANK, ROUTE_W_LO, ROUTE_W_HI, ROUTE_COLS = 0, 1, 2, 3, 4

LANES = 128
SUBLANES = 8
V7X_VMEM_LIMIT_BYTES = 56 * 1024 * 1024

COL_POOL = 0
COL_Q = COL_POOL + POOL_W
COL_K = COL_Q + ATTN_W
COL_V = COL_K + ATTN_W
COL_XQ = COL_V + ATTN_W
COL_QI = COL_XQ + XATTN_W
COL_KI = COL_QI + IDX_HEADS * LANES
COL_WI = COL_KI + LANES
FRONT_W = COL_WI + LANES

NEG_BIG = -1e30
INT_MIN = -2147483648
KEY_NEG_INF = -2139095041
SOFTMAX_SUM_MIN = 2.0 ** -64
SOFTMAX_SUM_MAX = 2.0 ** 100
NT_DIMS = (((1,), (1,)), ((), ()))


def _cparams(semantics):
    return pltpu.CompilerParams(dimension_semantics=semantics, vmem_limit_bytes=V7X_VMEM_LIMIT_BYTES)


def _resident(shape):
    return pl.BlockSpec(shape, lambda *_: (0,) * len(shape), pipeline_mode=pl.Buffered(1))


def _rms_rows(x, g):
    ms = jnp.mean(x * x, axis=-1, keepdims=True)
    return x * lax.rsqrt(ms + NORM_EPS) * g


def _sigmoid(x):
    return 1.0 / (1.0 + jnp.exp(-x))


def _proj_kernel(x_ref, g_ref, w_ref, o_ref, h_ref):
    @pl.when(pl.program_id(1) == 0)
    def _():
        h_ref[...] = _rms_rows(x_ref[...], g_ref[...]).astype(BF16)

    o_ref[...] = jnp.dot(h_ref[...], w_ref[...], preferred_element_type=F32).astype(o_ref.dtype)


def _gate_proj_kernel(x_ref, g_ref, w_ref, b_ref, o_ref, h_ref):
    @pl.when(pl.program_id(1) == 0)
    def _():
        h_ref[...] = _rms_rows(x_ref[...], g_ref[...]).astype(BF16)

    acc = jnp.dot(h_ref[...], w_ref[...], preferred_element_type=F32)
    o_ref[...] = _sigmoid(acc + b_ref[...]).astype(o_ref.dtype)


def _norm_proj(x, g, w, bias, *, tm, tn, out_dtype):
    s, d = x.shape
    n = w.shape[1]
    in_specs = [
        pl.BlockSpec((tm, d), lambda i, j: (i, 0)),
        pl.BlockSpec((1, d), lambda i, j: (0, 0)),
        pl.BlockSpec((d, tn), lambda i, j: (0, j)),
    ]
    args = [x, g, w]
    body = _proj_kernel
    if bias is not None:
        in_specs.append(pl.BlockSpec((1, tn), lambda i, j: (0, j)))
        args.append(bias)
        body = _gate_proj_kernel
    return pl.pallas_call(
        body,
        grid=(s // tm, n // tn),
        in_specs=in_specs,
        out_specs=pl.BlockSpec((tm, tn), lambda i, j: (i, j)),
        out_shape=jax.ShapeDtypeStruct((s, n), out_dtype),
        scratch_shapes=[pltpu.VMEM((tm, d), BF16)],
        compiler_params=_cparams(("parallel", "arbitrary")),
        name="norm_proj_gate" if bias is not None else "norm_proj_front",
    )(*args)


def _prep_kernel(f_ref, pos_ref, invf_ref, qg_ref, kg_ref, xg_ref,
                 q_ref, kt_ref, v_ref, xq_ref, qi_ref, kit_ref, wib_ref):
    tm = f_ref.shape[0]
    pos = pos_ref[...].astype(F32)
    lane = lax.broadcasted_iota(I32, (1, LANES), 1)
    ang = pos * invf_ref[0:1, :]
    cos_a = jnp.cos(ang)
    sin_a = jnp.where(lane < HEAD_DIM // 2, -jnp.sin(ang), jnp.sin(ang))
    ang_i = pos * invf_ref[1:2, :]
    low_i = (lane & (IDX_DIM - 1)) < IDX_DIM // 2
    cos_i = jnp.cos(ang_i)
    sin_i = jnp.where(low_i, -jnp.sin(ang_i), jnp.sin(ang_i))

    def rope_head(y):
        return y * cos_a + pltpu.roll(y, HEAD_DIM // 2, 1) * sin_a

    def rope_idx(y):
        partner = jnp.where(low_i, pltpu.roll(y, LANES - IDX_DIM // 2, 1), pltpu.roll(y, IDX_DIM // 2, 1))
        return y * cos_i + partner * sin_i

    for h in range(ATTN_HEADS):
        sl = slice(h * HEAD_DIM, (h + 1) * HEAD_DIM)
        qh = f_ref[:, COL_Q + h * HEAD_DIM:COL_Q + (h + 1) * HEAD_DIM]
        kh = f_ref[:, COL_K + h * HEAD_DIM:COL_K + (h + 1) * HEAD_DIM]
        q_ref[:, sl] = rope_head(_rms_rows(qh, qg_ref[...])).astype(BF16)
        kt_ref[sl, :] = rope_head(_rms_rows(kh, kg_ref[...])).T.astype(BF16)
    v_ref[...] = f_ref[:, COL_V:COL_V + ATTN_W].astype(BF16)
    for h in range(XATTN_HEADS):
        sl = slice(h * HEAD_DIM, (h + 1) * HEAD_DIM)
        xh = f_ref[:, COL_XQ + h * HEAD_DIM:COL_XQ + (h + 1) * HEAD_DIM]
        xq_ref[:, sl] = _rms_rows(xh, xg_ref[...]).astype(BF16)
    for h in range(IDX_HEADS):
        sl = slice(h * LANES, (h + 1) * LANES)
        qi_ref[:, sl] = rope_idx(f_ref[:, COL_QI + h * LANES:COL_QI + (h + 1) * LANES]).astype(BF16)
    kit_ref[...] = rope_idx(f_ref[:, COL_KI:COL_KI + LANES]).T.astype(BF16)
    wi = f_ref[:, COL_WI:COL_WI + LANES] * (IDX_HEADS ** -0.5) * (IDX_DIM ** -0.5)
    for h in range(IDX_HEADS):
        wib_ref[:, h * LANES:(h + 1) * LANES] = jnp.broadcast_to(wi[:, h:h + 1], (tm, LANES))


def _prep(front, pos_col, invf, qg, kg, xg, *, tm):
    s = front.shape[0]
    row = lambda w: pl.BlockSpec((tm, w), lambda i: (i, 0))
    small = lambda shape: pl.BlockSpec(shape, lambda i: (0, 0))
    col = lambda w: pl.BlockSpec((w, tm), lambda i: (0, i))
    outs = [(row(ATTN_W), (s, ATTN_W), BF16), (col(ATTN_W), (ATTN_W, s), BF16), (row(ATTN_W), (s, ATTN_W), BF16),
            (row(XATTN_W), (s, XATTN_W), BF16), (row(IDX_HEADS * LANES), (s, IDX_HEADS * LANES), BF16),
            (col(LANES), (LANES, s), BF16), (row(IDX_HEADS * LANES), (s, IDX_HEADS * LANES), F32)]
    return pl.pallas_call(
        _prep_kernel,
        grid=(s // tm,),
        in_specs=[row(FRONT_W), row(1), small((2, LANES)), small((1, LANES)), small((1, LANES)),
                  small((1, LANES))],
        out_specs=[spec for spec, _, _ in outs],
        out_shape=[jax.ShapeDtypeStruct(shape, dt) for _, shape, dt in outs],
        compiler_params=_cparams(("parallel",)),
        name="qknorm_rope",
    )(front, pos_col, invf, qg, kg, xg)


def _pool_kernel(u_ref, halo_ref, bd_ref, sc_ref, o_ref, buf_ref):
    i = pl.program_id(0)
    tm = u_ref.shape[0]
    u = u_ref[...]
    buf_ref[0:POOL_HALO, :] = jnp.where(i == 0, 0.0, halo_ref[...])
    buf_ref[POOL_HALO:POOL_HALO + tm, :] = u
    t = i * tm + lax.broadcasted_iota(I32, (tm, 1), 0)
    lane = lax.broadcasted_iota(I32, (1, POOL_W), 1)
    acc = u
    pooled = None
    for d in range(1, max(POOL_WINDOWS)):
        acc = acc + buf_ref[POOL_HALO - d:POOL_HALO - d + tm, :]
        if d + 1 in POOL_WINDOWS:
            g = POOL_WINDOWS.index(d + 1)
            mean = acc / jnp.minimum(t + 1, d + 1).astype(F32)
            in_group = (lane >= g * POOL_GROUP_W) & (lane < (g + 1) * POOL_GROUP_W)
            pooled = jnp.where(in_group, mean, 0.0 if pooled is None else pooled)
    p = (pooled - u).astype(BF16)
    y = jnp.dot(p, bd_ref[...], preferred_element_type=F32) * sc_ref[...]
    o_ref[...] = y.astype(BF16)


def _pool(front, bd, scale, *, tm):
    s = front.shape[0]
    halo_blocks = tm // POOL_HALO
    return pl.pallas_call(
        _pool_kernel,
        grid=(s // tm,),
        in_specs=[
            pl.BlockSpec((tm, POOL_W), lambda i: (i, 0)),
            pl.BlockSpec((POOL_HALO, POOL_W), lambda i: (jnp.maximum(i * halo_blocks - 1, 0), 0)),
            pl.BlockSpec((POOL_W, POOL_W), lambda i: (0, 0)),
            pl.BlockSpec((1, POOL_W), lambda i: (0, 0)),
        ],
        out_specs=pl.BlockSpec((tm, POOL_W), lambda i: (i, 0)),
        out_shape=jax.ShapeDtypeStruct((s, POOL_W), BF16),
        scratch_shapes=[pltpu.VMEM((tm + POOL_HALO, POOL_W), F32)],
        compiler_params=_cparams(("parallel",)),
        name="pool_mixer",
    )(front, front, bd, scale)


def _memkv_kernel(mem_ref, g_ref, w_ref, kg_ref, k_ref, v_ref):
    h = _rms_rows(mem_ref[...], g_ref[...]).astype(BF16)
    kv = jnp.dot(h, w_ref[...], preferred_element_type=F32)
    for hd in range(XATTN_HEADS):
        sl = slice(hd * HEAD_DIM, (hd + 1) * HEAD_DIM)
        k_ref[:, sl] = _rms_rows(kv[:, sl], kg_ref[...]).astype(BF16)
    v_ref[...] = kv[:, XATTN_W:].astype(BF16)


def _memkv(mem, g, w, kg):
    m = mem.shape[0]
    return pl.pallas_call(
        _memkv_kernel,
        out_shape=[jax.ShapeDtypeStruct((m, XATTN_W), BF16), jax.ShapeDtypeStruct((m, XATTN_W), BF16)],
        compiler_params=pltpu.CompilerParams(vmem_limit_bytes=V7X_VMEM_LIMIT_BYTES),
        name="mem_kv",
    )(mem, g, w, kg)


def _cross_kernel(xq_ref, k_ref, v_ref, o_ref):
    for h in range(XATTN_HEADS):
        sl = slice(h * HEAD_DIM, (h + 1) * HEAD_DIM)
        logits = lax.dot_general(xq_ref[:, sl], k_ref[:, sl], NT_DIMS,
                                 preferred_element_type=F32) * (HEAD_DIM ** -0.5)
        e = jnp.exp(logits - jnp.max(logits, axis=-1, keepdims=True))
        p = e / jnp.sum(e, axis=-1, keepdims=True)
        o_ref[:, sl] = jnp.dot(p.astype(BF16), v_ref[:, sl], preferred_element_type=F32).astype(BF16)


def _cross(xq, k_m, v_m, *, tm):
    s = xq.shape[0]
    m = k_m.shape[0]
    return pl.pallas_call(
        _cross_kernel,
        grid=(s // tm,),
        in_specs=[pl.BlockSpec((tm, XATTN_W), lambda i: (i, 0)),
                  pl.BlockSpec((m, XATTN_W), lambda i: (0, 0)),
                  pl.BlockSpec((m, XATTN_W), lambda i: (0, 0))],
        out_specs=pl.BlockSpec((tm, XATTN_W), lambda i: (i, 0)),
        out_shape=jax.ShapeDtypeStruct((s, XATTN_W), BF16),
        compiler_params=_cparams(("parallel",)),
        name="mem_cross_attn",
    )(xq, k_m, v_m)


def _key_to_f32(key):
    bits = key ^ ((key >> 31) & 0x7FFFFFFF)
    return lax.bitcast_convert_type(bits, F32)


def _dsa_kernel(qi_ref, wib_ref, q_ref, kit_ref, kt_ref, v_ref, tri_ref, o_ref, sc_ref, hi_ref, lo_ref,
                *, tq, kc, k_sel):
    i = pl.program_id(0)
    n_chunks = ((i + 1) * tq + kc - 1) // kc
    t = i * tq + lax.broadcasted_iota(I32, (tq, 1), 0)
    slabs = kc // LANES

    def slab(c, j):
        return pl.ds(pl.multiple_of(c * kc, kc) + j * LANES, LANES)

    def score_chunk(c, carry):
        off = pl.multiple_of(c * kc, kc)
        ki_c = kit_ref[:, pl.ds(off, kc)]
        dots = [jnp.dot(qi_ref[:, h * LANES:(h + 1) * LANES], ki_c, preferred_element_type=F32)
                for h in range(IDX_HEADS)]
        for j in range(slabs):
            s = jnp.zeros((tq, LANES), F32)
            for h in range(IDX_HEADS):
                s = s + jnp.maximum(dots[h][:, j * LANES:(j + 1) * LANES], 0.0) * wib_ref[:, h * LANES:(h + 1) * LANES]
            kpos = off + j * LANES + lax.broadcasted_iota(I32, (1, LANES), 1)
            s = jnp.where(kpos <= t, jnp.where(s == 0.0, 0.0, s), -jnp.inf)
            sc_ref[:, slab(c, j)] = s
            bits = pltpu.bitcast(s, I32)
            key = bits ^ ((bits >> 31) & 0x7FFFFFFF)
            hi_ref[:, slab(c, j)] = (key >> 16).astype(I16)
            lo_ref[:, slab(c, j)] = ((key & 0xFFFF) + I16_MIN).astype(I16)
        return carry

    lax.fori_loop(0, n_chunks, score_chunk, 0)

    def count16(ref, cand16):
        def body(c, acc):
            for j in range(slabs):
                acc = acc + jnp.where(ref[:, slab(c, j)] >= cand16, jnp.int16(1), jnp.int16(0))
            return acc
        acc = lax.fori_loop(0, n_chunks, body, jnp.zeros((tq, LANES), I16))
        return jnp.sum(acc.astype(I32), axis=-1, keepdims=True)

    def search16(ref, counted_already):
        def step(b, best):
            cand = best + jnp.left_shift(jnp.int32(1), 15 - b)
            enough = count16(ref, cand.astype(I16)) + counted_already >= k_sel
            return jnp.where(enough, cand, best)
        return lax.fori_loop(0, 16, step, jnp.full((tq, LANES), I16_MIN, I32))

    hi_best = search16(hi_ref, 0)
    hi_best16 = hi_best.astype(I16)

    def split_chunk(c, acc):
        for j in range(slabs):
            h = hi_ref[:, slab(c, j)]
            lo_ref[:, slab(c, j)] = jnp.where(h == hi_best16, lo_ref[:, slab(c, j)], jnp.int16(I16_MIN))
            acc = acc + jnp.where(h > hi_best16, jnp.int16(1), jnp.int16(0))
        return acc

    above = lax.fori_loop(0, n_chunks, split_chunk, jnp.zeros((tq, LANES), I16))
    lo_best = search16(lo_ref, jnp.sum(above.astype(I32), axis=-1, keepdims=True))
    thr_key = jnp.left_shift(hi_best, 16) + (lo_best - I16_MIN)

    def count(pred):
        def body(c, acc):
            for j in range(slabs):
                acc = acc + pred(sc_ref[:, slab(c, j)]).astype(I32)
            return acc
        acc = lax.fori_loop(0, n_chunks, body, jnp.zeros((tq, LANES), I32))
        return jnp.sum(acc, axis=-1, keepdims=True)

    thr_found = _key_to_f32(thr_key)
    thr = jnp.where(count(lambda s: s >= thr_found) >= k_sel, thr_found, _key_to_f32(thr_key - 1))
    need = (k_sel - count(lambda s: s > thr)).astype(F32)
    thr_col = thr[:, 0:1]

    def mask_chunk(c, ties_before):
        off = pl.multiple_of(c * kc, kc)
        s = sc_ref[:, pl.ds(off, kc)]
        tie = s == thr_col
        prefix = jnp.dot(jnp.where(tie, 1.0, 0.0).astype(BF16), tri_ref[...], preferred_element_type=F32)
        kpos = off + lax.broadcasted_iota(I32, (1, kc), 1)
        take_tie = tie & (prefix + ties_before <= need)
        sel = ((s > thr_col) | take_tie) & (kpos <= t)
        sc_ref[:, pl.ds(off, kc)] = jnp.where(sel, 0.0, NEG_BIG)
        return ties_before + prefix[:, kc - 1:kc]

    lax.fori_loop(0, n_chunks, mask_chunk, jnp.zeros((tq, 1), F32))

    scale2 = (HEAD_DIM ** -0.5) * math.log2(math.e)
    heads = [slice(h * HEAD_DIM, (h + 1) * HEAD_DIM) for h in range(ATTN_HEADS)]

    def unshifted_chunk(c, carry):
        off = pl.multiple_of(c * kc, kc)
        bias = sc_ref[:, pl.ds(off, kc)]
        out = []
        for sl, (lane_sum, acc) in zip(heads, carry):
            p = jnp.exp2(jnp.dot(q_ref[:, sl], kt_ref[sl, pl.ds(off, kc)],
                                 preferred_element_type=F32) * scale2 + bias)
            for j in range(slabs):
                lane_sum = lane_sum + p[:, j * LANES:(j + 1) * LANES]
            acc = acc + jnp.dot(p.astype(BF16), v_ref[pl.ds(off, kc), sl], preferred_element_type=F32)
            out.append((lane_sum, acc))
        return tuple(out)

    zeros = jnp.zeros((tq, LANES), F32)
    result = lax.fori_loop(0, n_chunks, unshifted_chunk, tuple((zeros, zeros) for _ in heads))
    in_range = None
    for sl, (lane_sum, acc) in zip(heads, result):
        row_sum = jnp.sum(lane_sum, axis=-1, keepdims=True)
        o_ref[:, sl] = (acc / row_sum).astype(BF16)
        ok = (row_sum >= SOFTMAX_SUM_MIN) & (row_sum <= SOFTMAX_SUM_MAX)
        in_range = ok if in_range is None else (in_range & ok)
    n_bad = jnp.sum(jnp.where(in_range, 0.0, 1.0))

    @pl.when(n_bad > 0.0)
    def _():
        _shifted_attention(q_ref, kt_ref, v_ref, sc_ref, o_ref, n_chunks, tq=tq, kc=kc, scale2=scale2)


def _shifted_attention(q_ref, kt_ref, v_ref, sc_ref, o_ref, n_chunks, *, tq, kc, scale2):
    slabs = kc // LANES
    for h in range(ATTN_HEADS):
        sl = slice(h * HEAD_DIM, (h + 1) * HEAD_DIM)

        def logits(c, sl=sl):
            off = pl.multiple_of(c * kc, kc)
            return jnp.dot(q_ref[:, sl], kt_ref[sl, pl.ds(off, kc)],
                           preferred_element_type=F32) * scale2 + sc_ref[:, pl.ds(off, kc)]

        def max_chunk(c, lane_max, logits=logits):
            s = logits(c)
            for j in range(slabs):
                lane_max = jnp.maximum(lane_max, s[:, j * LANES:(j + 1) * LANES])
            return lane_max

        lane_max = lax.fori_loop(0, n_chunks, max_chunk, jnp.full((tq, LANES), NEG_BIG, F32))
        row_max = jnp.max(lane_max, axis=-1, keepdims=True)

        def pv_chunk(c, carry, sl=sl, row_max=row_max, logits=logits):
            lane_sum, acc = carry
            off = pl.multiple_of(c * kc, kc)
            p = jnp.exp2(logits(c) - row_max)
            for j in range(slabs):
                lane_sum = lane_sum + p[:, j * LANES:(j + 1) * LANES]
            acc = acc + jnp.dot(p.astype(BF16), v_ref[pl.ds(off, kc), sl], preferred_element_type=F32)
            return lane_sum, acc

        lane_sum, acc = lax.fori_loop(0, n_chunks, pv_chunk,
                                      (jnp.zeros((tq, LANES), F32), jnp.zeros((tq, HEAD_DIM), F32)))
        o_ref[:, sl] = (acc / jnp.sum(lane_sum, axis=-1, keepdims=True)).astype(BF16)


def _dsa(qi, wib, q, kit, kt, v, *, tq, kc, k_sel):
    s = q.shape[0]
    tri = jnp.triu(jnp.ones((kc, kc), BF16))
    row = lambda w: pl.BlockSpec((tq, w), lambda i: (i, 0))
    return pl.pallas_call(
        functools.partial(_dsa_kernel, tq=tq, kc=kc, k_sel=k_sel),
        grid=(s // tq,),
        in_specs=[row(IDX_HEADS * LANES), row(IDX_HEADS * LANES), row(ATTN_W),
                  _resident((LANES, s)), _resident((ATTN_W, s)), _resident((s, ATTN_W)),
                  _resident((kc, kc))],
        out_specs=row(ATTN_W),
        out_shape=jax.ShapeDtypeStruct((s, ATTN_W), BF16),
        scratch_shapes=[pltpu.VMEM((tq, s), F32), pltpu.VMEM((tq, s), I16), pltpu.VMEM((tq, s), I16)],
        compiler_params=_cparams(("parallel",)),
        name="dsa_attention",
    )(qi, wib, q, kit, kt, v, tri)


def _route(logits):
    lane = lax.broadcasted_iota(I32, (1, LANES), 1).astype(F32)
    far = float(LANES)
    is_group = lane < N_GROUPS
    g_max = jnp.max(jnp.where(is_group, logits, -jnp.inf), axis=-1, keepdims=True)
    g_top = jnp.min(jnp.where(is_group & (logits == g_max), lane, far), axis=-1, keepdims=True)
    pg_top = 1.0 / jnp.sum(jnp.where(is_group, jnp.exp(logits - g_max), 0.0), axis=-1, keepdims=True)
    first = N_GROUPS + EXPERTS_PER_GROUP * g_top
    in_grp = (lane >= first) & (lane < first + EXPERTS_PER_GROUP)
    e_max = jnp.max(jnp.where(in_grp, logits, -jnp.inf), axis=-1, keepdims=True)
    e_exp = jnp.where(in_grp, jnp.exp(logits - e_max), 0.0)
    pe = jnp.where(in_grp, e_exp / jnp.sum(e_exp, axis=-1, keepdims=True), -1.0)
    p1 = jnp.max(pe, axis=-1, keepdims=True)
    e1 = jnp.min(jnp.where(pe == p1, lane, far), axis=-1, keepdims=True)
    pe_rest = jnp.where(lane == e1, -1.0, pe)
    p2 = jnp.max(pe_rest, axis=-1, keepdims=True)
    e2 = jnp.min(jnp.where(pe_rest == p2, lane, far), axis=-1, keepdims=True)
    den = p1 + p2
    w1 = pg_top * p1 / den
    w2 = pg_top * p2 / den
    lo = jnp.minimum(e1, e2) - first
    hi = jnp.maximum(e1, e2) - first
    pair = lo * (2 * EXPERTS_PER_GROUP - 1 - lo) * 0.5 + (hi - lo - 1.0)
    first_is_lo = e1 < e2
    return (g_top * PAIRS_PER_GROUP + pair, jnp.where(first_is_lo, w1, w2), jnp.where(first_is_lo, w2, w1))


def _merge_kernel(x_ref, p_ref, a_ref, c_ref, gt_ref, wpo_ref, wao_ref, wco_ref, wo_ref, gf_ref, wr_ref,
                  tri_ref, x1_ref, route_ref, cnt_ref):
    @pl.when(pl.program_id(0) == 0)
    def _():
        cnt_ref[...] = jnp.zeros_like(cnt_ref)

    d = x_ref.shape[1]
    merged = gt_ref[:, 0:d].astype(F32) * jnp.dot(p_ref[...], wpo_ref[...], preferred_element_type=F32)
    merged += gt_ref[:, d:2 * d].astype(F32) * jnp.dot(a_ref[...], wao_ref[...], preferred_element_type=F32)
    merged += gt_ref[:, 2 * d:3 * d].astype(F32) * jnp.dot(c_ref[...], wco_ref[...], preferred_element_type=F32)
    x1 = x_ref[...] + jnp.dot(merged.astype(BF16), wo_ref[...], preferred_element_type=F32)
    x1_ref[...] = x1
    h2 = _rms_rows(x1, gf_ref[...]).astype(BF16)
    bucket, w_lo, w_hi = _route(jnp.dot(h2, wr_ref[...], preferred_element_type=F32))
    lane = lax.broadcasted_iota(I32, (1, LANES), 1).astype(F32)
    onehot = lane == bucket
    before = jnp.dot(tri_ref[...], jnp.where(onehot, 1.0, 0.0).astype(BF16), preferred_element_type=F32)
    rank = jnp.sum(jnp.where(onehot, before + cnt_ref[...], 0.0), axis=-1, keepdims=True)
    cnt_ref[...] += jnp.sum(jnp.where(onehot, 1.0, 0.0), axis=0, keepdims=True)
    route_ref[...] = jnp.where(lane == ROUTE_BUCKET, bucket,
                               jnp.where(lane == ROUTE_RANK, rank,
                                         jnp.where(lane == ROUTE_W_LO, w_lo,
                                                   jnp.where(lane == ROUTE_W_HI, w_hi, 0.0))))


def _merge(x, p, a, c, gates, wpo, wao, wco, wo, gf, wr, *, tm):
    s, d = x.shape
    row = lambda w: pl.BlockSpec((tm, w), lambda i: (i, 0))
    tri = jnp.tril(jnp.ones((tm, tm), BF16), -1)
    return pl.pallas_call(
        _merge_kernel,
        grid=(s // tm,),
        in_specs=[row(d), row(POOL_W), row(ATTN_W), row(XATTN_W), row(N_BRANCH * d),
                  _resident(wpo.shape), _resident(wao.shape), _resident(wco.shape), _resident(wo.shape),
                  _resident(gf.shape), _resident(wr.shape), _resident(tri.shape)],
        out_specs=[row(d), row(LANES)],
        out_shape=[jax.ShapeDtypeStruct((s, d), F32), jax.ShapeDtypeStruct((s, LANES), F32)],
        scratch_shapes=[pltpu.VMEM((1, LANES), F32)],
        compiler_params=_cparams(("arbitrary",)),
        name="gated_merge_router",
    )(x, p, a, c, gates, wpo, wao, wco, wo, gf, wr, tri)


def _moe_kernel(src_ref, elo_ref, ehi_ref, nvalid_ref, ntiles_ref,
                x1_hbm, w_ref, gf_ref, wg_lo, wu_lo, wd_lo, wg_hi, wu_hi, wd_hi,
                out_hbm, xbuf, obuf, gsem, ssem, *, tile):
    j = pl.program_id(0)
    n_tiles = ntiles_ref[0]
    slot = lax.rem(j, 2)

    def start_gather(t, s):
        def body(r, carry):
            tok = src_ref[t * tile + r]
            pltpu.make_async_copy(x1_hbm.at[pl.ds(tok, 1)], xbuf.at[s, pl.ds(r, 1)], gsem.at[s]).start()
            return carry
        lax.fori_loop(0, tile, body, 0, unroll=8)

    def wait_scatter(t, s):
        n = nvalid_ref[t]
        n_whole = pl.multiple_of(lax.shift_left(lax.shift_right_logical(n, 3), 3), SUBLANES)

        @pl.when(n_whole > 0)
        def _():
            pltpu.make_async_copy(obuf.at[s, pl.ds(0, n_whole)], out_hbm.at[pl.ds(0, n_whole)], ssem.at[s]).wait()

        def wait_row(r, carry):
            pltpu.make_async_copy(obuf.at[s, pl.ds(0, 1)], out_hbm.at[pl.ds(0, 1)], ssem.at[s]).wait()
            return carry
        lax.fori_loop(0, n - n_whole, wait_row, 0)

    @pl.when(j == 0)
    def _():
        start_gather(0, 0)

    @pl.when(j + 1 < n_tiles)
    def _():
        start_gather(j + 1, 1 - slot)

    @pl.when(j < n_tiles)
    def _():
        pltpu.make_async_copy(x1_hbm.at[pl.ds(0, tile)], xbuf.at[slot], gsem.at[slot]).wait()
        x1 = xbuf[slot]
        h = _rms_rows(x1, gf_ref[...]).astype(BF16)
        y = x1
        for w_col, wg, wu, wd in ((ROUTE_W_LO, wg_lo, wu_lo, wd_lo), (ROUTE_W_HI, wg_hi, wu_hi, wd_hi)):
            a = jnp.dot(h, wg[0], preferred_element_type=F32)
            b = jnp.dot(h, wu[0], preferred_element_type=F32)
            act = (a * _sigmoid(a)) * b * w_ref[:, w_col:w_col + 1]
            y = y + jnp.dot(act.astype(BF16), wd[0], preferred_element_type=F32)
        obuf[slot] = y

        def scatter_row(r, carry):
            tok = src_ref[j * tile + r]
            pltpu.make_async_copy(obuf.at[slot, pl.ds(r, 1)], out_hbm.at[pl.ds(tok, 1)], ssem.at[slot]).start()
            return carry
        lax.fori_loop(0, nvalid_ref[j], scatter_row, 0)

        @pl.when(j >= 1)
        def _():
            wait_scatter(j - 1, 1 - slot)

        @pl.when(j == n_tiles - 1)
        def _():
            wait_scatter(j, slot)


def _moe(x1, route, gf, wg, wu, wd, *, tile):
    s, d = x1.shape
    ff = wg.shape[2]
    n_buckets = N_GROUPS * PAIRS_PER_GROUP
    max_tiles = (s + n_buckets * (tile - 1)) // tile
    bucket = route[:, ROUTE_BUCKET].astype(I32)
    rank = route[:, ROUTE_RANK].astype(I32)
    counts = jnp.zeros((n_buckets,), I32).at[bucket].add(1)
    padded = (counts + tile - 1) // tile * tile
    ends = jnp.cumsum(padded)
    starts = ends - padded
    dest = starts[bucket] + rank
    src = jnp.zeros((max_tiles * tile,), I32).at[dest].set(jnp.arange(s, dtype=I32))
    w_sorted = jnp.zeros((max_tiles * tile, ROUTE_COLS), F32).at[dest].set(route[:, :ROUTE_COLS])
    n_tiles = ends[-1] // tile
    tile_start = jnp.arange(max_tiles, dtype=I32) * tile
    first_row = jnp.minimum(tile_start, ends[-1] - tile)
    tile_bucket = jnp.sum((ends[None, :] <= first_row[:, None]).astype(I32), axis=1)
    nvalid = jnp.where(tile_start < ends[-1],
                       jnp.clip(counts[tile_bucket] - (tile_start - starts[tile_bucket]), 0, tile), 0).astype(I32)
    pair_lo = jnp.array([0, 0, 0, 1, 1, 2], I32)
    pair_hi = jnp.array([1, 2, 3, 2, 3, 3], I32)
    group = tile_bucket // PAIRS_PER_GROUP
    e_lo = group * EXPERTS_PER_GROUP + pair_lo[tile_bucket % PAIRS_PER_GROUP]
    e_hi = group * EXPERTS_PER_GROUP + pair_hi[tile_bucket % PAIRS_PER_GROUP]

    lo_w = lambda shape: pl.BlockSpec(shape, lambda j, src, elo, ehi, nv, nt: (elo[j], 0, 0))
    hi_w = lambda shape: pl.BlockSpec(shape, lambda j, src, elo, ehi, nv, nt: (ehi[j], 0, 0))
    grid_spec = pltpu.PrefetchScalarGridSpec(
        num_scalar_prefetch=5,
        grid=(max_tiles,),
        in_specs=[pl.BlockSpec(memory_space=pl.ANY),
                  pl.BlockSpec((tile, ROUTE_COLS), lambda j, *_: (j, 0)),
                  pl.BlockSpec((1, d), lambda j, *_: (0, 0)),
                  lo_w((1, d, ff)), lo_w((1, d, ff)), lo_w((1, ff, d)),
                  hi_w((1, d, ff)), hi_w((1, d, ff)), hi_w((1, ff, d))],
        out_specs=pl.BlockSpec(memory_space=pl.ANY),
        scratch_shapes=[pltpu.VMEM((2, tile, d), F32), pltpu.VMEM((2, tile, d), F32),
                        pltpu.SemaphoreType.DMA((2,)), pltpu.SemaphoreType.DMA((2,))],
    )
    return pl.pallas_call(
        functools.partial(_moe_kernel, tile=tile),
        grid_spec=grid_spec,
        out_shape=jax.ShapeDtypeStruct((s, d), F32),
        compiler_params=_cparams(("arbitrary",)),
        name="moe_experts",
    )(src, e_lo, e_hi, nvalid, n_tiles.reshape(1), x1, w_sorted, gf, wg, wu, wd, wg, wu, wd)


def _front_weight(w_in):
    d = w_in.shape[0]
    sizes = (POOL_W, ATTN_W, ATTN_W, ATTN_W, IDX_HEADS * IDX_DIM, IDX_DIM, IDX_HEADS, XATTN_W)
    offs = [0]
    for sz in sizes:
        offs.append(offs[-1] + sz)
    w_pool, w_q, w_k, w_v, w_qi, w_ki, w_wi, w_xq = (w_in[:, offs[n]:offs[n + 1]] for n in range(len(sizes)))
    pad = lambda w, to: jnp.pad(w, ((0, 0), (0, to - w.shape[1])))
    w_qi = jnp.pad(w_qi.reshape(d, IDX_HEADS, IDX_DIM), ((0, 0), (0, 0), (0, LANES - IDX_DIM)))
    front = jnp.concatenate([w_pool, w_q, w_k, w_v, w_xq, w_qi.reshape(d, IDX_HEADS * LANES),
                             pad(w_ki, LANES), pad(w_wi, LANES)], axis=1)
    return front, w_in[:, offs[-1]:]


def _rope_inv_freq():
    def inv(dim):
        return ROPE_THETA ** (-jnp.arange(0, dim, 2, dtype=F32) / dim)
    return jnp.stack([jnp.tile(inv(HEAD_DIM), LANES // (HEAD_DIM // 2)),
                      jnp.tile(inv(IDX_DIM), LANES // (IDX_DIM // 2))])


def _layer(x, mem, pos, g_mix, w_in, b_gate, w_pool_grp, pool_scale, q_norm_g, k_norm_g, g_mem, w_mem_kv,
           xq_norm_g, xk_norm_g, w_pool_out, w_attn_out, w_cross_out, w_o, g_ffn, w_router_group,
           w_router_expert, w_e_gate, w_e_up, w_e_down):
    s, d = x.shape
    row2 = lambda v: v.reshape(1, -1)
    k_sel = min(TOPK_MAX, s // 4)

    w_front, w_gate = _front_weight(w_in)
    front = _norm_proj(x, row2(g_mix), w_front.astype(BF16), None, tm=512, tn=FRONT_W // 2, out_dtype=F32)
    gates = _norm_proj(x, row2(g_mix), w_gate.astype(BF16), row2(b_gate), tm=1024, tn=1024, out_dtype=BF16)

    q, kt, v, xq, qi, kit, wib = _prep(front, pos.reshape(s, 1), _rope_inv_freq(), row2(q_norm_g),
                                       row2(k_norm_g), row2(xq_norm_g), tm=512)

    bd = jax.scipy.linalg.block_diag(*[w_pool_grp[g] for g in range(len(POOL_WINDOWS))])
    pool_pre = _pool(front, bd.astype(BF16), row2(pool_scale), tm=512)

    k_m, v_m = _memkv(mem, row2(g_mem), w_mem_kv.astype(BF16), row2(xk_norm_g))
    cross = _cross(xq, k_m, v_m, tm=512)

    attn = _dsa(qi, wib, q, kit, kt, v, tq=256, kc=512, k_sel=k_sel)

    w_router = jnp.pad(jnp.concatenate([w_router_group, w_router_expert], axis=1),
                       ((0, 0), (0, LANES - N_GROUPS - N_EXPERTS)))
    x1, route = _merge(x, pool_pre, attn, cross, gates, w_pool_out.astype(BF16), w_attn_out.astype(BF16),
                       w_cross_out.astype(BF16), w_o.astype(BF16), row2(g_ffn), w_router.astype(BF16), tm=256)
    return _moe(x1, route, row2(g_ffn), w_e_gate.astype(BF16), w_e_up.astype(BF16), w_e_down.astype(BF16),
                tile=256)


def kernel(x, mem, positions, g_mix, w_in, b_gate, w_pool_grp, pool_scale, q_norm_g, k_norm_g, g_mem, w_mem_kv,
           xq_norm_g, xk_norm_g, w_pool_out, w_attn_out, w_cross_out, w_o, g_ffn, w_router_group,
           w_router_expert, w_e_gate, w_e_up, w_e_down):
    depth = g_mix.shape[0]
    outs = []
    for b in range(x.shape[0]):
        xb = x[b]
        for l in range(depth):
            xb = _layer(xb, mem[b], positions[b], g_mix[l], w_in[l], b_gate[l], w_pool_grp[l], pool_scale[l],
                        q_norm_g[l], k_norm_g[l], g_mem[l], w_mem_kv[l], xq_norm_g[l], xk_norm_g[l],
                        w_pool_out[l], w_attn_out[l], w_cross_out[l], w_o[l], g_ffn[l], w_router_group[l],
                        w_router_expert[l], w_e_gate[l], w_e_up[l], w_e_down[l])
        outs.append(xb)
    return jnp.stack(outs)
```

```python
import functools
import math

import jax
import jax.numpy as jnp
from jax import lax
from jax.experimental import pallas as pl
from jax.experimental.pallas import tpu as pltpu

F32 = jnp.float32
BF16 = jnp.bfloat16
I32 = jnp.int32
I16 = jnp.int16
I16_MIN = -32768

NORM_EPS = 1e-6
ROPE_THETA = 10000.0
HEAD_DIM = 128
ATTN_HEADS = 6
ATTN_W = ATTN_HEADS * HEAD_DIM
IDX_HEADS = 4
IDX_DIM = 64
XATTN_HEADS = 4
XATTN_W = XATTN_HEADS * HEAD_DIM
POOL_WINDOWS = (2, 4, 8, 16)
POOL_GROUP_W = 192
POOL_W = len(POOL_WINDOWS) * POOL_GROUP_W
POOL_HALO = 16
TOPK_MAX = 256
N_GROUPS = 4
EXPERTS_PER_GROUP = 4
N_EXPERTS = N_GROUPS * EXPERTS_PER_GROUP
N_BRANCH = 3
PAIRS_PER_GROUP = EXPERTS_PER_GROUP * (EXPERTS_PER_GROUP - 1) // 2
ROUTE_BUCKET, ROUTE_RANK, ROUTE_W_LO, ROUTE_W_HI, ROUTE_COLS = 0, 1, 2, 3, 4

LANES = 128
SUBLANES = 8
V7X_VMEM_LIMIT_BYTES = 56 * 1024 * 1024

COL_POOL = 0
COL_Q = COL_POOL + POOL_W
COL_K = COL_Q + ATTN_W
COL_V = COL_K + ATTN_W
COL_XQ = COL_V + ATTN_W
COL_QI = COL_XQ + XATTN_W
COL_KI = COL_QI + IDX_HEADS * LANES
COL_WI = COL_KI + LANES
FRONT_W = COL_WI + LANES
IDX_PITCH = IDX_HEADS * LANES + LANES

NEG_BIG = -1e30
INT_MIN = -2147483648
KEY_NEG_INF = -2139095041
SOFTMAX_SUM_MIN = 2.0 ** -64
SOFTMAX_SUM_MAX = 2.0 ** 100
NT_DIMS = (((1,), (1,)), ((), ()))


def _cparams(semantics):
    return pltpu.CompilerParams(dimension_semantics=semantics, vmem_limit_bytes=V7X_VMEM_LIMIT_BYTES)


def _resident(shape):
    return pl.BlockSpec(shape, lambda *_: (0,) * len(shape), pipeline_mode=pl.Buffered(1))


def _rms_rows(x, g):
    ms = jnp.mean(x * x, axis=-1, keepdims=True)
    return x * lax.rsqrt(ms + NORM_EPS) * g


def _sigmoid(x):
    return 1.0 / (1.0 + jnp.exp(-x))


def _proj_kernel(x_ref, g_ref, w_ref, o_ref, h_ref):
    @pl.when(pl.program_id(1) == 0)
    def _():
        h_ref[...] = _rms_rows(x_ref[...], g_ref[...]).astype(BF16)

    o_ref[...] = jnp.dot(h_ref[...], w_ref[...], preferred_element_type=F32).astype(o_ref.dtype)


def _gate_proj_kernel(x_ref, g_ref, w_ref, b_ref, o_ref, h_ref):
    @pl.when(pl.program_id(1) == 0)
    def _():
        h_ref[...] = _rms_rows(x_ref[...], g_ref[...]).astype(BF16)

    acc = jnp.dot(h_ref[...], w_ref[...], preferred_element_type=F32)
    o_ref[...] = _sigmoid(acc + b_ref[...]).astype(o_ref.dtype)


def _norm_proj(x, g, w, bias, *, tm, tn, out_dtype):
    s, d = x.shape
    n = w.shape[1]
    in_specs = [
        pl.BlockSpec((tm, d), lambda i, j: (i, 0)),
        pl.BlockSpec((1, d), lambda i, j: (0, 0)),
        pl.BlockSpec((d, tn), lambda i, j: (0, j)),
    ]
    args = [x, g, w]
    body = _proj_kernel
    if bias is not None:
        in_specs.append(pl.BlockSpec((1, tn), lambda i, j: (0, j)))
        args.append(bias)
        body = _gate_proj_kernel
    return pl.pallas_call(
        body,
        grid=(s // tm, n // tn),
        in_specs=in_specs,
        out_specs=pl.BlockSpec((tm, tn), lambda i, j: (i, j)),
        out_shape=jax.ShapeDtypeStruct((s, n), out_dtype),
        scratch_shapes=[pltpu.VMEM((tm, d), BF16)],
        compiler_params=_cparams(("parallel", "arbitrary")),
        name="norm_proj_gate" if bias is not None else "norm_proj_front",
    )(*args)


def _prep_kernel(f_ref, pos_ref, invf_ref, qg_ref, kg_ref, xg_ref,
                 q_ref, k_ref, v_ref, xq_ref, qi_ref, ki_ref, wib_ref):
    tm = f_ref.shape[0]
    pos = pos_ref[...].astype(F32)
    lane = lax.broadcasted_iota(I32, (1, LANES), 1)
    ang = pos * invf_ref[0:1, :]
    cos_a = jnp.cos(ang)
    sin_a = jnp.where(lane < HEAD_DIM // 2, -jnp.sin(ang), jnp.sin(ang))
    ang_i = pos * invf_ref[1:2, :]
    low_i = (lane & (IDX_DIM - 1)) < IDX_DIM // 2
    cos_i = jnp.cos(ang_i)
    sin_i = jnp.where(low_i, -jnp.sin(ang_i), jnp.sin(ang_i))

    def rope_head(y):
        return y * cos_a + pltpu.roll(y, HEAD_DIM // 2, 1) * sin_a

    def rope_idx(y):
        partner = jnp.where(low_i, pltpu.roll(y, LANES - IDX_DIM // 2, 1), pltpu.roll(y, IDX_DIM // 2, 1))
        return y * cos_i + partner * sin_i

    for h in range(ATTN_HEADS):
        sl = slice(h * HEAD_DIM, (h + 1) * HEAD_DIM)
        qh = f_ref[:, COL_Q + h * HEAD_DIM:COL_Q + (h + 1) * HEAD_DIM]
        kh = f_ref[:, COL_K + h * HEAD_DIM:COL_K + (h + 1) * HEAD_DIM]
        q_ref[:, sl] = rope_head(_rms_rows(qh, qg_ref[...])).astype(BF16)
        k_ref[:, sl] = rope_head(_rms_rows(kh, kg_ref[...])).astype(BF16)
    v_ref[...] = f_ref[:, COL_V:COL_V + ATTN_W].astype(BF16)
    for h in range(XATTN_HEADS):
        sl = slice(h * HEAD_DIM, (h + 1) * HEAD_DIM)
        xh = f_ref[:, COL_XQ + h * HEAD_DIM:COL_XQ + (h + 1) * HEAD_DIM]
        xq_ref[:, sl] = _rms_rows(xh, xg_ref[...]).astype(BF16)
    for h in range(IDX_HEADS):
        sl = slice(h * LANES, (h + 1) * LANES)
        qi_ref[:, sl] = rope_idx(f_ref[:, COL_QI + h * LANES:COL_QI + (h + 1) * LANES]).astype(BF16)
    ki_ref[...] = rope_idx(f_ref[:, COL_KI:COL_KI + LANES]).astype(BF16)
    wi = f_ref[:, COL_WI:COL_WI + LANES] * (IDX_HEADS ** -0.5) * (IDX_DIM ** -0.5)
    for h in range(IDX_HEADS):
        wib_ref[:, h * LANES:(h + 1) * LANES] = jnp.broadcast_to(wi[:, h:h + 1], (tm, LANES))
    qi_ref[:, IDX_HEADS * LANES:] = jnp.zeros((tm, LANES), BF16)
    wib_ref[:, IDX_HEADS * LANES:] = jnp.zeros((tm, LANES), F32)


def _prep(front, pos_col, invf, qg, kg, xg, *, tm):
    s = front.shape[0]
    row = lambda w: pl.BlockSpec((tm, w), lambda i: (i, 0))
    small = lambda shape: pl.BlockSpec(shape, lambda i: (0, 0))
    outs = [(row(ATTN_W), (s, ATTN_W), BF16), (row(ATTN_W), (s, ATTN_W), BF16), (row(ATTN_W), (s, ATTN_W), BF16),
            (row(XATTN_W), (s, XATTN_W), BF16), (row(IDX_PITCH), (s, IDX_PITCH), BF16),
            (row(LANES), (s, LANES), BF16), (row(IDX_PITCH), (s, IDX_PITCH), F32)]
    return pl.pallas_call(
        _prep_kernel,
        grid=(s // tm,),
        in_specs=[row(FRONT_W), row(1), small((2, LANES)), small((1, LANES)), small((1, LANES)),
                  small((1, LANES))],
        out_specs=[spec for spec, _, _ in outs],
        out_shape=[jax.ShapeDtypeStruct(shape, dt) for _, shape, dt in outs],
        compiler_params=_cparams(("parallel",)),
        name="qknorm_rope",
    )(front, pos_col, invf, qg, kg, xg)


def _pool_kernel(u_ref, halo_ref, bd_ref, sc_ref, o_ref, buf_ref):
    i = pl.program_id(0)
    tm = u_ref.shape[0]
    u = u_ref[...]
    buf_ref[0:POOL_HALO, :] = jnp.where(i == 0, 0.0, halo_ref[...])
    buf_ref[POOL_HALO:POOL_HALO + tm, :] = u
    t = i * tm + lax.broadcasted_iota(I32, (tm, 1), 0)
    lane = lax.broadcasted_iota(I32, (1, POOL_W), 1)
    acc = u
    pooled = None
    for d in range(1, max(POOL_WINDOWS)):
        acc = acc + buf_ref[POOL_HALO - d:POOL_HALO - d + tm, :]
        if d + 1 in POOL_WINDOWS:
            g = POOL_WINDOWS.index(d + 1)
            mean = acc / jnp.minimum(t + 1, d + 1).astype(F32)
            in_group = (lane >= g * POOL_GROUP_W) & (lane < (g + 1) * POOL_GROUP_W)
            pooled = jnp.where(in_group, mean, 0.0 if pooled is None else pooled)
    p = (pooled - u).astype(BF16)
    y = jnp.dot(p, bd_ref[...], preferred_element_type=F32) * sc_ref[...]
    o_ref[...] = y.astype(BF16)


def _pool(front, bd, scale, *, tm):
    s = front.shape[0]
    halo_blocks = tm // POOL_HALO
    return pl.pallas_call(
        _pool_kernel,
        grid=(s // tm,),
        in_specs=[
            pl.BlockSpec((tm, POOL_W), lambda i: (i, 0)),
            pl.BlockSpec((POOL_HALO, POOL_W), lambda i: (jnp.maximum(i * halo_blocks - 1, 0), 0)),
            pl.BlockSpec((POOL_W, POOL_W), lambda i: (0, 0)),
            pl.BlockSpec((1, POOL_W), lambda i: (0, 0)),
        ],
        out_specs=pl.BlockSpec((tm, POOL_W), lambda i: (i, 0)),
        out_shape=jax.ShapeDtypeStruct((s, POOL_W), BF16),
        scratch_shapes=[pltpu.VMEM((tm + POOL_HALO, POOL_W), F32)],
        compiler_params=_cparams(("parallel",)),
        name="pool_mixer",
    )(front, front, bd, scale)


def _memkv_kernel(mem_ref, g_ref, w_ref, kg_ref, k_ref, v_ref):
    h = _rms_rows(mem_ref[...], g_ref[...]).astype(BF16)
    kv = jnp.dot(h, w_ref[...], preferred_element_type=F32)
    for hd in range(XATTN_HEADS):
        sl = slice(hd * HEAD_DIM, (hd + 1) * HEAD_DIM)
        k_ref[:, sl] = _rms_rows(kv[:, sl], kg_ref[...]).astype(BF16)
    v_ref[...] = kv[:, XATTN_W:].astype(BF16)


def _memkv(mem, g, w, kg):
    m = mem.shape[0]
    return pl.pallas_call(
        _memkv_kernel,
        out_shape=[jax.ShapeDtypeStruct((m, XATTN_W), BF16), jax.ShapeDtypeStruct((m, XATTN_W), BF16)],
        compiler_params=pltpu.CompilerParams(vmem_limit_bytes=V7X_VMEM_LIMIT_BYTES),
        name="mem_kv",
    )(mem, g, w, kg)


def _cross_kernel(xq_ref, k_ref, v_ref, o_ref):
    for h in range(XATTN_HEADS):
        sl = slice(h * HEAD_DIM, (h + 1) * HEAD_DIM)
        logits = lax.dot_general(xq_ref[:, sl], k_ref[:, sl], NT_DIMS,
                                 preferred_element_type=F32) * (HEAD_DIM ** -0.5)
        e = jnp.exp(logits - jnp.max(logits, axis=-1, keepdims=True))
        p = e / jnp.sum(e, axis=-1, keepdims=True)
        o_ref[:, sl] = jnp.dot(p.astype(BF16), v_ref[:, sl], preferred_element_type=F32).astype(BF16)


def _cross(xq, k_m, v_m, *, tm):
    s = xq.shape[0]
    m = k_m.shape[0]
    return pl.pallas_call(
        _cross_kernel,
        grid=(s // tm,),
        in_specs=[pl.BlockSpec((tm, XATTN_W), lambda i: (i, 0)),
                  pl.BlockSpec((m, XATTN_W), lambda i: (0, 0)),
                  pl.BlockSpec((m, XATTN_W), lambda i: (0, 0))],
        out_specs=pl.BlockSpec((tm, XATTN_W), lambda i: (i, 0)),
        out_shape=jax.ShapeDtypeStruct((s, XATTN_W), BF16),
        compiler_params=_cparams(("parallel",)),
        name="mem_cross_attn",
    )(xq, k_m, v_m)


def _key_to_f32(key):
    bits = key ^ ((key >> 31) & 0x7FFFFFFF)
    return lax.bitcast_convert_type(bits, F32)


def _dsa_kernel(qi_ref, wib_ref, q_ref, ki_ref, k_ref, v_ref, tri_ref, o_ref, sc_ref, hi_ref, lo_ref,
                *, tq, kc, k_sel):
    i = pl.program_id(0)
    n_chunks = ((i + 1) * tq + kc - 1) // kc
    t = i * tq + lax.broadcasted_iota(I32, (tq, 1), 0)
    slabs = kc // LANES

    def slab(c, j):
        return pl.ds(pl.multiple_of(c * kc, kc) + j * LANES, LANES)

    def score_chunk(c, carry):
        off = pl.multiple_of(c * kc, kc)
        ki_c = ki_ref[pl.ds(off, kc), :]
        dots = [lax.dot_general(qi_ref[:, h * LANES:(h + 1) * LANES], ki_c, NT_DIMS,
                                preferred_element_type=F32) for h in range(IDX_HEADS)]
        for j in range(slabs):
            s = jnp.zeros((tq, LANES), F32)
            for h in range(IDX_HEADS):
                s = s + jnp.maximum(dots[h][:, j * LANES:(j + 1) * LANES], 0.0) * wib_ref[:, h * LANES:(h + 1) * LANES]
            kpos = off + j * LANES + lax.broadcasted_iota(I32, (1, LANES), 1)
            s = jnp.where(kpos <= t, jnp.where(s == 0.0, 0.0, s), -jnp.inf)
            sc_ref[:, slab(c, j)] = s
            bits = pltpu.bitcast(s, I32)
            key = bits ^ ((bits >> 31) & 0x7FFFFFFF)
            hi_ref[:, slab(c, j)] = (key >> 16).astype(I16)
            lo_ref[:, slab(c, j)] = ((key & 0xFFFF) + I16_MIN).astype(I16)
        return carry

    lax.fori_loop(0, n_chunks, score_chunk, 0)

    def count16(ref, cand16):
        def body(c, acc):
            for j in range(slabs):
                acc = acc + jnp.where(ref[:, slab(c, j)] >= cand16, jnp.int16(1), jnp.int16(0))
            return acc
        acc = lax.fori_loop(0, n_chunks, body, jnp.zeros((tq, LANES), I16))
        return jnp.sum(acc.astype(I32), axis=-1, keepdims=True)

    def search16(ref, counted_already):
        def step(b, best):
            cand = best + jnp.left_shift(jnp.int32(1), 15 - b)
            enough = count16(ref, cand.astype(I16)) + counted_already >= k_sel
            return jnp.where(enough, cand, best)
        return lax.fori_loop(0, 16, step, jnp.full((tq, LANES), I16_MIN, I32))

    hi_best = search16(hi_ref, 0)
    hi_best16 = hi_best.astype(I16)

    def split_chunk(c, acc):
        for j in range(slabs):
            h = hi_ref[:, slab(c, j)]
            lo_ref[:, slab(c, j)] = jnp.where(h == hi_best16, lo_ref[:, slab(c, j)], jnp.int16(I16_MIN))
            acc = acc + jnp.where(h > hi_best16, jnp.int16(1), jnp.int16(0))
        return acc

    above = lax.fori_loop(0, n_chunks, split_chunk, jnp.zeros((tq, LANES), I16))
    lo_best = search16(lo_ref, jnp.sum(above.astype(I32), axis=-1, keepdims=True))
    thr_key = jnp.left_shift(hi_best, 16) + (lo_best - I16_MIN)

    def count(pred):
        def body(c, acc):
            for j in range(slabs):
                acc = acc + pred(sc_ref[:, slab(c, j)]).astype(I32)
            return acc
        acc = lax.fori_loop(0, n_chunks, body, jnp.zeros((tq, LANES), I32))
        return jnp.sum(acc, axis=-1, keepdims=True)

    thr_found = _key_to_f32(thr_key)
    thr = jnp.where(count(lambda s: s >= thr_found) >= k_sel, thr_found, _key_to_f32(thr_key - 1))
    need = (k_sel - count(lambda s: s > thr)).astype(F32)
    thr_col = thr[:, 0:1]

    def mask_chunk(c, ties_before):
        off = pl.multiple_of(c * kc, kc)
        s = sc_ref[:, pl.ds(off, kc)]
        tie = s == thr_col
        prefix = jnp.dot(jnp.where(tie, 1.0, 0.0).astype(BF16), tri_ref[:, 0:kc], preferred_element_type=F32)
        kpos = off + lax.broadcasted_iota(I32, (1, kc), 1)
        take_tie = tie & (prefix + ties_before <= need)
        sel = ((s > thr_col) | take_tie) & (kpos <= t)
        sc_ref[:, pl.ds(off, kc)] = jnp.where(sel, 0.0, NEG_BIG)
        return ties_before + prefix[:, kc - 1:kc]

    lax.fori_loop(0, n_chunks, mask_chunk, jnp.zeros((tq, 1), F32))

    scale2 = (HEAD_DIM ** -0.5) * math.log2(math.e)
    heads = [slice(h * HEAD_DIM, (h + 1) * HEAD_DIM) for h in range(ATTN_HEADS)]

    def unshifted_chunk(c, carry):
        off = pl.multiple_of(c * kc, kc)
        bias = sc_ref[:, pl.ds(off, kc)]
        out = []
        for sl, (lane_sum, acc) in zip(heads, carry):
            p = jnp.exp2(lax.dot_general(q_ref[:, sl], k_ref[pl.ds(off, kc), sl], NT_DIMS,
                                         preferred_element_type=F32) * scale2 + bias)
            for j in range(slabs):
                lane_sum = lane_sum + p[:, j * LANES:(j + 1) * LANES]
            acc = acc + jnp.dot(p.astype(BF16), v_ref[pl.ds(off, kc), sl], preferred_element_type=F32)
            out.append((lane_sum, acc))
        return tuple(out)

    zeros = jnp.zeros((tq, LANES), F32)
    result = lax.fori_loop(0, n_chunks, unshifted_chunk, tuple((zeros, zeros) for _ in heads))
    in_range = None
    for sl, (lane_sum, acc) in zip(heads, result):
        row_sum = jnp.sum(lane_sum, axis=-1, keepdims=True)
        o_ref[:, sl] = (acc / row_sum).astype(BF16)
        ok = (row_sum >= SOFTMAX_SUM_MIN) & (row_sum <= SOFTMAX_SUM_MAX)
        in_range = ok if in_range is None else (in_range & ok)
    n_bad = jnp.sum(jnp.where(in_range, 0.0, 1.0))

    @pl.when(n_bad > 0.0)
    def _():
        _shifted_attention(q_ref, k_ref, v_ref, sc_ref, o_ref, n_chunks, tq=tq, kc=kc, scale2=scale2)


def _shifted_attention(q_ref, k_ref, v_ref, sc_ref, o_ref, n_chunks, *, tq, kc, scale2):
    slabs = kc // LANES
    for h in range(ATTN_HEADS):
        sl = slice(h * HEAD_DIM, (h + 1) * HEAD_DIM)

        def logits(c, sl=sl):
            off = pl.multiple_of(c * kc, kc)
            return lax.dot_general(q_ref[:, sl], k_ref[pl.ds(off, kc), sl], NT_DIMS,
                                   preferred_element_type=F32) * scale2 + sc_ref[:, pl.ds(off, kc)]

        def max_chunk(c, lane_max, logits=logits):
            s = logits(c)
            for j in range(slabs):
                lane_max = jnp.maximum(lane_max, s[:, j * LANES:(j + 1) * LANES])
            return lane_max

        lane_max = lax.fori_loop(0, n_chunks, max_chunk, jnp.full((tq, LANES), NEG_BIG, F32))
        row_max = jnp.max(lane_max, axis=-1, keepdims=True)

        def pv_chunk(c, carry, sl=sl, row_max=row_max, logits=logits):
            lane_sum, acc = carry
            off = pl.multiple_of(c * kc, kc)
            p = jnp.exp2(logits(c) - row_max)
            for j in range(slabs):
                lane_sum = lane_sum + p[:, j * LANES:(j + 1) * LANES]
            acc = acc + jnp.dot(p.astype(BF16), v_ref[pl.ds(off, kc), sl], preferred_element_type=F32)
            return lane_sum, acc

        lane_sum, acc = lax.fori_loop(0, n_chunks, pv_chunk,
                                      (jnp.zeros((tq, LANES), F32), jnp.zeros((tq, HEAD_DIM), F32)))
        o_ref[:, sl] = (acc / jnp.sum(lane_sum, axis=-1, keepdims=True)).astype(BF16)


def _dsa(qi, wib, q, ki, k, v, *, tq, kc, k_sel):
    s = q.shape[0]
    tri = jnp.pad(jnp.triu(jnp.ones((kc, kc), BF16)), ((0, 0), (0, LANES)))
    row = lambda w: pl.BlockSpec((tq, w), lambda i: (i, 0))
    return pl.pallas_call(
        functools.partial(_dsa_kernel, tq=tq, kc=kc, k_sel=k_sel),
        grid=(s // tq,),
        in_specs=[row(IDX_PITCH), row(IDX_PITCH), row(ATTN_W),
                  _resident((s, LANES)), _resident((s, ATTN_W)), _resident((s, ATTN_W)),
                  _resident(tri.shape)],
        out_specs=row(ATTN_W),
        out_shape=jax.ShapeDtypeStruct((s, ATTN_W), BF16),
        scratch_shapes=[pltpu.VMEM((tq, s + LANES), F32), pltpu.VMEM((tq, s + LANES), I16),
                        pltpu.VMEM((tq, s + LANES), I16)],
        compiler_params=_cparams(("parallel",)),
        name="dsa_attention",
    )(qi, wib, q, ki, k, v, tri)


def _route(logits):
    lane = lax.broadcasted_iota(I32, (1, LANES), 1).astype(F32)
    far = float(LANES)
    is_group = lane < N_GROUPS
    g_max = jnp.max(jnp.where(is_group, logits, -jnp.inf), axis=-1, keepdims=True)
    g_top = jnp.min(jnp.where(is_group & (logits == g_max), lane, far), axis=-1, keepdims=True)
    pg_top = 1.0 / jnp.sum(jnp.where(is_group, jnp.exp(logits - g_max), 0.0), axis=-1, keepdims=True)
    first = N_GROUPS + EXPERTS_PER_GROUP * g_top
    in_grp = (lane >= first) & (lane < first + EXPERTS_PER_GROUP)
    e_max = jnp.max(jnp.where(in_grp, logits, -jnp.inf), axis=-1, keepdims=True)
    e_exp = jnp.where(in_grp, jnp.exp(logits - e_max), 0.0)
    pe = jnp.where(in_grp, e_exp / jnp.sum(e_exp, axis=-1, keepdims=True), -1.0)
    p1 = jnp.max(pe, axis=-1, keepdims=True)
    e1 = jnp.min(jnp.where(pe == p1, lane, far), axis=-1, keepdims=True)
    pe_rest = jnp.where(lane == e1, -1.0, pe)
    p2 = jnp.max(pe_rest, axis=-1, keepdims=True)
    e2 = jnp.min(jnp.where(pe_rest == p2, lane, far), axis=-1, keepdims=True)
    den = p1 + p2
    w1 = pg_top * p1 / den
    w2 = pg_top * p2 / den
    lo = jnp.minimum(e1, e2) - first
    hi = jnp.maximum(e1, e2) - first
    pair = lo * (2 * EXPERTS_PER_GROUP - 1 - lo) * 0.5 + (hi - lo - 1.0)
    first_is_lo = e1 < e2
    return (g_top * PAIRS_PER_GROUP + pair, jnp.where(first_is_lo, w1, w2), jnp.where(first_is_lo, w2, w1))


def _merge_kernel(x_ref, p_ref, a_ref, c_ref, gt_ref, wpo_ref, wao_ref, wco_ref, wo_ref, gf_ref, wr_ref,
                  tri_ref, x1_ref, route_ref, cnt_ref):
    @pl.when(pl.program_id(0) == 0)
    def _():
        cnt_ref[...] = jnp.zeros_like(cnt_ref)

    d = x_ref.shape[1]
    merged = gt_ref[:, 0:d].astype(F32) * jnp.dot(p_ref[...], wpo_ref[...], preferred_element_type=F32)
    merged += gt_ref[:, d:2 * d].astype(F32) * jnp.dot(a_ref[...], wao_ref[...], preferred_element_type=F32)
    merged += gt_ref[:, 2 * d:3 * d].astype(F32) * jnp.dot(c_ref[...], wco_ref[...], preferred_element_type=F32)
    x1 = x_ref[...] + jnp.dot(merged.astype(BF16), wo_ref[...], preferred_element_type=F32)
    x1_ref[...] = x1
    h2 = _rms_rows(x1, gf_ref[...]).astype(BF16)
    bucket, w_lo, w_hi = _route(jnp.dot(h2, wr_ref[...], preferred_element_type=F32))
    lane = lax.broadcasted_iota(I32, (1, LANES), 1).astype(F32)
    onehot = lane == bucket
    before = jnp.dot(tri_ref[...], jnp.where(onehot, 1.0, 0.0).astype(BF16), preferred_element_type=F32)
    rank = jnp.sum(jnp.where(onehot, before + cnt_ref[...], 0.0), axis=-1, keepdims=True)
    cnt_ref[...] += jnp.sum(jnp.where(onehot, 1.0, 0.0), axis=0, keepdims=True)
    route_ref[...] = jnp.where(lane == ROUTE_BUCKET, bucket,
                               jnp.where(lane == ROUTE_RANK, rank,
                                         jnp.where(lane == ROUTE_W_LO, w_lo,
                                                   jnp.where(lane == ROUTE_W_HI, w_hi, 0.0))))


def _merge(x, p, a, c, gates, wpo, wao, wco, wo, gf, wr, *, tm):
    s, d = x.shape
    row = lambda w: pl.BlockSpec((tm, w), lambda i: (i, 0))
    tri = jnp.tril(jnp.ones((tm, tm), BF16), -1)
    return pl.pallas_call(
        _merge_kernel,
        grid=(s // tm,),
        in_specs=[row(d), row(POOL_W), row(ATTN_W), row(XATTN_W), row(N_BRANCH * d),
                  _resident(wpo.shape), _resident(wao.shape), _resident(wco.shape), _resident(wo.shape),
                  _resident(gf.shape), _resident(wr.shape), _resident(tri.shape)],
        out_specs=[row(d), row(LANES)],
        out_shape=[jax.ShapeDtypeStruct((s, d), F32), jax.ShapeDtypeStruct((s, LANES), F32)],
        scratch_shapes=[pltpu.VMEM((1, LANES), F32)],
        compiler_params=_cparams(("arbitrary",)),
        name="gated_merge_router",
    )(x, p, a, c, gates, wpo, wao, wco, wo, gf, wr, tri)


def _moe_kernel(src_ref, elo_ref, ehi_ref, nvalid_ref, ntiles_ref,
                x1_hbm, w_ref, gf_ref, wg_lo, wu_lo, wd_lo, wg_hi, wu_hi, wd_hi,
                out_hbm, xbuf, obuf, gsem, ssem, *, tile):
    j = pl.program_id(0)
    n_tiles = ntiles_ref[0]
    slot = lax.rem(j, 2)

    def start_gather(t, s):
        def body(r, carry):
            tok = src_ref[t * tile + r]
            pltpu.make_async_copy(x1_hbm.at[pl.ds(tok, 1)], xbuf.at[s, pl.ds(r, 1)], gsem.at[s]).start()
            return carry
        lax.fori_loop(0, tile, body, 0, unroll=8)

    def wait_scatter(t, s):
        n = nvalid_ref[t]
        n_whole = pl.multiple_of(lax.shift_left(lax.shift_right_logical(n, 3), 3), SUBLANES)

        @pl.when(n_whole > 0)
        def _():
            pltpu.make_async_copy(obuf.at[s, pl.ds(0, n_whole)], out_hbm.at[pl.ds(0, n_whole)], ssem.at[s]).wait()

        def wait_row(r, carry):
            pltpu.make_async_copy(obuf.at[s, pl.ds(0, 1)], out_hbm.at[pl.ds(0, 1)], ssem.at[s]).wait()
            return carry
        lax.fori_loop(0, n - n_whole, wait_row, 0)

    @pl.when(j == 0)
    def _():
        start_gather(0, 0)

    @pl.when(j + 1 < n_tiles)
    def _():
        start_gather(j + 1, 1 - slot)

    @pl.when(j < n_tiles)
    def _():
        pltpu.make_async_copy(x1_hbm.at[pl.ds(0, tile)], xbuf.at[slot], gsem.at[slot]).wait()
        x1 = xbuf[slot]
        h = _rms_rows(x1, gf_ref[...]).astype(BF16)
        y = x1
        for w_col, wg, wu, wd in ((ROUTE_W_LO, wg_lo, wu_lo, wd_lo), (ROUTE_W_HI, wg_hi, wu_hi, wd_hi)):
            a = jnp.dot(h, wg[0], preferred_element_type=F32)
            b = jnp.dot(h, wu[0], preferred_element_type=F32)
            act = (a * _sigmoid(a)) * b * w_ref[:, w_col:w_col + 1]
            y = y + jnp.dot(act.astype(BF16), wd[0], preferred_element_type=F32)
        obuf[slot] = y

        def scatter_row(r, carry):
            tok = src_ref[j * tile + r]
            pltpu.make_async_copy(obuf.at[slot, pl.ds(r, 1)], out_hbm.at[pl.ds(tok, 1)], ssem.at[slot]).start()
            return carry
        lax.fori_loop(0, nvalid_ref[j], scatter_row, 0)

        @pl.when(j >= 1)
        def _():
            wait_scatter(j - 1, 1 - slot)

        @pl.when(j == n_tiles - 1)
        def _():
            wait_scatter(j, slot)


def _moe(x1, route, gf, wg, wu, wd, *, tile):
    s, d = x1.shape
    ff = wg.shape[2]
    n_buckets = N_GROUPS * PAIRS_PER_GROUP
    max_tiles = (s + n_buckets * (tile - 1)) // tile
    bucket = route[:, ROUTE_BUCKET].astype(I32)
    rank = route[:, ROUTE_RANK].astype(I32)
    counts = jnp.zeros((n_buckets,), I32).at[bucket].add(1)
    padded = (counts + tile - 1) // tile * tile
    ends = jnp.cumsum(padded)
    starts = ends - padded
    dest = starts[bucket] + rank
    src = jnp.zeros((max_tiles * tile,), I32).at[dest].set(jnp.arange(s, dtype=I32))
    w_sorted = jnp.zeros((max_tiles * tile, ROUTE_COLS), F32).at[dest].set(route[:, :ROUTE_COLS])
    n_tiles = ends[-1] // tile
    tile_start = jnp.arange(max_tiles, dtype=I32) * tile
    first_row = jnp.minimum(tile_start, ends[-1] - tile)
    tile_bucket = jnp.sum((ends[None, :] <= first_row[:, None]).astype(I32), axis=1)
    nvalid = jnp.where(tile_start < ends[-1],
                       jnp.clip(counts[tile_bucket] - (tile_start - starts[tile_bucket]), 0, tile), 0).astype(I32)
    pair_lo = jnp.array([0, 0, 0, 1, 1, 2], I32)
    pair_hi = jnp.array([1, 2, 3, 2, 3, 3], I32)
    group = tile_bucket // PAIRS_PER_GROUP
    e_lo = group * EXPERTS_PER_GROUP + pair_lo[tile_bucket % PAIRS_PER_GROUP]
    e_hi = group * EXPERTS_PER_GROUP + pair_hi[tile_bucket % PAIRS_PER_GROUP]

    lo_w = lambda shape: pl.BlockSpec(shape, lambda j, src, elo, ehi, nv, nt: (elo[j], 0, 0))
    hi_w = lambda shape: pl.BlockSpec(shape, lambda j, src, elo, ehi, nv, nt: (ehi[j], 0, 0))
    grid_spec = pltpu.PrefetchScalarGridSpec(
        num_scalar_prefetch=5,
        grid=(max_tiles,),
        in_specs=[pl.BlockSpec(memory_space=pl.ANY),
                  pl.BlockSpec((tile, ROUTE_COLS), lambda j, *_: (j, 0)),
                  pl.BlockSpec((1, d), lambda j, *_: (0, 0)),
                  lo_w((1, d, ff)), lo_w((1, d, ff)), lo_w((1, ff, d)),
                  hi_w((1, d, ff)), hi_w((1, d, ff)), hi_w((1, ff, d))],
        out_specs=pl.BlockSpec(memory_space=pl.ANY),
        scratch_shapes=[pltpu.VMEM((2, tile, d), F32), pltpu.VMEM((2, tile, d), F32),
                        pltpu.SemaphoreType.DMA((2,)), pltpu.SemaphoreType.DMA((2,))],
    )
    return pl.pallas_call(
        functools.partial(_moe_kernel, tile=tile),
        grid_spec=grid_spec,
        out_shape=jax.ShapeDtypeStruct((s, d), F32),
        compiler_params=_cparams(("arbitrary",)),
        name="moe_experts",
    )(src, e_lo, e_hi, nvalid, n_tiles.reshape(1), x1, w_sorted, gf, wg, wu, wd, wg, wu, wd)


def _front_weight(w_in):
    d = w_in.shape[0]
    sizes = (POOL_W, ATTN_W, ATTN_W, ATTN_W, IDX_HEADS * IDX_DIM, IDX_DIM, IDX_HEADS, XATTN_W)
    offs = [0]
    for sz in sizes:
        offs.append(offs[-1] + sz)
    w_pool, w_q, w_k, w_v, w_qi, w_ki, w_wi, w_xq = (w_in[:, offs[n]:offs[n + 1]] for n in range(len(sizes)))
    pad = lambda w, to: jnp.pad(w, ((0, 0), (0, to - w.shape[1])))
    w_qi = jnp.pad(w_qi.reshape(d, IDX_HEADS, IDX_DIM), ((0, 0), (0, 0), (0, LANES - IDX_DIM)))
    front = jnp.concatenate([w_pool, w_q, w_k, w_v, w_xq, w_qi.reshape(d, IDX_HEADS * LANES),
                             pad(w_ki, LANES), pad(w_wi, LANES)], axis=1)
    return front, w_in[:, offs[-1]:]


def _rope_inv_freq():
    def inv(dim):
        return ROPE_THETA ** (-jnp.arange(0, dim, 2, dtype=F32) / dim)
    return jnp.stack([jnp.tile(inv(HEAD_DIM), LANES // (HEAD_DIM // 2)),
                      jnp.tile(inv(IDX_DIM), LANES // (IDX_DIM // 2))])


def _layer(x, mem, pos, g_mix, w_in, b_gate, w_pool_grp, pool_scale, q_norm_g, k_norm_g, g_mem, w_mem_kv,
           xq_norm_g, xk_norm_g, w_pool_out, w_attn_out, w_cross_out, w_o, g_ffn, w_router_group,
           w_router_expert, w_e_gate, w_e_up, w_e_down):
    s, d = x.shape
    row2 = lambda v: v.reshape(1, -1)
    k_sel = min(TOPK_MAX, s // 4)

    w_front, w_gate = _front_weight(w_in)
    front = _norm_proj(x, row2(g_mix), w_front.astype(BF16), None, tm=512, tn=FRONT_W // 2, out_dtype=F32)
    gates = _norm_proj(x, row2(g_mix), w_gate.astype(BF16), row2(b_gate), tm=1024, tn=1024, out_dtype=BF16)

    q, k, v, xq, qi, ki, wib = _prep(front, pos.reshape(s, 1), _rope_inv_freq(), row2(q_norm_g),
                                     row2(k_norm_g), row2(xq_norm_g), tm=512)

    bd = jax.scipy.linalg.block_diag(*[w_pool_grp[g] for g in range(len(POOL_WINDOWS))])
    pool_pre = _pool(front, bd.astype(BF16), row2(pool_scale), tm=512)

    k_m, v_m = _memkv(mem, row2(g_mem), w_mem_kv.astype(BF16), row2(xk_norm_g))
    cross = _cross(xq, k_m, v_m, tm=512)

    attn = _dsa(qi, wib, q, ki, k, v, tq=256, kc=512, k_sel=k_sel)

    w_router = jnp.pad(jnp.concatenate([w_router_group, w_router_expert], axis=1),
                       ((0, 0), (0, LANES - N_GROUPS - N_EXPERTS)))
    x1, route = _merge(x, pool_pre, attn, cross, gates, w_pool_out.astype(BF16), w_attn_out.astype(BF16),
                       w_cross_out.astype(BF16), w_o.astype(BF16), row2(g_ffn), w_router.astype(BF16), tm=256)
    return _moe(x1, route, row2(g_ffn), w_e_gate.astype(BF16), w_e_up.astype(BF16), w_e_down.astype(BF16),
                tile=256)


def kernel(x, mem, positions, g_mix, w_in, b_gate, w_pool_grp, pool_scale, q_norm_g, k_norm_g, g_mem, w_mem_kv,
           xq_norm_g, xk_norm_g, w_pool_out, w_attn_out, w_cross_out, w_o, g_ffn, w_router_group,
           w_router_expert, w_e_gate, w_e_up, w_e_down):
    depth = g_mix.shape[0]
    outs = []
    for b in range(x.shape[0]):
        xb = x[b]
        for l in range(depth):
            xb = _layer(xb, mem[b], positions[b], g_mix[l], w_in[l], b_gate[l], w_pool_grp[l], pool_scale[l],
                        q_norm_g[l], k_norm_g[l], g_mem[l], w_mem_kv[l], xq_norm_g[l], xk_norm_g[l],
                        w_pool_out[l], w_attn_out[l], w_cross_out[l], w_o[l], g_ffn[l], w_router_group[l],
                        w_router_expert[l], w_e_gate[l], w_e_up[l], w_e_down[l])
        outs.append(xb)
    return jnp.stack(outs)
```

```python
import functools
import math

import jax
import jax.numpy as jnp
from jax import lax
from jax.experimental import pallas as pl
from jax.experimental.pallas import tpu as pltpu

F32 = jnp.float32
BF16 = jnp.bfloat16
I32 = jnp.int32
I16 = jnp.int16
I16_MIN = -32768

NORM_EPS = 1e-6
ROPE_THETA = 10000.0
HEAD_DIM = 128
ATTN_HEADS = 6
ATTN_W = ATTN_HEADS * HEAD_DIM
IDX_HEADS = 4
IDX_DIM = 64
XATTN_HEADS = 4
XATTN_W = XATTN_HEADS * HEAD_DIM
POOL_WINDOWS = (2, 4, 8, 16)
POOL_GROUP_W = 192
POOL_W = len(POOL_WINDOWS) * POOL_GROUP_W
POOL_HALO = 16
TOPK_MAX = 256
N_GROUPS = 4
EXPERTS_PER_GROUP = 4
N_EXPERTS = N_GROUPS * EXPERTS_PER_GROUP
N_BRANCH = 3
PAIRS_PER_GROUP = EXPERTS_PER_GROUP * (EXPERTS_PER_GROUP - 1) // 2
ROUTE_BUCKET, ROUTE_RANK, ROUTE_W_LO, ROUTE_W_HI, ROUTE_COLS = 0, 1, 2, 3, 4

LANES = 128
SUBLANES = 8
V7X_VMEM_LIMIT_BYTES = 56 * 1024 * 1024

COL_POOL = 0
COL_Q = COL_POOL + POOL_W
COL_K = COL_Q + ATTN_W
COL_V = COL_K + ATTN_W
MAIN_W = COL_V + ATTN_W
MAIN_TN = 512
TAIL_XQ = 0
TAIL_QI = TAIL_XQ + XATTN_W
TAIL_KI = TAIL_QI + IDX_HEADS * LANES
TAIL_WI = TAIL_KI + LANES
TAIL_W = TAIL_WI + LANES
GATE_COL0 = MAIN_W + IDX_HEADS * IDX_DIM + IDX_DIM + IDX_HEADS + XATTN_W
GATE_TN = 1280
GATE_WINDOW0 = GATE_COL0 // GATE_TN * GATE_TN
GATE_SHIFT = GATE_COL0 - GATE_WINDOW0
assert MAIN_W % MAIN_TN == 0 and GATE_SHIFT < LANES
IDX_PITCH = IDX_HEADS * LANES + LANES

NEG_BIG = -1e30
INT_MIN = -2147483648
KEY_NEG_INF = -2139095041
SOFTMAX_SUM_MIN = 2.0 ** -64
SOFTMAX_SUM_MAX = 2.0 ** 100
NT_DIMS = (((1,), (1,)), ((), ()))


def _cparams(semantics):
    return pltpu.CompilerParams(dimension_semantics=semantics, vmem_limit_bytes=V7X_VMEM_LIMIT_BYTES)


def _resident(shape):
    return pl.BlockSpec(shape, lambda *_: (0,) * len(shape), pipeline_mode=pl.Buffered(1))


def _rms_rows(x, g):
    ms = jnp.mean(x * x, axis=-1, keepdims=True)
    return x * lax.rsqrt(ms + NORM_EPS) * g


def _sigmoid(x):
    return 1.0 / (1.0 + jnp.exp(-x))


def _main_proj_kernel(x_ref, g_ref, w_ref, o_ref, h_ref):
    @pl.when(pl.program_id(1) == 0)
    def _():
        h_ref[...] = _rms_rows(x_ref[...], g_ref[...]).astype(BF16)

    o_ref[...] = jnp.dot(h_ref[...], w_ref[...].astype(BF16), preferred_element_type=F32)


def _main_proj(x, g, w_in, *, tm):
    s, d = x.shape
    return pl.pallas_call(
        _main_proj_kernel,
        grid=(s // tm, MAIN_W // MAIN_TN),
        in_specs=[pl.BlockSpec((tm, d), lambda i, j: (i, 0)),
                  pl.BlockSpec((1, d), lambda i, j: (0, 0)),
                  pl.BlockSpec((d, MAIN_TN), lambda i, j: (0, j))],
        out_specs=[pl.BlockSpec((tm, MAIN_TN), lambda i, j: (i, j)),
                   pl.BlockSpec((tm, d), lambda i, j: (i, 0))],
        out_shape=[jax.ShapeDtypeStruct((s, MAIN_W), F32), jax.ShapeDtypeStruct((s, d), BF16)],
        compiler_params=_cparams(("parallel", "arbitrary")),
        name="norm_proj_main",
    )(x, g, w_in)


def _tail_proj_kernel(h_ref, w_ref, o_ref):
    o_ref[...] = jnp.dot(h_ref[...], w_ref[...], preferred_element_type=F32)


def _tail_proj(h, w_tail, *, tm):
    s, d = h.shape
    return pl.pallas_call(
        _tail_proj_kernel,
        grid=(s // tm,),
        in_specs=[pl.BlockSpec((tm, d), lambda i: (i, 0)), _resident(w_tail.shape)],
        out_specs=pl.BlockSpec((tm, TAIL_W), lambda i: (i, 0)),
        out_shape=jax.ShapeDtypeStruct((s, TAIL_W), F32),
        compiler_params=_cparams(("parallel",)),
        name="proj_tail",
    )(h, w_tail)


def _gate_proj_kernel(h_ref, w_ref, b_ref, o_ref, wb_ref, *, n_cols):
    @pl.when(pl.program_id(1) == 0)
    def _():
        col = GATE_WINDOW0 + pl.program_id(0) * GATE_TN + lax.broadcasted_iota(I32, (1, GATE_TN), 1)
        wb_ref[...] = jnp.where(col < n_cols, w_ref[...], 0.0).astype(BF16)

    acc = jnp.dot(h_ref[...], wb_ref[...], preferred_element_type=F32)
    o_ref[...] = _sigmoid(acc + b_ref[...]).astype(BF16)


def _gate_proj(h, w_in, bias_window, *, tm):
    s, d = h.shape
    n = bias_window.shape[1]
    first = GATE_WINDOW0 // GATE_TN
    return pl.pallas_call(
        functools.partial(_gate_proj_kernel, n_cols=w_in.shape[1]),
        grid=(n // GATE_TN, s // tm),
        in_specs=[pl.BlockSpec((tm, d), lambda j, i: (i, 0)),
                  pl.BlockSpec((d, GATE_TN), lambda j, i: (0, first + j)),
                  pl.BlockSpec((1, GATE_TN), lambda j, i: (0, j))],
        out_specs=pl.BlockSpec((tm, GATE_TN), lambda j, i: (i, j)),
        out_shape=jax.ShapeDtypeStruct((s, n), BF16),
        scratch_shapes=[pltpu.VMEM((d, GATE_TN), BF16)],
        compiler_params=_cparams(("parallel", "arbitrary")),
        name="gate_proj",
    )(h, w_in, bias_window)


def _prep_kernel(f_ref, t_ref, pos_ref, invf_ref, qg_ref, kg_ref, xg_ref,
                 q_ref, k_ref, v_ref, xq_ref, qi_ref, ki_ref, wib_ref):
    tm = f_ref.shape[0]
    pos = pos_ref[...].astype(F32)
    lane = lax.broadcasted_iota(I32, (1, LANES), 1)
    ang = pos * invf_ref[0:1, :]
    cos_a = jnp.cos(ang)
    sin_a = jnp.where(lane < HEAD_DIM // 2, -jnp.sin(ang), jnp.sin(ang))
    ang_i = pos * invf_ref[1:2, :]
    low_i = (lane & (IDX_DIM - 1)) < IDX_DIM // 2
    cos_i = jnp.cos(ang_i)
    sin_i = jnp.where(low_i, -jnp.sin(ang_i), jnp.sin(ang_i))

    def rope_head(y):
        return y * cos_a + pltpu.roll(y, HEAD_DIM // 2, 1) * sin_a

    def rope_idx(y):
        partner = jnp.where(low_i, pltpu.roll(y, LANES - IDX_DIM // 2, 1), pltpu.roll(y, IDX_DIM // 2, 1))
        return y * cos_i + partner * sin_i

    for h in range(ATTN_HEADS):
        sl = slice(h * HEAD_DIM, (h + 1) * HEAD_DIM)
        qh = f_ref[:, COL_Q + h * HEAD_DIM:COL_Q + (h + 1) * HEAD_DIM]
        kh = f_ref[:, COL_K + h * HEAD_DIM:COL_K + (h + 1) * HEAD_DIM]
        q_ref[:, sl] = rope_head(_rms_rows(qh, qg_ref[...])).astype(BF16)
        k_ref[:, sl] = rope_head(_rms_rows(kh, kg_ref[...])).astype(BF16)
    v_ref[...] = f_ref[:, COL_V:COL_V + ATTN_W].astype(BF16)
    for h in range(XATTN_HEADS):
        sl = slice(h * HEAD_DIM, (h + 1) * HEAD_DIM)
        xh = t_ref[:, TAIL_XQ + h * HEAD_DIM:TAIL_XQ + (h + 1) * HEAD_DIM]
        xq_ref[:, sl] = _rms_rows(xh, xg_ref[...]).astype(BF16)
    for h in range(IDX_HEADS):
        sl = slice(h * LANES, (h + 1) * LANES)
        qi_ref[:, sl] = rope_idx(t_ref[:, TAIL_QI + h * LANES:TAIL_QI + (h + 1) * LANES]).astype(BF16)
    ki_ref[...] = rope_idx(t_ref[:, TAIL_KI:TAIL_KI + LANES]).astype(BF16)
    wi = t_ref[:, TAIL_WI:TAIL_WI + LANES] * (IDX_HEADS ** -0.5) * (IDX_DIM ** -0.5)
    for h in range(IDX_HEADS):
        wib_ref[:, h * LANES:(h + 1) * LANES] = jnp.broadcast_to(wi[:, h:h + 1], (tm, LANES))
    qi_ref[:, IDX_HEADS * LANES:] = jnp.zeros((tm, LANES), BF16)
    wib_ref[:, IDX_HEADS * LANES:] = jnp.zeros((tm, LANES), F32)


def _prep(main, tail, pos_col, invf, qg, kg, xg, *, tm):
    s = main.shape[0]
    row = lambda w: pl.BlockSpec((tm, w), lambda i: (i, 0))
    small = lambda shape: pl.BlockSpec(shape, lambda i: (0, 0))
    outs = [(row(ATTN_W), (s, ATTN_W), BF16), (row(ATTN_W), (s, ATTN_W), BF16), (row(ATTN_W), (s, ATTN_W), BF16),
            (row(XATTN_W), (s, XATTN_W), BF16), (row(IDX_PITCH), (s, IDX_PITCH), BF16),
            (row(LANES), (s, LANES), BF16), (row(IDX_PITCH), (s, IDX_PITCH), F32)]
    return pl.pallas_call(
        _prep_kernel,
        grid=(s // tm,),
        in_specs=[row(MAIN_W), row(TAIL_W), row(1), small((2, LANES)), small((1, LANES)), small((1, LANES)),
                  small((1, LANES))],
        out_specs=[spec for spec, _, _ in outs],
        out_shape=[jax.ShapeDtypeStruct(shape, dt) for _, shape, dt in outs],
        compiler_params=_cparams(("parallel",)),
        name="qknorm_rope",
    )(main, tail, pos_col, invf, qg, kg, xg)


def _pool_kernel(u_ref, halo_ref, bd_ref, sc_ref, o_ref, buf_ref):
    i = pl.program_id(0)
    tm = u_ref.shape[0]
    u = u_ref[...]
    buf_ref[0:POOL_HALO, :] = jnp.where(i == 0, 0.0, halo_ref[...])
    buf_ref[POOL_HALO:POOL_HALO + tm, :] = u
    t = i * tm + lax.broadcasted_iota(I32, (tm, 1), 0)
    lane = lax.broadcasted_iota(I32, (1, POOL_W), 1)
    acc = u
    pooled = None
    for d in range(1, max(POOL_WINDOWS)):
        acc = acc + buf_ref[POOL_HALO - d:POOL_HALO - d + tm, :]
        if d + 1 in POOL_WINDOWS:
            g = POOL_WINDOWS.index(d + 1)
            mean = acc / jnp.minimum(t + 1, d + 1).astype(F32)
            in_group = (lane >= g * POOL_GROUP_W) & (lane < (g + 1) * POOL_GROUP_W)
            pooled = jnp.where(in_group, mean, 0.0 if pooled is None else pooled)
    p = (pooled - u).astype(BF16)
    y = jnp.dot(p, bd_ref[...], preferred_element_type=F32) * sc_ref[...]
    o_ref[...] = y.astype(BF16)


def _pool(front, bd, scale, *, tm):
    s = front.shape[0]
    halo_blocks = tm // POOL_HALO
    return pl.pallas_call(
        _pool_kernel,
        grid=(s // tm,),
        in_specs=[
            pl.BlockSpec((tm, POOL_W), lambda i: (i, 0)),
            pl.BlockSpec((POOL_HALO, POOL_W), lambda i: (jnp.maximum(i * halo_blocks - 1, 0), 0)),
            pl.BlockSpec((POOL_W, POOL_W), lambda i: (0, 0)),
            pl.BlockSpec((1, POOL_W), lambda i: (0, 0)),
        ],
        out_specs=pl.BlockSpec((tm, POOL_W), lambda i: (i, 0)),
        out_shape=jax.ShapeDtypeStruct((s, POOL_W), BF16),
        scratch_shapes=[pltpu.VMEM((tm + POOL_HALO, POOL_W), F32)],
        compiler_params=_cparams(("parallel",)),
        name="pool_mixer",
    )(front, front, bd, scale)


def _memkv_kernel(mem_ref, g_ref, w_ref, kg_ref, k_ref, v_ref):
    h = _rms_rows(mem_ref[...], g_ref[...]).astype(BF16)
    kv = jnp.dot(h, w_ref[...], preferred_element_type=F32)
    for hd in range(XATTN_HEADS):
        sl = slice(hd * HEAD_DIM, (hd + 1) * HEAD_DIM)
        k_ref[:, sl] = _rms_rows(kv[:, sl], kg_ref[...]).astype(BF16)
    v_ref[...] = kv[:, XATTN_W:].astype(BF16)


def _memkv(mem, g, w, kg):
    m = mem.shape[0]
    return pl.pallas_call(
        _memkv_kernel,
        out_shape=[jax.ShapeDtypeStruct((m, XATTN_W), BF16), jax.ShapeDtypeStruct((m, XATTN_W), BF16)],
        compiler_params=pltpu.CompilerParams(vmem_limit_bytes=V7X_VMEM_LIMIT_BYTES),
        name="mem_kv",
    )(mem, g, w, kg)


def _cross_kernel(xq_ref, k_ref, v_ref, o_ref):
    for h in range(XATTN_HEADS):
        sl = slice(h * HEAD_DIM, (h + 1) * HEAD_DIM)
        logits = lax.dot_general(xq_ref[:, sl], k_ref[:, sl], NT_DIMS,
                                 preferred_element_type=F32) * (HEAD_DIM ** -0.5)
        e = jnp.exp(logits - jnp.max(logits, axis=-1, keepdims=True))
        p = e / jnp.sum(e, axis=-1, keepdims=True)
        o_ref[:, sl] = jnp.dot(p.astype(BF16), v_ref[:, sl], preferred_element_type=F32).astype(BF16)


def _cross(xq, k_m, v_m, *, tm):
    s = xq.shape[0]
    m = k_m.shape[0]
    return pl.pallas_call(
        _cross_kernel,
        grid=(s // tm,),
        in_specs=[pl.BlockSpec((tm, XATTN_W), lambda i: (i, 0)),
                  pl.BlockSpec((m, XATTN_W), lambda i: (0, 0)),
                  pl.BlockSpec((m, XATTN_W), lambda i: (0, 0))],
        out_specs=pl.BlockSpec((tm, XATTN_W), lambda i: (i, 0)),
        out_shape=jax.ShapeDtypeStruct((s, XATTN_W), BF16),
        compiler_params=_cparams(("parallel",)),
        name="mem_cross_attn",
    )(xq, k_m, v_m)


def _key_to_f32(key):
    bits = key ^ ((key >> 31) & 0x7FFFFFFF)
    return lax.bitcast_convert_type(bits, F32)


def _dsa_kernel(qi_ref, wib_ref, q_ref, ki_ref, k_ref, v_ref, tri_ref, o_ref, sc_ref, hi_ref, lo_ref,
                *, tq, kc, k_sel):
    i = pl.program_id(0)
    n_chunks = ((i + 1) * tq + kc - 1) // kc
    t = i * tq + lax.broadcasted_iota(I32, (tq, 1), 0)
    slabs = kc // LANES

    def slab(c, j):
        return pl.ds(pl.multiple_of(c * kc, kc) + j * LANES, LANES)

    def score_chunk(c, carry):
        off = pl.multiple_of(c * kc, kc)
        ki_c = ki_ref[pl.ds(off, kc), :]
        dots = [lax.dot_general(qi_ref[:, h * LANES:(h + 1) * LANES], ki_c, NT_DIMS,
                                preferred_element_type=F32) for h in range(IDX_HEADS)]
        for j in range(slabs):
            s = jnp.zeros((tq, LANES), F32)
            for h in range(IDX_HEADS):
                s = s + jnp.maximum(dots[h][:, j * LANES:(j + 1) * LANES], 0.0) * wib_ref[:, h * LANES:(h + 1) * LANES]
            kpos = off + j * LANES + lax.broadcasted_iota(I32, (1, LANES), 1)
            s = jnp.where(kpos <= t, jnp.where(s == 0.0, 0.0, s), -jnp.inf)
            sc_ref[:, slab(c, j)] = s
            bits = pltpu.bitcast(s, I32)
            key = bits ^ ((bits >> 31) & 0x7FFFFFFF)
            hi_ref[:, slab(c, j)] = (key >> 16).astype(I16)
            lo_ref[:, slab(c, j)] = ((key & 0xFFFF) + I16_MIN).astype(I16)
        return carry

    lax.fori_loop(0, n_chunks, score_chunk, 0)

    def count16(ref, cand16):
        def body(c, acc):
            for j in range(slabs):
                acc = acc + jnp.where(ref[:, slab(c, j)] >= cand16, jnp.int16(1), jnp.int16(0))
            return acc
        acc = lax.fori_loop(0, n_chunks, body, jnp.zeros((tq, LANES), I16))
        return jnp.sum(acc.astype(I32), axis=-1, keepdims=True)

    def search16(ref, counted_already):
        def step(b, best):
            cand = best + jnp.left_shift(jnp.int32(1), 15 - b)
            enough = count16(ref, cand.astype(I16)) + counted_already >= k_sel
            return jnp.where(enough, cand, best)
        return lax.fori_loop(0, 16, step, jnp.full((tq, LANES), I16_MIN, I32))

    hi_best = search16(hi_ref, 0)
    hi_best16 = hi_best.astype(I16)

    def split_chunk(c, acc):
        for j in range(slabs):
            h = hi_ref[:, slab(c, j)]
            lo_ref[:, slab(c, j)] = jnp.where(h == hi_best16, lo_ref[:, slab(c, j)], jnp.int16(I16_MIN))
            acc = acc + jnp.where(h > hi_best16, jnp.int16(1), jnp.int16(0))
        return acc

    above = lax.fori_loop(0, n_chunks, split_chunk, jnp.zeros((tq, LANES), I16))
    lo_best = search16(lo_ref, jnp.sum(above.astype(I32), axis=-1, keepdims=True))
    thr_key = jnp.left_shift(hi_best, 16) + (lo_best - I16_MIN)

    def count(pred):
        def body(c, acc):
            for j in range(slabs):
                acc = acc + pred(sc_ref[:, slab(c, j)]).astype(I32)
            return acc
        acc = lax.fori_loop(0, n_chunks, body, jnp.zeros((tq, LANES), I32))
        return jnp.sum(acc, axis=-1, keepdims=True)

    thr_found = _key_to_f32(thr_key)
    thr = jnp.where(count(lambda s: s >= thr_found) >= k_sel, thr_found, _key_to_f32(thr_key - 1))
    need = (k_sel - count(lambda s: s > thr)).astype(F32)
    thr_col = thr[:, 0:1]

    def mask_chunk(c, ties_before):
        off = pl.multiple_of(c * kc, kc)
        s = sc_ref[:, pl.ds(off, kc)]
        tie = s == thr_col
        prefix = jnp.dot(jnp.where(tie, 1.0, 0.0).astype(BF16), tri_ref[:, 0:kc], preferred_element_type=F32)
        kpos = off + lax.broadcasted_iota(I32, (1, kc), 1)
        take_tie = tie & (prefix + ties_before <= need)
        sel = ((s > thr_col) | take_tie) & (kpos <= t)
        sc_ref[:, pl.ds(off, kc)] = jnp.where(sel, 0.0, NEG_BIG)
        return ties_before + prefix[:, kc - 1:kc]

    lax.fori_loop(0, n_chunks, mask_chunk, jnp.zeros((tq, 1), F32))

    scale2 = (HEAD_DIM ** -0.5) * math.log2(math.e)
    heads = [slice(h * HEAD_DIM, (h + 1) * HEAD_DIM) for h in range(ATTN_HEADS)]

    def unshifted_chunk(c, carry):
        off = pl.multiple_of(c * kc, kc)
        bias = sc_ref[:, pl.ds(off, kc)]
        out = []
        for sl, (lane_sum, acc) in zip(heads, carry):
            p = jnp.exp2(lax.dot_general(q_ref[:, sl], k_ref[pl.ds(off, kc), sl], NT_DIMS,
                                         preferred_element_type=F32) * scale2 + bias)
            for j in range(slabs):
                lane_sum = lane_sum + p[:, j * LANES:(j + 1) * LANES]
            acc = acc + jnp.dot(p.astype(BF16), v_ref[pl.ds(off, kc), sl], preferred_element_type=F32)
            out.append((lane_sum, acc))
        return tuple(out)

    zeros = jnp.zeros((tq, LANES), F32)
    result = lax.fori_loop(0, n_chunks, unshifted_chunk, tuple((zeros, zeros) for _ in heads))
    in_range = None
    for sl, (lane_sum, acc) in zip(heads, result):
        row_sum = jnp.sum(lane_sum, axis=-1, keepdims=True)
        o_ref[:, sl] = (acc / row_sum).astype(BF16)
        ok = (row_sum >= SOFTMAX_SUM_MIN) & (row_sum <= SOFTMAX_SUM_MAX)
        in_range = ok if in_range is None else (in_range & ok)
    n_bad = jnp.sum(jnp.where(in_range, 0.0, 1.0))

    @pl.when(n_bad > 0.0)
    def _():
        _shifted_attention(q_ref, k_ref, v_ref, sc_ref, o_ref, n_chunks, tq=tq, kc=kc, scale2=scale2)


def _shifted_attention(q_ref, k_ref, v_ref, sc_ref, o_ref, n_chunks, *, tq, kc, scale2):
    slabs = kc // LANES
    for h in range(ATTN_HEADS):
        sl = slice(h * HEAD_DIM, (h + 1) * HEAD_DIM)

        def logits(c, sl=sl):
            off = pl.multiple_of(c * kc, kc)
            return lax.dot_general(q_ref[:, sl], k_ref[pl.ds(off, kc), sl], NT_DIMS,
                                   preferred_element_type=F32) * scale2 + sc_ref[:, pl.ds(off, kc)]

        def max_chunk(c, lane_max, logits=logits):
            s = logits(c)
            for j in range(slabs):
                lane_max = jnp.maximum(lane_max, s[:, j * LANES:(j + 1) * LANES])
            return lane_max

        lane_max = lax.fori_loop(0, n_chunks, max_chunk, jnp.full((tq, LANES), NEG_BIG, F32))
        row_max = jnp.max(lane_max, axis=-1, keepdims=True)

        def pv_chunk(c, carry, sl=sl, row_max=row_max, logits=logits):
            lane_sum, acc = carry
            off = pl.multiple_of(c * kc, kc)
            p = jnp.exp2(logits(c) - row_max)
            for j in range(slabs):
                lane_sum = lane_sum + p[:, j * LANES:(j + 1) * LANES]
            acc = acc + jnp.dot(p.astype(BF16), v_ref[pl.ds(off, kc), sl], preferred_element_type=F32)
            return lane_sum, acc

        lane_sum, acc = lax.fori_loop(0, n_chunks, pv_chunk,
                                      (jnp.zeros((tq, LANES), F32), jnp.zeros((tq, HEAD_DIM), F32)))
        o_ref[:, sl] = (acc / jnp.sum(lane_sum, axis=-1, keepdims=True)).astype(BF16)


def _dsa(qi, wib, q, ki, k, v, *, tq, kc, k_sel):
    s = q.shape[0]
    tri = jnp.pad(jnp.triu(jnp.ones((kc, kc), BF16)), ((0, 0), (0, LANES)))
    row = lambda w: pl.BlockSpec((tq, w), lambda i: (i, 0))
    return pl.pallas_call(
        functools.partial(_dsa_kernel, tq=tq, kc=kc, k_sel=k_sel),
        grid=(s // tq,),
        in_specs=[row(IDX_PITCH), row(IDX_PITCH), row(ATTN_W),
                  _resident((s, LANES)), _resident((s, ATTN_W)), _resident((s, ATTN_W)),
                  _resident(tri.shape)],
        out_specs=row(ATTN_W),
        out_shape=jax.ShapeDtypeStruct((s, ATTN_W), BF16),
        scratch_shapes=[pltpu.VMEM((tq, s + LANES), F32), pltpu.VMEM((tq, s + LANES), I16),
                        pltpu.VMEM((tq, s + LANES), I16)],
        compiler_params=_cparams(("parallel",)),
        name="dsa_attention",
    )(qi, wib, q, ki, k, v, tri)


def _route(logits):
    lane = lax.broadcasted_iota(I32, (1, LANES), 1).astype(F32)
    far = float(LANES)
    is_group = lane < N_GROUPS
    g_max = jnp.max(jnp.where(is_group, logits, -jnp.inf), axis=-1, keepdims=True)
    g_top = jnp.min(jnp.where(is_group & (logits == g_max), lane, far), axis=-1, keepdims=True)
    pg_top = 1.0 / jnp.sum(jnp.where(is_group, jnp.exp(logits - g_max), 0.0), axis=-1, keepdims=True)
    first = N_GROUPS + EXPERTS_PER_GROUP * g_top
    in_grp = (lane >= first) & (lane < first + EXPERTS_PER_GROUP)
    e_max = jnp.max(jnp.where(in_grp, logits, -jnp.inf), axis=-1, keepdims=True)
    e_exp = jnp.where(in_grp, jnp.exp(logits - e_max), 0.0)
    pe = jnp.where(in_grp, e_exp / jnp.sum(e_exp, axis=-1, keepdims=True), -1.0)
    p1 = jnp.max(pe, axis=-1, keepdims=True)
    e1 = jnp.min(jnp.where(pe == p1, lane, far), axis=-1, keepdims=True)
    pe_rest = jnp.where(lane == e1, -1.0, pe)
    p2 = jnp.max(pe_rest, axis=-1, keepdims=True)
    e2 = jnp.min(jnp.where(pe_rest == p2, lane, far), axis=-1, keepdims=True)
    den = p1 + p2
    w1 = pg_top * p1 / den
    w2 = pg_top * p2 / den
    lo = jnp.minimum(e1, e2) - first
    hi = jnp.maximum(e1, e2) - first
    pair = lo * (2 * EXPERTS_PER_GROUP - 1 - lo) * 0.5 + (hi - lo - 1.0)
    first_is_lo = e1 < e2
    return (g_top * PAIRS_PER_GROUP + pair, jnp.where(first_is_lo, w1, w2), jnp.where(first_is_lo, w2, w1))


def _merge_kernel(x_ref, p_ref, a_ref, c_ref, gt_ref, wpo_ref, wao_ref, wco_ref, wo_ref, gf_ref, wr_ref,
                  tri_ref, x1_ref, route_ref, cnt_ref):
    @pl.when(pl.program_id(0) == 0)
    def _():
        cnt_ref[...] = jnp.zeros_like(cnt_ref)

    d = x_ref.shape[1]
    def gate(b):
        window = gt_ref[:, b * d:(b + 1) * d + LANES].astype(F32)
        return window[:, GATE_SHIFT:GATE_SHIFT + d]

    merged = gate(0) * jnp.dot(p_ref[...], wpo_ref[...], preferred_element_type=F32)
    merged += gate(1) * jnp.dot(a_ref[...], wao_ref[...], preferred_element_type=F32)
    merged += gate(2) * jnp.dot(c_ref[...], wco_ref[...], preferred_element_type=F32)
    x1 = x_ref[...] + jnp.dot(merged.astype(BF16), wo_ref[...], preferred_element_type=F32)
    x1_ref[...] = x1
    h2 = _rms_rows(x1, gf_ref[...]).astype(BF16)
    bucket, w_lo, w_hi = _route(jnp.dot(h2, wr_ref[...], preferred_element_type=F32))
    lane = lax.broadcasted_iota(I32, (1, LANES), 1).astype(F32)
    onehot = lane == bucket
    before = jnp.dot(tri_ref[...], jnp.where(onehot, 1.0, 0.0).astype(BF16), preferred_element_type=F32)
    rank = jnp.sum(jnp.where(onehot, before + cnt_ref[...], 0.0), axis=-1, keepdims=True)
    cnt_ref[...] += jnp.sum(jnp.where(onehot, 1.0, 0.0), axis=0, keepdims=True)
    route_ref[...] = jnp.where(lane == ROUTE_BUCKET, bucket,
                               jnp.where(lane == ROUTE_RANK, rank,
                                         jnp.where(lane == ROUTE_W_LO, w_lo,
                                                   jnp.where(lane == ROUTE_W_HI, w_hi, 0.0))))


def _merge(x, p, a, c, gates, wpo, wao, wco, wo, gf, wr, *, tm):
    s, d = x.shape
    row = lambda w: pl.BlockSpec((tm, w), lambda i: (i, 0))
    tri = jnp.tril(jnp.ones((tm, tm), BF16), -1)
    return pl.pallas_call(
        _merge_kernel,
        grid=(s // tm,),
        in_specs=[row(d), row(POOL_W), row(ATTN_W), row(XATTN_W), row(gates.shape[1]),
                  _resident(wpo.shape), _resident(wao.shape), _resident(wco.shape), _resident(wo.shape),
                  _resident(gf.shape), _resident(wr.shape), _resident(tri.shape)],
        out_specs=[row(d), row(LANES)],
        out_shape=[jax.ShapeDtypeStruct((s, d), F32), jax.ShapeDtypeStruct((s, LANES), F32)],
        scratch_shapes=[pltpu.VMEM((1, LANES), F32)],
        compiler_params=_cparams(("arbitrary",)),
        name="gated_merge_router",
    )(x, p, a, c, gates, wpo, wao, wco, wo, gf, wr, tri)


def _moe_kernel(src_ref, elo_ref, ehi_ref, nvalid_ref, ntiles_ref,
                x1_hbm, w_ref, gf_ref, wg_lo, wu_lo, wd_lo, wg_hi, wu_hi, wd_hi,
                out_hbm, xbuf, obuf, gsem, ssem, *, tile):
    j = pl.program_id(0)
    n_tiles = ntiles_ref[0]
    slot = lax.rem(j, 2)

    def start_gather(t, s):
        def body(r, carry):
            tok = src_ref[t * tile + r]
            pltpu.make_async_copy(x1_hbm.at[pl.ds(tok, 1)], xbuf.at[s, pl.ds(r, 1)], gsem.at[s]).start()
            return carry
        lax.fori_loop(0, tile, body, 0, unroll=8)

    def wait_scatter(t, s):
        n = nvalid_ref[t]
        n_whole = pl.multiple_of(lax.shift_left(lax.shift_right_logical(n, 3), 3), SUBLANES)

        @pl.when(n_whole > 0)
        def _():
            pltpu.make_async_copy(obuf.at[s, pl.ds(0, n_whole)], out_hbm.at[pl.ds(0, n_whole)], ssem.at[s]).wait()

        def wait_row(r, carry):
            pltpu.make_async_copy(obuf.at[s, pl.ds(0, 1)], out_hbm.at[pl.ds(0, 1)], ssem.at[s]).wait()
            return carry
        lax.fori_loop(0, n - n_whole, wait_row, 0)

    @pl.when(j == 0)
    def _():
        start_gather(0, 0)

    @pl.when(j + 1 < n_tiles)
    def _():
        start_gather(j + 1, 1 - slot)

    @pl.when(j < n_tiles)
    def _():
        pltpu.make_async_copy(x1_hbm.at[pl.ds(0, tile)], xbuf.at[slot], gsem.at[slot]).wait()
        x1 = xbuf[slot]
        h = _rms_rows(x1, gf_ref[...]).astype(BF16)
        y = x1
        for w_col, wg, wu, wd in ((ROUTE_W_LO, wg_lo, wu_lo, wd_lo), (ROUTE_W_HI, wg_hi, wu_hi, wd_hi)):
            a = jnp.dot(h, wg[0], preferred_element_type=F32)
            b = jnp.dot(h, wu[0], preferred_element_type=F32)
            act = (a * _sigmoid(a)) * b * w_ref[:, w_col:w_col + 1]
            y = y + jnp.dot(act.astype(BF16), wd[0], preferred_element_type=F32)
        obuf[slot] = y

        def scatter_row(r, carry):
            tok = src_ref[j * tile + r]
            pltpu.make_async_copy(obuf.at[slot, pl.ds(r, 1)], out_hbm.at[pl.ds(tok, 1)], ssem.at[slot]).start()
            return carry
        lax.fori_loop(0, nvalid_ref[j], scatter_row, 0)

        @pl.when(j >= 1)
        def _():
            wait_scatter(j - 1, 1 - slot)

        @pl.when(j == n_tiles - 1)
        def _():
            wait_scatter(j, slot)


def _moe(x1, route, gf, wg, wu, wd, *, tile):
    s, d = x1.shape
    ff = wg.shape[2]
    n_buckets = N_GROUPS * PAIRS_PER_GROUP
    max_tiles = (s + n_buckets * (tile - 1)) // tile
    bucket = route[:, ROUTE_BUCKET].astype(I32)
    rank = route[:, ROUTE_RANK].astype(I32)
    counts = jnp.zeros((n_buckets,), I32).at[bucket].add(1)
    padded = (counts + tile - 1) // tile * tile
    ends = jnp.cumsum(padded)
    starts = ends - padded
    dest = starts[bucket] + rank
    src = jnp.zeros((max_tiles * tile,), I32).at[dest].set(jnp.arange(s, dtype=I32))
    w_sorted = jnp.zeros((max_tiles * tile, ROUTE_COLS), F32).at[dest].set(route[:, :ROUTE_COLS])
    n_tiles = ends[-1] // tile
    tile_start = jnp.arange(max_tiles, dtype=I32) * tile
    first_row = jnp.minimum(tile_start, ends[-1] - tile)
    tile_bucket = jnp.sum((ends[None, :] <= first_row[:, None]).astype(I32), axis=1)
    nvalid = jnp.where(tile_start < ends[-1],
                       jnp.clip(counts[tile_bucket] - (tile_start - starts[tile_bucket]), 0, tile), 0).astype(I32)
    pair_lo = jnp.array([0, 0, 0, 1, 1, 2], I32)
    pair_hi = jnp.array([1, 2, 3, 2, 3, 3], I32)
    group = tile_bucket // PAIRS_PER_GROUP
    e_lo = group * EXPERTS_PER_GROUP + pair_lo[tile_bucket % PAIRS_PER_GROUP]
    e_hi = group * EXPERTS_PER_GROUP + pair_hi[tile_bucket % PAIRS_PER_GROUP]

    lo_w = lambda shape: pl.BlockSpec(shape, lambda j, src, elo, ehi, nv, nt: (elo[j], 0, 0))
    hi_w = lambda shape: pl.BlockSpec(shape, lambda j, src, elo, ehi, nv, nt: (ehi[j], 0, 0))
    grid_spec = pltpu.PrefetchScalarGridSpec(
        num_scalar_prefetch=5,
        grid=(max_tiles,),
        in_specs=[pl.BlockSpec(memory_space=pl.ANY),
                  pl.BlockSpec((tile, ROUTE_COLS), lambda j, *_: (j, 0)),
                  pl.BlockSpec((1, d), lambda j, *_: (0, 0)),
                  lo_w((1, d, ff)), lo_w((1, d, ff)), lo_w((1, ff, d)),
                  hi_w((1, d, ff)), hi_w((1, d, ff)), hi_w((1, ff, d))],
        out_specs=pl.BlockSpec(memory_space=pl.ANY),
        scratch_shapes=[pltpu.VMEM((2, tile, d), F32), pltpu.VMEM((2, tile, d), F32),
                        pltpu.SemaphoreType.DMA((2,)), pltpu.SemaphoreType.DMA((2,))],
    )
    return pl.pallas_call(
        functools.partial(_moe_kernel, tile=tile),
        grid_spec=grid_spec,
        out_shape=jax.ShapeDtypeStruct((s, d), F32),
        compiler_params=_cparams(("arbitrary",)),
        name="moe_experts",
    )(src, e_lo, e_hi, nvalid, n_tiles.reshape(1), x1, w_sorted, gf, wg, wu, wd, wg, wu, wd)


def _tail_weight(w_in):
    d = w_in.shape[0]
    sizes = (IDX_HEADS * IDX_DIM, IDX_DIM, IDX_HEADS, XATTN_W)
    offs = [MAIN_W]
    for sz in sizes:
        offs.append(offs[-1] + sz)
    w_qi, w_ki, w_wi, w_xq = (w_in[:, offs[n]:offs[n + 1]] for n in range(len(sizes)))
    pad = lambda w, to: jnp.pad(w, ((0, 0), (0, to - w.shape[1])))
    w_qi = jnp.pad(w_qi.reshape(d, IDX_HEADS, IDX_DIM), ((0, 0), (0, 0), (0, LANES - IDX_DIM)))
    return jnp.concatenate([w_xq, w_qi.reshape(d, IDX_HEADS * LANES), pad(w_ki, LANES), pad(w_wi, LANES)],
                           axis=1).astype(BF16)


def _rope_inv_freq():
    def inv(dim):
        return ROPE_THETA ** (-jnp.arange(0, dim, 2, dtype=F32) / dim)
    return jnp.stack([jnp.tile(inv(HEAD_DIM), LANES // (HEAD_DIM // 2)),
                      jnp.tile(inv(IDX_DIM), LANES // (IDX_DIM // 2))])


def _layer(x, mem, pos, g_mix, w_in, b_gate, w_pool_grp, pool_scale, q_norm_g, k_norm_g, g_mem, w_mem_kv,
           xq_norm_g, xk_norm_g, w_pool_out, w_attn_out, w_cross_out, w_o, g_ffn, w_router_group,
           w_router_expert, w_e_gate, w_e_up, w_e_down):
    s, d = x.shape
    row2 = lambda v: v.reshape(1, -1)
    k_sel = min(TOPK_MAX, s // 4)

    main, h = _main_proj(x, row2(g_mix), w_in, tm=1024)
    tail = _tail_proj(h, _tail_weight(w_in), tm=1024)
    n_gate_cols = -(-(GATE_SHIFT + N_BRANCH * d) // GATE_TN) * GATE_TN
    bias_window = jnp.pad(row2(b_gate), ((0, 0), (GATE_SHIFT, n_gate_cols - GATE_SHIFT - N_BRANCH * d)))
    gates = _gate_proj(h, w_in, bias_window, tm=1024)

    q, k, v, xq, qi, ki, wib = _prep(main, tail, pos.reshape(s, 1), _rope_inv_freq(), row2(q_norm_g),
                                     row2(k_norm_g), row2(xq_norm_g), tm=512)

    bd = jax.scipy.linalg.block_diag(*[w_pool_grp[g] for g in range(len(POOL_WINDOWS))])
    pool_pre = _pool(main, bd.astype(BF16), row2(pool_scale), tm=512)

    k_m, v_m = _memkv(mem, row2(g_mem), w_mem_kv.astype(BF16), row2(xk_norm_g))
    cross = _cross(xq, k_m, v_m, tm=512)

    attn = _dsa(qi, wib, q, ki, k, v, tq=256, kc=512, k_sel=k_sel)

    w_router = jnp.pad(jnp.concatenate([w_router_group, w_router_expert], axis=1),
                       ((0, 0), (0, LANES - N_GROUPS - N_EXPERTS)))
    x1, route = _merge(x, pool_pre, attn, cross, gates, w_pool_out.astype(BF16), w_attn_out.astype(BF16),
                       w_cross_out.astype(BF16), w_o.astype(BF16), row2(g_ffn), w_router.astype(BF16), tm=256)
    return _moe(x1, route, row2(g_ffn), w_e_gate.astype(BF16), w_e_up.astype(BF16), w_e_down.astype(BF16),
                tile=256)


def kernel(x, mem, positions, g_mix, w_in, b_gate, w_pool_grp, pool_scale, q_norm_g, k_norm_g, g_mem, w_mem_kv,
           xq_norm_g, xk_norm_g, w_pool_out, w_attn_out, w_cross_out, w_o, g_ffn, w_router_group,
           w_router_expert, w_e_gate, w_e_up, w_e_down):
    depth = g_mix.shape[0]
    outs = []
    for b in range(x.shape[0]):
        xb = x[b]
        for l in range(depth):
            xb = _layer(xb, mem[b], positions[b], g_mix[l], w_in[l], b_gate[l], w_pool_grp[l], pool_scale[l],
                        q_norm_g[l], k_norm_g[l], g_mem[l], w_mem_kv[l], xq_norm_g[l], xk_norm_g[l],
                        w_pool_out[l], w_attn_out[l], w_cross_out[l], w_o[l], g_ffn[l], w_router_group[l],
                        w_router_expert[l], w_e_gate[l], w_e_up[l], w_e_down[l])
        outs.append(xb)
    return jnp.stack(outs)
```

```python
import functools
import math

import jax
import jax.numpy as jnp
from jax import lax
from jax.experimental import pallas as pl
from jax.experimental.pallas import tpu as pltpu

F32 = jnp.float32
BF16 = jnp.bfloat16
I32 = jnp.int32
I16 = jnp.int16
I16_MIN = -32768

NORM_EPS = 1e-6
ROPE_THETA = 10000.0
HEAD_DIM = 128
ATTN_HEADS = 6
ATTN_W = ATTN_HEADS * HEAD_DIM
IDX_HEADS = 4
IDX_DIM = 64
XATTN_HEADS = 4
XATTN_W = XATTN_HEADS * HEAD_DIM
POOL_WINDOWS = (2, 4, 8, 16)
POOL_GROUP_W = 192
POOL_W = len(POOL_WINDOWS) * POOL_GROUP_W
POOL_HALO = 16
TOPK_MAX = 256
N_GROUPS = 4
EXPERTS_PER_GROUP = 4
N_EXPERTS = N_GROUPS * EXPERTS_PER_GROUP
N_BRANCH = 3
PAIRS_PER_GROUP = EXPERTS_PER_GROUP * (EXPERTS_PER_GROUP - 1) // 2
ROUTE_BUCKET, ROUTE_RANK, ROUTE_W_LO, ROUTE_W_HI, ROUTE_COLS = 0, 1, 2, 3, 4

LANES = 128
SUBLANES = 8
V7X_VMEM_LIMIT_BYTES = 56 * 1024 * 1024

COL_POOL = 0
COL_Q = COL_POOL + POOL_W
COL_K = COL_Q + ATTN_W
COL_V = COL_K + ATTN_W
COL_QI = COL_V + ATTN_W
COL_KI = COL_QI + IDX_HEADS * IDX_DIM
XQ_SHIFT = IDX_DIM + IDX_HEADS
GATE_COL0 = COL_KI + XQ_SHIFT + XATTN_W
MAIN_TN = 512
MAIN_W = -(-GATE_COL0 // MAIN_TN) * MAIN_TN
GATE_TN = 1280
GATE_WINDOW0 = GATE_COL0 // GATE_TN * GATE_TN
GATE_SHIFT = GATE_COL0 - GATE_WINDOW0
assert COL_QI % LANES == 0 and COL_KI % LANES == 0 and GATE_SHIFT < LANES
IDX_PITCH = IDX_HEADS * LANES + LANES

NEG_BIG = -1e30
INT_MIN = -2147483648
KEY_NEG_INF = -2139095041
SOFTMAX_SUM_MIN = 2.0 ** -64
SOFTMAX_SUM_MAX = 2.0 ** 100
NT_DIMS = (((1,), (1,)), ((), ()))


def _cparams(semantics):
    return pltpu.CompilerParams(dimension_semantics=semantics, vmem_limit_bytes=V7X_VMEM_LIMIT_BYTES)


def _resident(shape):
    return pl.BlockSpec(shape, lambda *_: (0,) * len(shape), pipeline_mode=pl.Buffered(1))


def _rms_rows(x, g):
    ms = jnp.mean(x * x, axis=-1, keepdims=True)
    return x * lax.rsqrt(ms + NORM_EPS) * g


def _sigmoid(x):
    return 1.0 / (1.0 + jnp.exp(-x))


def _main_proj_kernel(x_ref, g_ref, w_ref, o_ref, h_ref):
    @pl.when(pl.program_id(1) == 0)
    def _():
        h_ref[...] = _rms_rows(x_ref[...], g_ref[...]).astype(BF16)

    o_ref[...] = lax.dot_general(h_ref[...], w_ref[...].astype(BF16), NT_DIMS, preferred_element_type=F32)


def _main_proj(x, g, w_in_all, layer, *, tm):
    s, d = x.shape
    return pl.pallas_call(
        _main_proj_kernel,
        grid=(s // tm, MAIN_W // MAIN_TN),
        in_specs=[pl.BlockSpec((tm, d), lambda i, j: (i, 0)),
                  pl.BlockSpec((1, d), lambda i, j: (0, 0)),
                  pl.BlockSpec((None, MAIN_TN, d), lambda i, j: (layer, j, 0))],
        out_specs=[pl.BlockSpec((tm, MAIN_TN), lambda i, j: (i, j)),
                   pl.BlockSpec((tm, d), lambda i, j: (i, 0))],
        out_shape=[jax.ShapeDtypeStruct((s, MAIN_W), F32), jax.ShapeDtypeStruct((s, d), BF16)],
        compiler_params=_cparams(("parallel", "arbitrary")),
        name="norm_proj_main",
    )(x, g, w_in_all)


def _gate_proj_kernel(h_ref, w_ref, b_ref, o_ref, wb_ref, *, n_cols):
    @pl.when(pl.program_id(1) == 0)
    def _():
        col = GATE_WINDOW0 + pl.program_id(0) * GATE_TN + lax.broadcasted_iota(I32, (GATE_TN, 1), 0)
        wb_ref[...] = jnp.where(col < n_cols, w_ref[...], 0.0).astype(BF16)

    acc = lax.dot_general(h_ref[...], wb_ref[...], NT_DIMS, preferred_element_type=F32)
    o_ref[...] = _sigmoid(acc + b_ref[...]).astype(BF16)


def _gate_proj(h, w_in_all, layer, bias_window, *, tm):
    s, d = h.shape
    n = bias_window.shape[1]
    first = GATE_WINDOW0 // GATE_TN
    return pl.pallas_call(
        functools.partial(_gate_proj_kernel, n_cols=w_in_all.shape[1]),
        grid=(n // GATE_TN, s // tm),
        in_specs=[pl.BlockSpec((tm, d), lambda j, i: (i, 0)),
                  pl.BlockSpec((None, GATE_TN, d), lambda j, i: (layer, first + j, 0)),
                  pl.BlockSpec((1, GATE_TN), lambda j, i: (0, j))],
        out_specs=pl.BlockSpec((tm, GATE_TN), lambda j, i: (i, j)),
        out_shape=jax.ShapeDtypeStruct((s, n), BF16),
        scratch_shapes=[pltpu.VMEM((GATE_TN, d), BF16)],
        compiler_params=_cparams(("parallel", "arbitrary")),
        name="gate_proj",
    )(h, w_in_all, bias_window)


def _prep_kernel(f_ref, pos_ref, invf_ref, qg_ref, kg_ref, xg_ref,
                 q_ref, k_ref, v_ref, xq_ref, qi_ref, ki_ref, wib_ref):
    tm = f_ref.shape[0]
    pos = pos_ref[...].astype(F32)
    lane = lax.broadcasted_iota(I32, (1, LANES), 1)
    ang = pos * invf_ref[0:1, :]
    cos_a = jnp.cos(ang)
    sin_a = jnp.where(lane < HEAD_DIM // 2, -jnp.sin(ang), jnp.sin(ang))
    ang_i = pos * invf_ref[1:2, :]
    low_i = (lane & (IDX_DIM - 1)) < IDX_DIM // 2
    cos_i = jnp.cos(ang_i)
    sin_i = jnp.where(low_i, -jnp.sin(ang_i), jnp.sin(ang_i))

    def rope_head(y):
        return y * cos_a + pltpu.roll(y, HEAD_DIM // 2, 1) * sin_a

    def rope_idx(y):
        partner = jnp.where(low_i, pltpu.roll(y, LANES - IDX_DIM // 2, 1), pltpu.roll(y, IDX_DIM // 2, 1))
        return y * cos_i + partner * sin_i

    for h in range(ATTN_HEADS):
        sl = slice(h * HEAD_DIM, (h + 1) * HEAD_DIM)
        qh = f_ref[:, COL_Q + h * HEAD_DIM:COL_Q + (h + 1) * HEAD_DIM]
        kh = f_ref[:, COL_K + h * HEAD_DIM:COL_K + (h + 1) * HEAD_DIM]
        q_ref[:, sl] = rope_head(_rms_rows(qh, qg_ref[...])).astype(BF16)
        k_ref[:, sl] = rope_head(_rms_rows(kh, kg_ref[...])).astype(BF16)
    v_ref[...] = f_ref[:, COL_V:COL_V + ATTN_W].astype(BF16)
    xq_all = f_ref[:, COL_KI:COL_KI + XATTN_W + LANES][:, XQ_SHIFT:XQ_SHIFT + XATTN_W]
    for h in range(XATTN_HEADS):
        sl = slice(h * HEAD_DIM, (h + 1) * HEAD_DIM)
        xq_ref[:, sl] = _rms_rows(xq_all[:, sl], xg_ref[...]).astype(BF16)
    first_half = lane < IDX_DIM
    for pair in range(IDX_HEADS // 2):
        both = rope_idx(f_ref[:, COL_QI + pair * LANES:COL_QI + (pair + 1) * LANES])
        qi_ref[:, (2 * pair) * LANES:(2 * pair + 1) * LANES] = jnp.where(first_half, both, 0.0).astype(BF16)
        qi_ref[:, (2 * pair + 1) * LANES:(2 * pair + 2) * LANES] = jnp.where(first_half, 0.0, both).astype(BF16)
    ki_tile = f_ref[:, COL_KI:COL_KI + LANES]
    ki = jnp.where(first_half, rope_idx(ki_tile), 0.0)
    ki_ref[...] = (ki + pltpu.roll(ki, IDX_DIM, 1)).astype(BF16)
    wi = ki_tile * (IDX_HEADS ** -0.5) * (IDX_DIM ** -0.5)
    for h in range(IDX_HEADS):
        wib_ref[:, h * LANES:(h + 1) * LANES] = jnp.broadcast_to(wi[:, IDX_DIM + h:IDX_DIM + h + 1], (tm, LANES))
    qi_ref[:, IDX_HEADS * LANES:] = jnp.zeros((tm, LANES), BF16)
    wib_ref[:, IDX_HEADS * LANES:] = jnp.zeros((tm, LANES), F32)


def _prep(main, pos_col, invf, qg, kg, xg, *, tm):
    s = main.shape[0]
    row = lambda w: pl.BlockSpec((tm, w), lambda i: (i, 0))
    small = lambda shape: pl.BlockSpec(shape, lambda i: (0, 0))
    outs = [(row(ATTN_W), (s, ATTN_W), BF16), (row(ATTN_W), (s, ATTN_W), BF16), (row(ATTN_W), (s, ATTN_W), BF16),
            (row(XATTN_W), (s, XATTN_W), BF16), (row(IDX_PITCH), (s, IDX_PITCH), BF16),
            (row(LANES), (s, LANES), BF16), (row(IDX_PITCH), (s, IDX_PITCH), F32)]
    return pl.pallas_call(
        _prep_kernel,
        grid=(s // tm,),
        in_specs=[row(MAIN_W), row(1), small((2, LANES)), small((1, LANES)), small((1, LANES)),
                  small((1, LANES))],
        out_specs=[spec for spec, _, _ in outs],
        out_shape=[jax.ShapeDtypeStruct(shape, dt) for _, shape, dt in outs],
        compiler_params=_cparams(("parallel",)),
        name="qknorm_rope",
    )(main, pos_col, invf, qg, kg, xg)


def _pool_kernel(u_ref, halo_ref, bd_ref, sc_ref, o_ref, buf_ref):
    i = pl.program_id(0)
    tm = u_ref.shape[0]
    u = u_ref[...]
    buf_ref[0:POOL_HALO, :] = jnp.where(i == 0, 0.0, halo_ref[...])
    buf_ref[POOL_HALO:POOL_HALO + tm, :] = u
    t = i * tm + lax.broadcasted_iota(I32, (tm, 1), 0)
    lane = lax.broadcasted_iota(I32, (1, POOL_W), 1)
    acc = u
    pooled = None
    for d in range(1, max(POOL_WINDOWS)):
        acc = acc + buf_ref[POOL_HALO - d:POOL_HALO - d + tm, :]
        if d + 1 in POOL_WINDOWS:
            g = POOL_WINDOWS.index(d + 1)
            mean = acc / jnp.minimum(t + 1, d + 1).astype(F32)
            in_group = (lane >= g * POOL_GROUP_W) & (lane < (g + 1) * POOL_GROUP_W)
            pooled = jnp.where(in_group, mean, 0.0 if pooled is None else pooled)
    p = (pooled - u).astype(BF16)
    y = jnp.dot(p, bd_ref[...], preferred_element_type=F32) * sc_ref[...]
    o_ref[...] = y.astype(BF16)


def _pool(front, bd, scale, *, tm):
    s = front.shape[0]
    halo_blocks = tm // POOL_HALO
    return pl.pallas_call(
        _pool_kernel,
        grid=(s // tm,),
        in_specs=[
            pl.BlockSpec((tm, POOL_W), lambda i: (i, 0)),
            pl.BlockSpec((POOL_HALO, POOL_W), lambda i: (jnp.maximum(i * halo_blocks - 1, 0), 0)),
            pl.BlockSpec((POOL_W, POOL_W), lambda i: (0, 0)),
            pl.BlockSpec((1, POOL_W), lambda i: (0, 0)),
        ],
        out_specs=pl.BlockSpec((tm, POOL_W), lambda i: (i, 0)),
        out_shape=jax.ShapeDtypeStruct((s, POOL_W), BF16),
        scratch_shapes=[pltpu.VMEM((tm + POOL_HALO, POOL_W), F32)],
        compiler_params=_cparams(("parallel",)),
        name="pool_mixer",
    )(front, front, bd, scale)


def _memkv_kernel(mem_ref, g_ref, w_ref, kg_ref, k_ref, v_ref):
    h = _rms_rows(mem_ref[...], g_ref[...]).astype(BF16)
    kv = jnp.dot(h, w_ref[...], preferred_element_type=F32)
    for hd in range(XATTN_HEADS):
        sl = slice(hd * HEAD_DIM, (hd + 1) * HEAD_DIM)
        k_ref[:, sl] = _rms_rows(kv[:, sl], kg_ref[...]).astype(BF16)
    v_ref[...] = kv[:, XATTN_W:].astype(BF16)


def _memkv(mem, g, w, kg):
    m = mem.shape[0]
    return pl.pallas_call(
        _memkv_kernel,
        out_shape=[jax.ShapeDtypeStruct((m, XATTN_W), BF16), jax.ShapeDtypeStruct((m, XATTN_W), BF16)],
        compiler_params=pltpu.CompilerParams(vmem_limit_bytes=V7X_VMEM_LIMIT_BYTES),
        name="mem_kv",
    )(mem, g, w, kg)


def _cross_kernel(xq_ref, k_ref, v_ref, o_ref):
    for h in range(XATTN_HEADS):
        sl = slice(h * HEAD_DIM, (h + 1) * HEAD_DIM)
        logits = lax.dot_general(xq_ref[:, sl], k_ref[:, sl], NT_DIMS,
                                 preferred_element_type=F32) * (HEAD_DIM ** -0.5)
        e = jnp.exp(logits - jnp.max(logits, axis=-1, keepdims=True))
        p = e / jnp.sum(e, axis=-1, keepdims=True)
        o_ref[:, sl] = jnp.dot(p.astype(BF16), v_ref[:, sl], preferred_element_type=F32).astype(BF16)


def _cross(xq, k_m, v_m, *, tm):
    s = xq.shape[0]
    m = k_m.shape[0]
    return pl.pallas_call(
        _cross_kernel,
        grid=(s // tm,),
        in_specs=[pl.BlockSpec((tm, XATTN_W), lambda i: (i, 0)),
                  pl.BlockSpec((m, XATTN_W), lambda i: (0, 0)),
                  pl.BlockSpec((m, XATTN_W), lambda i: (0, 0))],
        out_specs=pl.BlockSpec((tm, XATTN_W), lambda i: (i, 0)),
        out_shape=jax.ShapeDtypeStruct((s, XATTN_W), BF16),
        compiler_params=_cparams(("parallel",)),
        name="mem_cross_attn",
    )(xq, k_m, v_m)


def _key_to_f32(key):
    bits = key ^ ((key >> 31) & 0x7FFFFFFF)
    return lax.bitcast_convert_type(bits, F32)


def _dsa_kernel(qi_ref, wib_ref, q_ref, ki_ref, k_ref, v_ref, tri_ref, o_ref, sc_ref, hi_ref, lo_ref,
                *, tq, kc, k_sel):
    i = pl.program_id(0)
    n_chunks = ((i + 1) * tq + kc - 1) // kc
    t = i * tq + lax.broadcasted_iota(I32, (tq, 1), 0)
    slabs = kc // LANES

    def slab(c, j):
        return pl.ds(pl.multiple_of(c * kc, kc) + j * LANES, LANES)

    def score_chunk(c, carry):
        off = pl.multiple_of(c * kc, kc)
        ki_c = ki_ref[pl.ds(off, kc), :]
        dots = [lax.dot_general(qi_ref[:, h * LANES:(h + 1) * LANES], ki_c, NT_DIMS,
                                preferred_element_type=F32) for h in range(IDX_HEADS)]
        for j in range(slabs):
            s = jnp.zeros((tq, LANES), F32)
            for h in range(IDX_HEADS):
                s = s + jnp.maximum(dots[h][:, j * LANES:(j + 1) * LANES], 0.0) * wib_ref[:, h * LANES:(h + 1) * LANES]
            kpos = off + j * LANES + lax.broadcasted_iota(I32, (1, LANES), 1)
            s = jnp.where(kpos <= t, jnp.where(s == 0.0, 0.0, s), -jnp.inf)
            sc_ref[:, slab(c, j)] = s
            bits = pltpu.bitcast(s, I32)
            key = bits ^ ((bits >> 31) & 0x7FFFFFFF)
            hi_ref[:, slab(c, j)] = (key >> 16).astype(I16)
            lo_ref[:, slab(c, j)] = ((key & 0xFFFF) + I16_MIN).astype(I16)
        return carry

    lax.fori_loop(0, n_chunks, score_chunk, 0)

    def count16(ref, cand16):
        def body(c, acc):
            for j in range(slabs):
                acc = acc + jnp.where(ref[:, slab(c, j)] >= cand16, jnp.int16(1), jnp.int16(0))
            return acc
        acc = lax.fori_loop(0, n_chunks, body, jnp.zeros((tq, LANES), I16))
        return jnp.sum(acc.astype(I32), axis=-1, keepdims=True)

    def search16(ref, counted_already):
        def step(b, best):
            cand = best + jnp.left_shift(jnp.int32(1), 15 - b)
            enough = count16(ref, cand.astype(I16)) + counted_already >= k_sel
            return jnp.where(enough, cand, best)
        return lax.fori_loop(0, 16, step, jnp.full((tq, LANES), I16_MIN, I32))

    hi_best = search16(hi_ref, 0)
    hi_best16 = hi_best.astype(I16)

    def split_chunk(c, acc):
        for j in range(slabs):
            h = hi_ref[:, slab(c, j)]
            lo_ref[:, slab(c, j)] = jnp.where(h == hi_best16, lo_ref[:, slab(c, j)], jnp.int16(I16_MIN))
            acc = acc + jnp.where(h > hi_best16, jnp.int16(1), jnp.int16(0))
        return acc

    above = lax.fori_loop(0, n_chunks, split_chunk, jnp.zeros((tq, LANES), I16))
    lo_best = search16(lo_ref, jnp.sum(above.astype(I32), axis=-1, keepdims=True))
    thr_key = jnp.left_shift(hi_best, 16) + (lo_best - I16_MIN)

    def count(pred):
        def body(c, acc):
            for j in range(slabs):
                acc = acc + pred(sc_ref[:, slab(c, j)]).astype(I32)
            return acc
        acc = lax.fori_loop(0, n_chunks, body, jnp.zeros((tq, LANES), I32))
        return jnp.sum(acc, axis=-1, keepdims=True)

    thr_found = _key_to_f32(thr_key)
    thr = jnp.where(count(lambda s: s >= thr_found) >= k_sel, thr_found, _key_to_f32(thr_key - 1))
    need = (k_sel - count(lambda s: s > thr)).astype(F32)
    thr_col = thr[:, 0:1]

    def mask_chunk(c, ties_before):
        off = pl.multiple_of(c * kc, kc)
        s = sc_ref[:, pl.ds(off, kc)]
        tie = s == thr_col
        prefix = jnp.dot(jnp.where(tie, 1.0, 0.0).astype(BF16), tri_ref[:, 0:kc], preferred_element_type=F32)
        kpos = off + lax.broadcasted_iota(I32, (1, kc), 1)
        take_tie = tie & (prefix + ties_before <= need)
        sel = ((s > thr_col) | take_tie) & (kpos <= t)
        sc_ref[:, pl.ds(off, kc)] = jnp.where(sel, 0.0, NEG_BIG)
        return ties_before + prefix[:, kc - 1:kc]

    lax.fori_loop(0, n_chunks, mask_chunk, jnp.zeros((tq, 1), F32))

    scale2 = (HEAD_DIM ** -0.5) * math.log2(math.e)
    heads = [slice(h * HEAD_DIM, (h + 1) * HEAD_DIM) for h in range(ATTN_HEADS)]

    def unshifted_chunk(c, carry):
        off = pl.multiple_of(c * kc, kc)
        bias = sc_ref[:, pl.ds(off, kc)]
        out = []
        for sl, (lane_sum, acc) in zip(heads, carry):
            p = jnp.exp2(lax.dot_general(q_ref[:, sl], k_ref[pl.ds(off, kc), sl], NT_DIMS,
                                         preferred_element_type=F32) * scale2 + bias)
            for j in range(slabs):
                lane_sum = lane_sum + p[:, j * LANES:(j + 1) * LANES]
            acc = acc + jnp.dot(p.astype(BF16), v_ref[pl.ds(off, kc), sl], preferred_element_type=F32)
            out.append((lane_sum, acc))
        return tuple(out)

    zeros = jnp.zeros((tq, LANES), F32)
    result = lax.fori_loop(0, n_chunks, unshifted_chunk, tuple((zeros, zeros) for _ in heads))
    in_range = None
    for sl, (lane_sum, acc) in zip(heads, result):
        row_sum = jnp.sum(lane_sum, axis=-1, keepdims=True)
        o_ref[:, sl] = (acc / row_sum).astype(BF16)
        ok = (row_sum >= SOFTMAX_SUM_MIN) & (row_sum <= SOFTMAX_SUM_MAX)
        in_range = ok if in_range is None else (in_range & ok)
    n_bad = jnp.sum(jnp.where(in_range, 0.0, 1.0))

    @pl.when(n_bad > 0.0)
    def _():
        _shifted_attention(q_ref, k_ref, v_ref, sc_ref, o_ref, n_chunks, tq=tq, kc=kc, scale2=scale2)


def _shifted_attention(q_ref, k_ref, v_ref, sc_ref, o_ref, n_chunks, *, tq, kc, scale2):
    slabs = kc // LANES
    for h in range(ATTN_HEADS):
        sl = slice(h * HEAD_DIM, (h + 1) * HEAD_DIM)

        def logits(c, sl=sl):
            off = pl.multiple_of(c * kc, kc)
            return lax.dot_general(q_ref[:, sl], k_ref[pl.ds(off, kc), sl], NT_DIMS,
                                   preferred_element_type=F32) * scale2 + sc_ref[:, pl.ds(off, kc)]

        def max_chunk(c, lane_max, logits=logits):
            s = logits(c)
            for j in range(slabs):
                lane_max = jnp.maximum(lane_max, s[:, j * LANES:(j + 1) * LANES])
            return lane_max

        lane_max = lax.fori_loop(0, n_chunks, max_chunk, jnp.full((tq, LANES), NEG_BIG, F32))
        row_max = jnp.max(lane_max, axis=-1, keepdims=True)

        def pv_chunk(c, carry, sl=sl, row_max=row_max, logits=logits):
            lane_sum, acc = carry
            off = pl.multiple_of(c * kc, kc)
            p = jnp.exp2(logits(c) - row_max)
            for j in range(slabs):
                lane_sum = lane_sum + p[:, j * LANES:(j + 1) * LANES]
            acc = acc + jnp.dot(p.astype(BF16), v_ref[pl.ds(off, kc), sl], preferred_element_type=F32)
            return lane_sum, acc

        lane_sum, acc = lax.fori_loop(0, n_chunks, pv_chunk,
                                      (jnp.zeros((tq, LANES), F32), jnp.zeros((tq, HEAD_DIM), F32)))
        o_ref[:, sl] = (acc / jnp.sum(lane_sum, axis=-1, keepdims=True)).astype(BF16)


def _dsa(qi, wib, q, ki, k, v, *, tq, kc, k_sel):
    s = q.shape[0]
    tri = jnp.pad(jnp.triu(jnp.ones((kc, kc), BF16)), ((0, 0), (0, LANES)))
    row = lambda w: pl.BlockSpec((tq, w), lambda i: (i, 0))
    return pl.pallas_call(
        functools.partial(_dsa_kernel, tq=tq, kc=kc, k_sel=k_sel),
        grid=(s // tq,),
        in_specs=[row(IDX_PITCH), row(IDX_PITCH), row(ATTN_W),
                  _resident((s, LANES)), _resident((s, ATTN_W)), _resident((s, ATTN_W)),
                  _resident(tri.shape)],
        out_specs=row(ATTN_W),
        out_shape=jax.ShapeDtypeStruct((s, ATTN_W), BF16),
        scratch_shapes=[pltpu.VMEM((tq, s + LANES), F32), pltpu.VMEM((tq, s + LANES), I16),
                        pltpu.VMEM((tq, s + LANES), I16)],
        compiler_params=_cparams(("parallel",)),
        name="dsa_attention",
    )(qi, wib, q, ki, k, v, tri)


def _route(logits):
    lane = lax.broadcasted_iota(I32, (1, LANES), 1).astype(F32)
    far = float(LANES)
    is_group = lane < N_GROUPS
    g_max = jnp.max(jnp.where(is_group, logits, -jnp.inf), axis=-1, keepdims=True)
    g_top = jnp.min(jnp.where(is_group & (logits == g_max), lane, far), axis=-1, keepdims=True)
    pg_top = 1.0 / jnp.sum(jnp.where(is_group, jnp.exp(logits - g_max), 0.0), axis=-1, keepdims=True)
    first = N_GROUPS + EXPERTS_PER_GROUP * g_top
    in_grp = (lane >= first) & (lane < first + EXPERTS_PER_GROUP)
    e_max = jnp.max(jnp.where(in_grp, logits, -jnp.inf), axis=-1, keepdims=True)
    e_exp = jnp.where(in_grp, jnp.exp(logits - e_max), 0.0)
    pe = jnp.where(in_grp, e_exp / jnp.sum(e_exp, axis=-1, keepdims=True), -1.0)
    p1 = jnp.max(pe, axis=-1, keepdims=True)
    e1 = jnp.min(jnp.where(pe == p1, lane, far), axis=-1, keepdims=True)
    pe_rest = jnp.where(lane == e1, -1.0, pe)
    p2 = jnp.max(pe_rest, axis=-1, keepdims=True)
    e2 = jnp.min(jnp.where(pe_rest == p2, lane, far), axis=-1, keepdims=True)
    den = p1 + p2
    w1 = pg_top * p1 / den
    w2 = pg_top * p2 / den
    lo = jnp.minimum(e1, e2) - first
    hi = jnp.maximum(e1, e2) - first
    pair = lo * (2 * EXPERTS_PER_GROUP - 1 - lo) * 0.5 + (hi - lo - 1.0)
    first_is_lo = e1 < e2
    return (g_top * PAIRS_PER_GROUP + pair, jnp.where(first_is_lo, w1, w2), jnp.where(first_is_lo, w2, w1))


def _merge_kernel(x_ref, p_ref, a_ref, c_ref, gt_ref, wpo_ref, wao_ref, wco_ref, wo_ref, gf_ref, wr_ref,
                  tri_ref, x1_ref, route_ref, cnt_ref):
    @pl.when(pl.program_id(0) == 0)
    def _():
        cnt_ref[...] = jnp.zeros_like(cnt_ref)

    d = x_ref.shape[1]
    def gate(b):
        window = gt_ref[:, b * d:(b + 1) * d + LANES].astype(F32)
        return window[:, GATE_SHIFT:GATE_SHIFT + d]

    merged = gate(0) * jnp.dot(p_ref[...], wpo_ref[...], preferred_element_type=F32)
    merged += gate(1) * jnp.dot(a_ref[...], wao_ref[...], preferred_element_type=F32)
    merged += gate(2) * jnp.dot(c_ref[...], wco_ref[...], preferred_element_type=F32)
    x1 = x_ref[...] + jnp.dot(merged.astype(BF16), wo_ref[...], preferred_element_type=F32)
    x1_ref[...] = x1
    h2 = _rms_rows(x1, gf_ref[...]).astype(BF16)
    bucket, w_lo, w_hi = _route(jnp.dot(h2, wr_ref[...], preferred_element_type=F32))
    lane = lax.broadcasted_iota(I32, (1, LANES), 1).astype(F32)
    onehot = lane == bucket
    before = jnp.dot(tri_ref[...], jnp.where(onehot, 1.0, 0.0).astype(BF16), preferred_element_type=F32)
    rank = jnp.sum(jnp.where(onehot, before + cnt_ref[...], 0.0), axis=-1, keepdims=True)
    cnt_ref[...] += jnp.sum(jnp.where(onehot, 1.0, 0.0), axis=0, keepdims=True)
    route_ref[...] = jnp.where(lane == ROUTE_BUCKET, bucket,
                               jnp.where(lane == ROUTE_RANK, rank,
                                         jnp.where(lane == ROUTE_W_LO, w_lo,
                                                   jnp.where(lane == ROUTE_W_HI, w_hi, 0.0))))


def _merge(x, p, a, c, gates, wpo, wao, wco, wo, gf, wr, *, tm):
    s, d = x.shape
    row = lambda w: pl.BlockSpec((tm, w), lambda i: (i, 0))
    tri = jnp.tril(jnp.ones((tm, tm), BF16), -1)
    return pl.pallas_call(
        _merge_kernel,
        grid=(s // tm,),
        in_specs=[row(d), row(POOL_W), row(ATTN_W), row(XATTN_W), row(gates.shape[1]),
                  _resident(wpo.shape), _resident(wao.shape), _resident(wco.shape), _resident(wo.shape),
                  _resident(gf.shape), _resident(wr.shape), _resident(tri.shape)],
        out_specs=[row(d), row(LANES)],
        out_shape=[jax.ShapeDtypeStruct((s, d), F32), jax.ShapeDtypeStruct((s, LANES), F32)],
        scratch_shapes=[pltpu.VMEM((1, LANES), F32)],
        compiler_params=_cparams(("arbitrary",)),
        name="gated_merge_router",
    )(x, p, a, c, gates, wpo, wao, wco, wo, gf, wr, tri)


def _moe_kernel(src_ref, elo_ref, ehi_ref, nvalid_ref, ntiles_ref,
                x1_hbm, w_ref, gf_ref, wg_lo, wu_lo, wd_lo, wg_hi, wu_hi, wd_hi,
                out_hbm, xbuf, obuf, gsem, ssem, *, tile):
    j = pl.program_id(0)
    n_tiles = ntiles_ref[0]
    slot = lax.rem(j, 2)

    def start_gather(t, s):
        def body(r, carry):
            tok = src_ref[t * tile + r]
            pltpu.make_async_copy(x1_hbm.at[pl.ds(tok, 1)], xbuf.at[s, pl.ds(r, 1)], gsem.at[s]).start()
            return carry
        lax.fori_loop(0, tile, body, 0, unroll=8)

    def wait_scatter(t, s):
        n = nvalid_ref[t]
        n_whole = pl.multiple_of(lax.shift_left(lax.shift_right_logical(n, 3), 3), SUBLANES)

        @pl.when(n_whole > 0)
        def _():
            pltpu.make_async_copy(obuf.at[s, pl.ds(0, n_whole)], out_hbm.at[pl.ds(0, n_whole)], ssem.at[s]).wait()

        def wait_row(r, carry):
            pltpu.make_async_copy(obuf.at[s, pl.ds(0, 1)], out_hbm.at[pl.ds(0, 1)], ssem.at[s]).wait()
            return carry
        lax.fori_loop(0, n - n_whole, wait_row, 0)

    @pl.when(j == 0)
    def _():
        start_gather(0, 0)

    @pl.when(j + 1 < n_tiles)
    def _():
        start_gather(j + 1, 1 - slot)

    @pl.when(j < n_tiles)
    def _():
        pltpu.make_async_copy(x1_hbm.at[pl.ds(0, tile)], xbuf.at[slot], gsem.at[slot]).wait()
        x1 = xbuf[slot]
        h = _rms_rows(x1, gf_ref[...]).astype(BF16)
        y = x1
        for w_col, wg, wu, wd in ((ROUTE_W_LO, wg_lo, wu_lo, wd_lo), (ROUTE_W_HI, wg_hi, wu_hi, wd_hi)):
            a = jnp.dot(h, wg[0], preferred_element_type=F32)
            b = jnp.dot(h, wu[0], preferred_element_type=F32)
            act = (a * _sigmoid(a)) * b * w_ref[:, w_col:w_col + 1]
            y = y + jnp.dot(act.astype(BF16), wd[0], preferred_element_type=F32)
        obuf[slot] = y

        def scatter_row(r, carry):
            tok = src_ref[j * tile + r]
            pltpu.make_async_copy(obuf.at[slot, pl.ds(r, 1)], out_hbm.at[pl.ds(tok, 1)], ssem.at[slot]).start()
            return carry
        lax.fori_loop(0, nvalid_ref[j], scatter_row, 0)

        @pl.when(j >= 1)
        def _():
            wait_scatter(j - 1, 1 - slot)

        @pl.when(j == n_tiles - 1)
        def _():
            wait_scatter(j, slot)


def _moe(x1, route, gf, wg, wu, wd, *, tile):
    s, d = x1.shape
    ff = wg.shape[2]
    n_buckets = N_GROUPS * PAIRS_PER_GROUP
    max_tiles = (s + n_buckets * (tile - 1)) // tile
    bucket = route[:, ROUTE_BUCKET].astype(I32)
    rank = route[:, ROUTE_RANK].astype(I32)
    counts = jnp.zeros((n_buckets,), I32).at[bucket].add(1)
    padded = (counts + tile - 1) // tile * tile
    ends = jnp.cumsum(padded)
    starts = ends - padded
    dest = starts[bucket] + rank
    src = jnp.zeros((max_tiles * tile,), I32).at[dest].set(jnp.arange(s, dtype=I32))
    w_sorted = jnp.zeros((max_tiles * tile, ROUTE_COLS), F32).at[dest].set(route[:, :ROUTE_COLS])
    n_tiles = ends[-1] // tile
    tile_start = jnp.arange(max_tiles, dtype=I32) * tile
    first_row = jnp.minimum(tile_start, ends[-1] - tile)
    tile_bucket = jnp.sum((ends[None, :] <= first_row[:, None]).astype(I32), axis=1)
    nvalid = jnp.where(tile_start < ends[-1],
                       jnp.clip(counts[tile_bucket] - (tile_start - starts[tile_bucket]), 0, tile), 0).astype(I32)
    pair_lo = jnp.array([0, 0, 0, 1, 1, 2], I32)
    pair_hi = jnp.array([1, 2, 3, 2, 3, 3], I32)
    group = tile_bucket // PAIRS_PER_GROUP
    e_lo = group * EXPERTS_PER_GROUP + pair_lo[tile_bucket % PAIRS_PER_GROUP]
    e_hi = group * EXPERTS_PER_GROUP + pair_hi[tile_bucket % PAIRS_PER_GROUP]

    lo_w = lambda shape: pl.BlockSpec(shape, lambda j, src, elo, ehi, nv, nt: (elo[j], 0, 0))
    hi_w = lambda shape: pl.BlockSpec(shape, lambda j, src, elo, ehi, nv, nt: (ehi[j], 0, 0))
    grid_spec = pltpu.PrefetchScalarGridSpec(
        num_scalar_prefetch=5,
        grid=(max_tiles,),
        in_specs=[pl.BlockSpec(memory_space=pl.ANY),
                  pl.BlockSpec((tile, ROUTE_COLS), lambda j, *_: (j, 0)),
                  pl.BlockSpec((1, d), lambda j, *_: (0, 0)),
                  lo_w((1, d, ff)), lo_w((1, d, ff)), lo_w((1, ff, d)),
                  hi_w((1, d, ff)), hi_w((1, d, ff)), hi_w((1, ff, d))],
        out_specs=pl.BlockSpec(memory_space=pl.ANY),
        scratch_shapes=[pltpu.VMEM((2, tile, d), F32), pltpu.VMEM((2, tile, d), F32),
                        pltpu.SemaphoreType.DMA((2,)), pltpu.SemaphoreType.DMA((2,))],
    )
    return pl.pallas_call(
        functools.partial(_moe_kernel, tile=tile),
        grid_spec=grid_spec,
        out_shape=jax.ShapeDtypeStruct((s, d), F32),
        compiler_params=_cparams(("arbitrary",)),
        name="moe_experts",
    )(src, e_lo, e_hi, nvalid, n_tiles.reshape(1), x1, w_sorted, gf, wg, wu, wd, wg, wu, wd)


def _rope_inv_freq():
    def inv(dim):
        return ROPE_THETA ** (-jnp.arange(0, dim, 2, dtype=F32) / dim)
    return jnp.stack([jnp.tile(inv(HEAD_DIM), LANES // (HEAD_DIM // 2)),
                      jnp.tile(inv(IDX_DIM), LANES // (IDX_DIM // 2))])


def _layer(x, mem, pos, g_mix, w_in_all, layer, b_gate, w_pool_grp, pool_scale, q_norm_g, k_norm_g, g_mem, w_mem_kv,
           xq_norm_g, xk_norm_g, w_pool_out, w_attn_out, w_cross_out, w_o, g_ffn, w_router_group,
           w_router_expert, w_e_gate, w_e_up, w_e_down):
    s, d = x.shape
    row2 = lambda v: v.reshape(1, -1)
    k_sel = min(TOPK_MAX, s // 4)

    main, h = _main_proj(x, row2(g_mix), w_in_all, layer, tm=1024)
    n_gate_cols = -(-(GATE_SHIFT + N_BRANCH * d) // GATE_TN) * GATE_TN
    bias_window = jnp.pad(row2(b_gate), ((0, 0), (GATE_SHIFT, n_gate_cols - GATE_SHIFT - N_BRANCH * d)))
    gates = _gate_proj(h, w_in_all, layer, bias_window, tm=1024)

    q, k, v, xq, qi, ki, wib = _prep(main, pos.reshape(s, 1), _rope_inv_freq(), row2(q_norm_g),
                                     row2(k_norm_g), row2(xq_norm_g), tm=512)

    bd = jax.scipy.linalg.block_diag(*[w_pool_grp[g] for g in range(len(POOL_WINDOWS))])
    pool_pre = _pool(main, bd.astype(BF16), row2(pool_scale), tm=512)

    k_m, v_m = _memkv(mem, row2(g_mem), w_mem_kv.astype(BF16), row2(xk_norm_g))
    cross = _cross(xq, k_m, v_m, tm=512)

    attn = _dsa(qi, wib, q, ki, k, v, tq=256, kc=512, k_sel=k_sel)

    w_router = jnp.pad(jnp.concatenate([w_router_group, w_router_expert], axis=1),
                       ((0, 0), (0, LANES - N_GROUPS - N_EXPERTS)))
    x1, route = _merge(x, pool_pre, attn, cross, gates, w_pool_out.astype(BF16), w_attn_out.astype(BF16),
                       w_cross_out.astype(BF16), w_o.astype(BF16), row2(g_ffn), w_router.astype(BF16), tm=256)
    return _moe(x1, route, row2(g_ffn), w_e_gate.astype(BF16), w_e_up.astype(BF16), w_e_down.astype(BF16),
                tile=256)


def kernel(x, mem, positions, g_mix, w_in, b_gate, w_pool_grp, pool_scale, q_norm_g, k_norm_g, g_mem, w_mem_kv,
           xq_norm_g, xk_norm_g, w_pool_out, w_attn_out, w_cross_out, w_o, g_ffn, w_router_group,
           w_router_expert, w_e_gate, w_e_up, w_e_down):
    depth = g_mix.shape[0]
    w_in_t = jnp.swapaxes(w_in, 1, 2)
    outs = []
    for b in range(x.shape[0]):
        xb = x[b]
        for l in range(depth):
            xb = _layer(xb, mem[b], positions[b], g_mix[l], w_in_t, l, b_gate[l], w_pool_grp[l], pool_scale[l],
                        q_norm_g[l], k_norm_g[l], g_mem[l], w_mem_kv[l], xq_norm_g[l], xk_norm_g[l],
                        w_pool_out[l], w_attn_out[l], w_cross_out[l], w_o[l], g_ffn[l], w_router_group[l],
                        w_router_expert[l], w_e_gate[l], w_e_up[l], w_e_down[l])
        outs.append(xb)
    return jnp.stack(outs)
```

```python
import functools
import math

import jax
import jax.numpy as jnp
from jax import lax
from jax.experimental import pallas as pl
from jax.experimental.pallas import tpu as pltpu

F32 = jnp.float32
BF16 = jnp.bfloat16
I32 = jnp.int32
I16 = jnp.int16
I16_MIN = -32768

NORM_EPS = 1e-6
ROPE_THETA = 10000.0
HEAD_DIM = 128
ATTN_HEADS = 6
ATTN_W = ATTN_HEADS * HEAD_DIM
IDX_HEADS = 4
IDX_DIM = 64
XATTN_HEADS = 4
XATTN_W = XATTN_HEADS * HEAD_DIM
POOL_WINDOWS = (2, 4, 8, 16)
POOL_GROUP_W = 192
POOL_W = len(POOL_WINDOWS) * POOL_GROUP_W
POOL_HALO = 16
TOPK_MAX = 256
N_GROUPS = 4
EXPERTS_PER_GROUP = 4
N_EXPERTS = N_GROUPS * EXPERTS_PER_GROUP
N_BRANCH = 3
PAIRS_PER_GROUP = EXPERTS_PER_GROUP * (EXPERTS_PER_GROUP - 1) // 2
ROUTE_BUCKET, ROUTE_RANK, ROUTE_W_LO, ROUTE_W_HI, ROUTE_COLS = 0, 1, 2, 3, 4

LANES = 128
SUBLANES = 8
V7X_VMEM_LIMIT_BYTES = 56 * 1024 * 1024

COL_POOL = 0
COL_Q = COL_POOL + POOL_W
COL_K = COL_Q + ATTN_W
COL_V = COL_K + ATTN_W
COL_QI = COL_V + ATTN_W
COL_KI = COL_QI + IDX_HEADS * IDX_DIM
XQ_SHIFT = IDX_DIM + IDX_HEADS
GATE_COL0 = COL_KI + XQ_SHIFT + XATTN_W
MAIN_TN = 512
MAIN_W = -(-GATE_COL0 // MAIN_TN) * MAIN_TN
GATE_TN = 1280
GATE_WINDOW0 = GATE_COL0 // GATE_TN * GATE_TN
GATE_SHIFT = GATE_COL0 - GATE_WINDOW0
assert COL_QI % LANES == 0 and COL_KI % LANES == 0 and GATE_SHIFT < LANES
IDX_PITCH = IDX_HEADS * LANES + LANES

NEG_BIG = -1e30
INT_MIN = -2147483648
KEY_NEG_INF = -2139095041
SOFTMAX_SUM_MIN = 2.0 ** -64
SOFTMAX_SUM_MAX = 2.0 ** 100
NT_DIMS = (((1,), (1,)), ((), ()))


def _cparams(semantics):
    return pltpu.CompilerParams(dimension_semantics=semantics, vmem_limit_bytes=V7X_VMEM_LIMIT_BYTES)


def _load_once(hbm_refs, vmem_refs, sems):
    @pl.when(pl.program_id(0) == 0)
    def _():
        copies = [pltpu.make_async_copy(src, dst, sems.at[n])
                  for n, (src, dst) in enumerate(zip(hbm_refs, vmem_refs))]
        for c in copies:
            c.start()
        for c in copies:
            c.wait()


def _hbm_spec():
    return pl.BlockSpec(memory_space=pl.ANY)


def _rms_rows(x, g):
    ms = jnp.mean(x * x, axis=-1, keepdims=True)
    return x * lax.rsqrt(ms + NORM_EPS) * g


def _sigmoid(x):
    return 1.0 / (1.0 + jnp.exp(-x))


def _main_proj_kernel(x_ref, g_ref, w_ref, o_ref, h_ref):
    @pl.when(pl.program_id(1) == 0)
    def _():
        h_ref[...] = _rms_rows(x_ref[...], g_ref[...]).astype(BF16)

    o_ref[...] = lax.dot_general(h_ref[...], w_ref[...].astype(BF16), NT_DIMS, preferred_element_type=F32)


def _main_proj(x, g, w_in_all, layer, *, tm):
    s, d = x.shape
    return pl.pallas_call(
        _main_proj_kernel,
        grid=(s // tm, MAIN_W // MAIN_TN),
        in_specs=[pl.BlockSpec((tm, d), lambda i, j: (i, 0)),
                  pl.BlockSpec((1, d), lambda i, j: (0, 0)),
                  pl.BlockSpec((None, MAIN_TN, d), lambda i, j: (layer, j, 0))],
        out_specs=[pl.BlockSpec((tm, MAIN_TN), lambda i, j: (i, j)),
                   pl.BlockSpec((tm, d), lambda i, j: (i, 0))],
        out_shape=[jax.ShapeDtypeStruct((s, MAIN_W), F32), jax.ShapeDtypeStruct((s, d), BF16)],
        compiler_params=_cparams(("parallel", "arbitrary")),
        name="norm_proj_main",
    )(x, g, w_in_all)


def _gate_proj_kernel(h_ref, w_ref, b_ref, o_ref, wb_ref, *, n_cols):
    @pl.when(pl.program_id(1) == 0)
    def _():
        col = GATE_WINDOW0 + pl.program_id(0) * GATE_TN + lax.broadcasted_iota(I32, (GATE_TN, 1), 0)
        wb_ref[...] = jnp.where(col < n_cols, w_ref[...], 0.0).astype(BF16)

    acc = lax.dot_general(h_ref[...], wb_ref[...], NT_DIMS, preferred_element_type=F32)
    o_ref[...] = _sigmoid(acc + b_ref[...]).astype(BF16)


def _gate_proj(h, w_in_all, layer, bias_window, *, tm):
    s, d = h.shape
    n = bias_window.shape[1]
    first = GATE_WINDOW0 // GATE_TN
    return pl.pallas_call(
        functools.partial(_gate_proj_kernel, n_cols=w_in_all.shape[1]),
        grid=(n // GATE_TN, s // tm),
        in_specs=[pl.BlockSpec((tm, d), lambda j, i: (i, 0)),
                  pl.BlockSpec((None, GATE_TN, d), lambda j, i: (layer, first + j, 0)),
                  pl.BlockSpec((1, GATE_TN), lambda j, i: (0, j))],
        out_specs=pl.BlockSpec((tm, GATE_TN), lambda j, i: (i, j)),
        out_shape=jax.ShapeDtypeStruct((s, n), BF16),
        scratch_shapes=[pltpu.VMEM((GATE_TN, d), BF16)],
        compiler_params=_cparams(("parallel", "arbitrary")),
        name="gate_proj",
    )(h, w_in_all, bias_window)


def _prep_kernel(f_ref, pos_ref, invf_ref, qg_ref, kg_ref, xg_ref,
                 q_ref, k_ref, v_ref, xq_ref, qi_ref, ki_ref, wib_ref):
    tm = f_ref.shape[0]
    pos = pos_ref[...].astype(F32)
    lane = lax.broadcasted_iota(I32, (1, LANES), 1)
    ang = pos * invf_ref[0:1, :]
    cos_a = jnp.cos(ang)
    sin_a = jnp.where(lane < HEAD_DIM // 2, -jnp.sin(ang), jnp.sin(ang))
    ang_i = pos * invf_ref[1:2, :]
    low_i = (lane & (IDX_DIM - 1)) < IDX_DIM // 2
    cos_i = jnp.cos(ang_i)
    sin_i = jnp.where(low_i, -jnp.sin(ang_i), jnp.sin(ang_i))

    def rope_head(y):
        return y * cos_a + pltpu.roll(y, HEAD_DIM // 2, 1) * sin_a

    def rope_idx(y):
        partner = jnp.where(low_i, pltpu.roll(y, LANES - IDX_DIM // 2, 1), pltpu.roll(y, IDX_DIM // 2, 1))
        return y * cos_i + partner * sin_i

    for h in range(ATTN_HEADS):
        sl = slice(h * HEAD_DIM, (h + 1) * HEAD_DIM)
        qh = f_ref[:, COL_Q + h * HEAD_DIM:COL_Q + (h + 1) * HEAD_DIM]
        kh = f_ref[:, COL_K + h * HEAD_DIM:COL_K + (h + 1) * HEAD_DIM]
        q_ref[:, sl] = rope_head(_rms_rows(qh, qg_ref[...])).astype(BF16)
        k_ref[:, sl] = rope_head(_rms_rows(kh, kg_ref[...])).astype(BF16)
    v_ref[...] = f_ref[:, COL_V:COL_V + ATTN_W].astype(BF16)
    xq_all = f_ref[:, COL_KI:COL_KI + XATTN_W + LANES][:, XQ_SHIFT:XQ_SHIFT + XATTN_W]
    for h in range(XATTN_HEADS):
        sl = slice(h * HEAD_DIM, (h + 1) * HEAD_DIM)
        xq_ref[:, sl] = _rms_rows(xq_all[:, sl], xg_ref[...]).astype(BF16)
    first_half = lane < IDX_DIM
    for pair in range(IDX_HEADS // 2):
        both = rope_idx(f_ref[:, COL_QI + pair * LANES:COL_QI + (pair + 1) * LANES])
        qi_ref[:, (2 * pair) * LANES:(2 * pair + 1) * LANES] = jnp.where(first_half, both, 0.0).astype(BF16)
        qi_ref[:, (2 * pair + 1) * LANES:(2 * pair + 2) * LANES] = jnp.where(first_half, 0.0, both).astype(BF16)
    ki_tile = f_ref[:, COL_KI:COL_KI + LANES]
    ki = jnp.where(first_half, rope_idx(ki_tile), 0.0)
    ki_ref[...] = (ki + pltpu.roll(ki, IDX_DIM, 1)).astype(BF16)
    wi = ki_tile * (IDX_HEADS ** -0.5) * (IDX_DIM ** -0.5)
    for h in range(IDX_HEADS):
        wib_ref[:, h * LANES:(h + 1) * LANES] = jnp.broadcast_to(wi[:, IDX_DIM + h:IDX_DIM + h + 1], (tm, LANES))
    qi_ref[:, IDX_HEADS * LANES:] = jnp.zeros((tm, LANES), BF16)
    wib_ref[:, IDX_HEADS * LANES:] = jnp.zeros((tm, LANES), F32)


def _prep(main, pos_col, invf, qg, kg, xg, *, tm):
    s = main.shape[0]
    row = lambda w: pl.BlockSpec((tm, w), lambda i: (i, 0))
    small = lambda shape: pl.BlockSpec(shape, lambda i: (0, 0))
    outs = [(row(ATTN_W), (s, ATTN_W), BF16), (row(ATTN_W), (s, ATTN_W), BF16), (row(ATTN_W), (s, ATTN_W), BF16),
            (row(XATTN_W), (s, XATTN_W), BF16), (row(IDX_PITCH), (s, IDX_PITCH), BF16),
            (row(LANES), (s, LANES), BF16), (row(IDX_PITCH), (s, IDX_PITCH), F32)]
    return pl.pallas_call(
        _prep_kernel,
        grid=(s // tm,),
        in_specs=[row(MAIN_W), row(1), small((2, LANES)), small((1, LANES)), small((1, LANES)),
                  small((1, LANES))],
        out_specs=[spec for spec, _, _ in outs],
        out_shape=[jax.ShapeDtypeStruct(shape, dt) for _, shape, dt in outs],
        compiler_params=_cparams(("parallel",)),
        name="qknorm_rope",
    )(main, pos_col, invf, qg, kg, xg)


def _pool_kernel(u_ref, halo_ref, bd_ref, sc_ref, o_ref, buf_ref):
    i = pl.program_id(0)
    tm = u_ref.shape[0]
    u = u_ref[...]
    buf_ref[0:POOL_HALO, :] = jnp.where(i == 0, 0.0, halo_ref[...])
    buf_ref[POOL_HALO:POOL_HALO + tm, :] = u
    t = i * tm + lax.broadcasted_iota(I32, (tm, 1), 0)
    lane = lax.broadcasted_iota(I32, (1, POOL_W), 1)
    acc = u
    pooled = None
    for d in range(1, max(POOL_WINDOWS)):
        acc = acc + buf_ref[POOL_HALO - d:POOL_HALO - d + tm, :]
        if d + 1 in POOL_WINDOWS:
            g = POOL_WINDOWS.index(d + 1)
            mean = acc / jnp.minimum(t + 1, d + 1).astype(F32)
            in_group = (lane >= g * POOL_GROUP_W) & (lane < (g + 1) * POOL_GROUP_W)
            pooled = jnp.where(in_group, mean, 0.0 if pooled is None else pooled)
    p = (pooled - u).astype(BF16)
    y = jnp.dot(p, bd_ref[...], preferred_element_type=F32) * sc_ref[...]
    o_ref[...] = y.astype(BF16)


def _pool(front, bd, scale, *, tm):
    s = front.shape[0]
    halo_blocks = tm // POOL_HALO
    return pl.pallas_call(
        _pool_kernel,
        grid=(s // tm,),
        in_specs=[
            pl.BlockSpec((tm, POOL_W), lambda i: (i, 0)),
            pl.BlockSpec((POOL_HALO, POOL_W), lambda i: (jnp.maximum(i * halo_blocks - 1, 0), 0)),
            pl.BlockSpec((POOL_W, POOL_W), lambda i: (0, 0)),
            pl.BlockSpec((1, POOL_W), lambda i: (0, 0)),
        ],
        out_specs=pl.BlockSpec((tm, POOL_W), lambda i: (i, 0)),
        out_shape=jax.ShapeDtypeStruct((s, POOL_W), BF16),
        scratch_shapes=[pltpu.VMEM((tm + POOL_HALO, POOL_W), F32)],
        compiler_params=_cparams(("parallel",)),
        name="pool_mixer",
    )(front, front, bd, scale)


def _memkv_kernel(mem_ref, g_ref, w_ref, kg_ref, k_ref, v_ref):
    h = _rms_rows(mem_ref[...], g_ref[...]).astype(BF16)
    kv = jnp.dot(h, w_ref[...], preferred_element_type=F32)
    for hd in range(XATTN_HEADS):
        sl = slice(hd * HEAD_DIM, (hd + 1) * HEAD_DIM)
        k_ref[:, sl] = _rms_rows(kv[:, sl], kg_ref[...]).astype(BF16)
    v_ref[...] = kv[:, XATTN_W:].astype(BF16)


def _memkv(mem, g, w, kg):
    m = mem.shape[0]
    return pl.pallas_call(
        _memkv_kernel,
        out_shape=[jax.ShapeDtypeStruct((m, XATTN_W), BF16), jax.ShapeDtypeStruct((m, XATTN_W), BF16)],
        compiler_params=pltpu.CompilerParams(vmem_limit_bytes=V7X_VMEM_LIMIT_BYTES),
        name="mem_kv",
    )(mem, g, w, kg)


def _cross_kernel(xq_ref, k_ref, v_ref, o_ref):
    for h in range(XATTN_HEADS):
        sl = slice(h * HEAD_DIM, (h + 1) * HEAD_DIM)
        logits = lax.dot_general(xq_ref[:, sl], k_ref[:, sl], NT_DIMS,
                                 preferred_element_type=F32) * (HEAD_DIM ** -0.5)
        e = jnp.exp(logits - jnp.max(logits, axis=-1, keepdims=True))
        p = e / jnp.sum(e, axis=-1, keepdims=True)
        o_ref[:, sl] = jnp.dot(p.astype(BF16), v_ref[:, sl], preferred_element_type=F32).astype(BF16)


def _cross(xq, k_m, v_m, *, tm):
    s = xq.shape[0]
    m = k_m.shape[0]
    return pl.pallas_call(
        _cross_kernel,
        grid=(s // tm,),
        in_specs=[pl.BlockSpec((tm, XATTN_W), lambda i: (i, 0)),
                  pl.BlockSpec((m, XATTN_W), lambda i: (0, 0)),
                  pl.BlockSpec((m, XATTN_W), lambda i: (0, 0))],
        out_specs=pl.BlockSpec((tm, XATTN_W), lambda i: (i, 0)),
        out_shape=jax.ShapeDtypeStruct((s, XATTN_W), BF16),
        compiler_params=_cparams(("parallel",)),
        name="mem_cross_attn",
    )(xq, k_m, v_m)


def _key_to_f32(key):
    bits = key ^ ((key >> 31) & 0x7FFFFFFF)
    return lax.bitcast_convert_type(bits, F32)


def _dsa_kernel(qi_ref, wib_ref, q_ref, ki_hbm, k_hbm, v_hbm, tri_hbm, o_ref, sc_ref, hi_ref, lo_ref,
                ki_ref, k_ref, v_ref, tri_ref, load_sems, thr_ref, need_ref, *, tq, kc, ka, k_sel):
    _load_once((ki_hbm, k_hbm, v_hbm, tri_hbm), (ki_ref, k_ref, v_ref, tri_ref), load_sems)
    i = pl.program_id(0)
    n_chunks = ((i + 1) * tq + kc - 1) // kc
    t = i * tq + lax.broadcasted_iota(I32, (tq, 1), 0)
    slabs = kc // LANES

    def slab(c, j):
        return pl.ds(pl.multiple_of(c * kc, kc) + j * LANES, LANES)

    def score_chunk(c, carry):
        off = pl.multiple_of(c * kc, kc)
        ki_c = ki_ref[pl.ds(off, kc), :]
        dots = [lax.dot_general(qi_ref[:, h * LANES:(h + 1) * LANES], ki_c, NT_DIMS,
                                preferred_element_type=F32) for h in range(IDX_HEADS)]
        for j in range(slabs):
            s = jnp.zeros((tq, LANES), F32)
            for h in range(IDX_HEADS):
                s = s + jnp.maximum(dots[h][:, j * LANES:(j + 1) * LANES], 0.0) * wib_ref[:, h * LANES:(h + 1) * LANES]
            kpos = off + j * LANES + lax.broadcasted_iota(I32, (1, LANES), 1)
            s = jnp.where(kpos <= t, jnp.where(s == 0.0, 0.0, s), -jnp.inf)
            sc_ref[:, slab(c, j)] = s
            bits = pltpu.bitcast(s, I32)
            key = bits ^ ((bits >> 31) & 0x7FFFFFFF)
            hi_ref[:, slab(c, j)] = (key >> 16).astype(I16)
            lo_ref[:, slab(c, j)] = ((key & 0xFFFF) + I16_MIN).astype(I16)
        return carry

    lax.fori_loop(0, n_chunks, score_chunk, 0)

    def count16(ref, cand16):
        def body(c, acc):
            for j in range(slabs):
                acc = acc + jnp.where(ref[:, slab(c, j)] >= cand16, jnp.int16(1), jnp.int16(0))
            return acc
        acc = lax.fori_loop(0, n_chunks, body, jnp.zeros((tq, LANES), I16))
        return jnp.sum(acc.astype(I32), axis=-1, keepdims=True)

    def search16(ref, counted_already):
        def step(b, best):
            cand = best + jnp.left_shift(jnp.int32(1), 15 - b)
            enough = count16(ref, cand.astype(I16)) + counted_already >= k_sel
            return jnp.where(enough, cand, best)
        return lax.fori_loop(0, 16, step, jnp.full((tq, LANES), I16_MIN, I32))

    hi_best = search16(hi_ref, 0)
    hi_best16 = hi_best.astype(I16)

    def split_chunk(c, acc):
        for j in range(slabs):
            h = hi_ref[:, slab(c, j)]
            lo_ref[:, slab(c, j)] = jnp.where(h == hi_best16, lo_ref[:, slab(c, j)], jnp.int16(I16_MIN))
            acc = acc + jnp.where(h > hi_best16, jnp.int16(1), jnp.int16(0))
        return acc

    above = lax.fori_loop(0, n_chunks, split_chunk, jnp.zeros((tq, LANES), I16))
    lo_best = search16(lo_ref, jnp.sum(above.astype(I32), axis=-1, keepdims=True))
    thr_key = jnp.left_shift(hi_best, 16) + (lo_best - I16_MIN)

    def count(pred):
        def body(c, acc):
            for j in range(slabs):
                acc = acc + pred(sc_ref[:, slab(c, j)]).astype(I32)
            return acc
        acc = lax.fori_loop(0, n_chunks, body, jnp.zeros((tq, LANES), I32))
        return jnp.sum(acc, axis=-1, keepdims=True)

    thr_found = _key_to_f32(thr_key)
    n_ge = count(lambda s: s >= thr_found)
    n_gt = count(lambda s: s > thr_found)
    thr_ref[...] = thr_found
    need_ref[...] = jnp.broadcast_to((k_sel - n_gt).astype(F32), (tq, LANES))
    n_wrong = jnp.sum(jnp.where((n_ge >= k_sel) & (n_gt < k_sel), 0.0, 1.0))

    @pl.when(n_wrong > 0.0)
    def _():
        def bisect(b, best):
            cand = best + jnp.left_shift(jnp.int32(1), 31 - b)
            cand_f = _key_to_f32(cand)
            enough = (count(lambda s: s >= cand_f) >= k_sel) | (cand < KEY_NEG_INF)
            return jnp.where(enough, cand, best)

        thr_slow = _key_to_f32(lax.fori_loop(0, 32, bisect, jnp.full((tq, LANES), INT_MIN, I32)))
        thr_ref[...] = thr_slow
        need_ref[...] = jnp.broadcast_to((k_sel - count(lambda s: s > thr_slow)).astype(F32), (tq, LANES))

    thr_col = thr_ref[:, 0:1]
    need = need_ref[:, 0:1]

    def mask_chunk(c, ties_before):
        off = pl.multiple_of(c * kc, kc)
        s = sc_ref[:, pl.ds(off, kc)]
        tie = s == thr_col
        prefix = jnp.dot(jnp.where(tie, 1.0, 0.0).astype(BF16), tri_ref[:, 0:kc], preferred_element_type=F32)
        kpos = off + lax.broadcasted_iota(I32, (1, kc), 1)
        take_tie = tie & (prefix + ties_before <= need)
        sel = ((s > thr_col) | take_tie) & (kpos <= t)
        sc_ref[:, pl.ds(off, kc)] = jnp.where(sel, 0.0, NEG_BIG)
        return ties_before + prefix[:, kc - 1:kc]

    lax.fori_loop(0, n_chunks, mask_chunk, jnp.zeros((tq, 1), F32))

    scale2 = (HEAD_DIM ** -0.5) * math.log2(math.e)
    heads = [slice(h * HEAD_DIM, (h + 1) * HEAD_DIM) for h in range(ATTN_HEADS)]

    assert ka in (kc, 2 * kc)
    if ka == 2 * kc:
        @pl.when(n_chunks % 2 == 1)
        def _():
            sc_ref[:, pl.ds(pl.multiple_of(n_chunks * kc, kc), kc)] = jnp.full((tq, kc), NEG_BIG, F32)

    def unshifted_chunk(c, carry):
        off = pl.multiple_of(c * ka, ka)
        bias = sc_ref[:, pl.ds(off, ka)]
        out = []
        for sl, (lane_sum, acc) in zip(heads, carry):
            p = jnp.exp2(lax.dot_general(q_ref[:, sl], k_ref[pl.ds(off, ka), sl], NT_DIMS,
                                         preferred_element_type=F32) * scale2 + bias)
            for j in range(ka // LANES):
                lane_sum = lane_sum + p[:, j * LANES:(j + 1) * LANES]
            acc = acc + jnp.dot(p.astype(BF16), v_ref[pl.ds(off, ka), sl], preferred_element_type=F32)
            out.append((lane_sum, acc))
        return tuple(out)

    zeros = jnp.zeros((tq, LANES), F32)
    result = lax.fori_loop(0, (n_chunks * kc + ka - 1) // ka, unshifted_chunk,
                           tuple((zeros, zeros) for _ in heads))
    in_range = None
    for sl, (lane_sum, acc) in zip(heads, result):
        row_sum = jnp.sum(lane_sum, axis=-1, keepdims=True)
        o_ref[:, sl] = (acc / row_sum).astype(BF16)
        ok = (row_sum >= SOFTMAX_SUM_MIN) & (row_sum <= SOFTMAX_SUM_MAX)
        in_range = ok if in_range is None else (in_range & ok)
    n_bad = jnp.sum(jnp.where(in_range, 0.0, 1.0))

    @pl.when(n_bad > 0.0)
    def _():
        _shifted_attention(q_ref, k_ref, v_ref, sc_ref, o_ref, n_chunks, tq=tq, kc=kc, scale2=scale2)


def _shifted_attention(q_ref, k_ref, v_ref, sc_ref, o_ref, n_chunks, *, tq, kc, scale2):
    slabs = kc // LANES
    for h in range(ATTN_HEADS):
        sl = slice(h * HEAD_DIM, (h + 1) * HEAD_DIM)

        def logits(c, sl=sl):
            off = pl.multiple_of(c * kc, kc)
            return lax.dot_general(q_ref[:, sl], k_ref[pl.ds(off, kc), sl], NT_DIMS,
                                   preferred_element_type=F32) * scale2 + sc_ref[:, pl.ds(off, kc)]

        def max_chunk(c, lane_max, logits=logits):
            s = logits(c)
            for j in range(slabs):
                lane_max = jnp.maximum(lane_max, s[:, j * LANES:(j + 1) * LANES])
            return lane_max

        lane_max = lax.fori_loop(0, n_chunks, max_chunk, jnp.full((tq, LANES), NEG_BIG, F32))
        row_max = jnp.max(lane_max, axis=-1, keepdims=True)

        def pv_chunk(c, carry, sl=sl, row_max=row_max, logits=logits):
            lane_sum, acc = carry
            off = pl.multiple_of(c * kc, kc)
            p = jnp.exp2(logits(c) - row_max)
            for j in range(slabs):
                lane_sum = lane_sum + p[:, j * LANES:(j + 1) * LANES]
            acc = acc + jnp.dot(p.astype(BF16), v_ref[pl.ds(off, kc), sl], preferred_element_type=F32)
            return lane_sum, acc

        lane_sum, acc = lax.fori_loop(0, n_chunks, pv_chunk,
                                      (jnp.zeros((tq, LANES), F32), jnp.zeros((tq, HEAD_DIM), F32)))
        o_ref[:, sl] = (acc / jnp.sum(lane_sum, axis=-1, keepdims=True)).astype(BF16)


def _dsa(qi, wib, q, ki, k, v, *, tq, kc, ka, k_sel):
    s = q.shape[0]
    tri = jnp.pad(jnp.triu(jnp.ones((kc, kc), BF16)), ((0, 0), (0, LANES)))
    row = lambda w: pl.BlockSpec((tq, w), lambda i: (i, 0))
    return pl.pallas_call(
        functools.partial(_dsa_kernel, tq=tq, kc=kc, ka=ka, k_sel=k_sel),
        grid=(s // tq,),
        in_specs=[row(IDX_PITCH), row(IDX_PITCH), row(ATTN_W),
                  _hbm_spec(), _hbm_spec(), _hbm_spec(), _hbm_spec()],
        out_specs=row(ATTN_W),
        out_shape=jax.ShapeDtypeStruct((s, ATTN_W), BF16),
        scratch_shapes=[pltpu.VMEM((tq, s + LANES), F32), pltpu.VMEM((tq, s + LANES), I16),
                        pltpu.VMEM((tq, s + LANES), I16),
                        pltpu.VMEM(ki.shape, BF16), pltpu.VMEM(k.shape, BF16), pltpu.VMEM(v.shape, BF16),
                        pltpu.VMEM(tri.shape, BF16), pltpu.SemaphoreType.DMA((4,)),
                        pltpu.VMEM((tq, LANES), F32), pltpu.VMEM((tq, LANES), F32)],
        compiler_params=_cparams(("arbitrary",)),
        name="dsa_attention",
    )(qi, wib, q, ki, k, v, tri)


def _route(logits):
    lane = lax.broadcasted_iota(I32, (1, LANES), 1).astype(F32)
    far = float(LANES)
    is_group = lane < N_GROUPS
    g_max = jnp.max(jnp.where(is_group, logits, -jnp.inf), axis=-1, keepdims=True)
    g_top = jnp.min(jnp.where(is_group & (logits == g_max), lane, far), axis=-1, keepdims=True)
    pg_top = 1.0 / jnp.sum(jnp.where(is_group, jnp.exp(logits - g_max), 0.0), axis=-1, keepdims=True)
    first = N_GROUPS + EXPERTS_PER_GROUP * g_top
    in_grp = (lane >= first) & (lane < first + EXPERTS_PER_GROUP)
    e_max = jnp.max(jnp.where(in_grp, logits, -jnp.inf), axis=-1, keepdims=True)
    e_exp = jnp.where(in_grp, jnp.exp(logits - e_max), 0.0)
    pe = jnp.where(in_grp, e_exp / jnp.sum(e_exp, axis=-1, keepdims=True), -1.0)
    p1 = jnp.max(pe, axis=-1, keepdims=True)
    e1 = jnp.min(jnp.where(pe == p1, lane, far), axis=-1, keepdims=True)
    pe_rest = jnp.where(lane == e1, -1.0, pe)
    p2 = jnp.max(pe_rest, axis=-1, keepdims=True)
    e2 = jnp.min(jnp.where(pe_rest == p2, lane, far), axis=-1, keepdims=True)
    den = p1 + p2
    w1 = pg_top * p1 / den
    w2 = pg_top * p2 / den
    lo = jnp.minimum(e1, e2) - first
    hi = jnp.maximum(e1, e2) - first
    pair = lo * (2 * EXPERTS_PER_GROUP - 1 - lo) * 0.5 + (hi - lo - 1.0)
    first_is_lo = e1 < e2
    return (g_top * PAIRS_PER_GROUP + pair, jnp.where(first_is_lo, w1, w2), jnp.where(first_is_lo, w2, w1))


def _merge_kernel(x_ref, p_ref, a_ref, c_ref, gt_ref, gf_ref, wpo_hbm, wao_hbm, wco_hbm, wo_hbm, wr_hbm, tri_hbm,
                  x1_ref, route_ref, cnt_ref, wpo_ref, wao_ref, wco_ref, wo_ref, wr_ref, tri_ref, load_sems):
    _load_once((wpo_hbm, wao_hbm, wco_hbm, wo_hbm, wr_hbm, tri_hbm),
               (wpo_ref, wao_ref, wco_ref, wo_ref, wr_ref, tri_ref), load_sems)

    @pl.when(pl.program_id(0) == 0)
    def _():
        cnt_ref[...] = jnp.zeros_like(cnt_ref)

    d = x_ref.shape[1]
    def gate(b):
        window = gt_ref[:, b * d:(b + 1) * d + LANES].astype(F32)
        return window[:, GATE_SHIFT:GATE_SHIFT + d]

    merged = gate(0) * jnp.dot(p_ref[...], wpo_ref[...], preferred_element_type=F32)
    merged += gate(1) * jnp.dot(a_ref[...], wao_ref[...], preferred_element_type=F32)
    merged += gate(2) * jnp.dot(c_ref[...], wco_ref[...], preferred_element_type=F32)
    x1 = x_ref[...] + jnp.dot(merged.astype(BF16), wo_ref[...], preferred_element_type=F32)
    x1_ref[...] = x1
    h2 = _rms_rows(x1, gf_ref[...]).astype(BF16)
    bucket, w_lo, w_hi = _route(jnp.dot(h2, wr_ref[...], preferred_element_type=F32))
    lane = lax.broadcasted_iota(I32, (1, LANES), 1).astype(F32)
    onehot = lane == bucket
    before = jnp.dot(tri_ref[...], jnp.where(onehot, 1.0, 0.0).astype(BF16), preferred_element_type=F32)
    rank = jnp.sum(jnp.where(onehot, before + cnt_ref[...], 0.0), axis=-1, keepdims=True)
    cnt_ref[...] += jnp.sum(jnp.where(onehot, 1.0, 0.0), axis=0, keepdims=True)
    route_ref[...] = jnp.where(lane == ROUTE_BUCKET, bucket,
                               jnp.where(lane == ROUTE_RANK, rank,
                                         jnp.where(lane == ROUTE_W_LO, w_lo,
                                                   jnp.where(lane == ROUTE_W_HI, w_hi, 0.0))))


def _merge(x, p, a, c, gates, wpo, wao, wco, wo, gf, wr, *, tm):
    s, d = x.shape
    row = lambda w: pl.BlockSpec((tm, w), lambda i: (i, 0))
    tri = jnp.tril(jnp.ones((tm, tm), BF16), -1)
    weights = (wpo, wao, wco, wo, wr, tri)
    return pl.pallas_call(
        _merge_kernel,
        grid=(s // tm,),
        in_specs=[row(d), row(POOL_W), row(ATTN_W), row(XATTN_W), row(gates.shape[1]),
                  pl.BlockSpec(gf.shape, lambda i: (0, 0))] + [_hbm_spec()] * len(weights),
        out_specs=[row(d), row(LANES)],
        out_shape=[jax.ShapeDtypeStruct((s, d), F32), jax.ShapeDtypeStruct((s, LANES), F32)],
        scratch_shapes=[pltpu.VMEM((1, LANES), F32)] + [pltpu.VMEM(w.shape, BF16) for w in weights]
                       + [pltpu.SemaphoreType.DMA((len(weights),))],
        compiler_params=_cparams(("arbitrary",)),
        name="gated_merge_router",
    )(x, p, a, c, gates, gf, *weights)


def _moe_kernel(src_ref, elo_ref, ehi_ref, nvalid_ref, ntiles_ref,
                x1_hbm, w_ref, gf_ref, wg_lo, wu_lo, wd_lo, wg_hi, wu_hi, wd_hi,
                out_hbm, xbuf, obuf, gsem, ssem, *, tile):
    j = pl.program_id(0)
    n_tiles = ntiles_ref[0]
    slot = lax.rem(j, 2)

    def start_gather(t, s):
        def body(r, carry):
            tok = src_ref[t * tile + r]
            pltpu.make_async_copy(x1_hbm.at[pl.ds(tok, 1)], xbuf.at[s, pl.ds(r, 1)], gsem.at[s]).start()
            return carry
        lax.fori_loop(0, tile, body, 0, unroll=8)

    def wait_scatter(t, s):
        n = nvalid_ref[t]
        n_whole = pl.multiple_of(lax.shift_left(lax.shift_right_logical(n, 3), 3), SUBLANES)

        @pl.when(n_whole > 0)
        def _():
            pltpu.make_async_copy(obuf.at[s, pl.ds(0, n_whole)], out_hbm.at[pl.ds(0, n_whole)], ssem.at[s]).wait()

        def wait_row(r, carry):
            pltpu.make_async_copy(obuf.at[s, pl.ds(0, 1)], out_hbm.at[pl.ds(0, 1)], ssem.at[s]).wait()
            return carry
        lax.fori_loop(0, n - n_whole, wait_row, 0)

    @pl.when(j == 0)
    def _():
        start_gather(0, 0)

    @pl.when(j + 1 < n_tiles)
    def _():
        start_gather(j + 1, 1 - slot)

    @pl.when(j < n_tiles)
    def _():
        pltpu.make_async_copy(x1_hbm.at[pl.ds(0, tile)], xbuf.at[slot], gsem.at[slot]).wait()
        x1 = xbuf[slot]
        h = _rms_rows(x1, gf_ref[...]).astype(BF16)
        y = x1
        for w_col, wg, wu, wd in ((ROUTE_W_LO, wg_lo, wu_lo, wd_lo), (ROUTE_W_HI, wg_hi, wu_hi, wd_hi)):
            a = jnp.dot(h, wg[0], preferred_element_type=F32)
            b = jnp.dot(h, wu[0], preferred_element_type=F32)
            act = (a * _sigmoid(a)) * b * w_ref[:, w_col:w_col + 1]
            y = y + jnp.dot(act.astype(BF16), wd[0], preferred_element_type=F32)
        obuf[slot] = y

        def scatter_row(r, carry):
            tok = src_ref[j * tile + r]
            pltpu.make_async_copy(obuf.at[slot, pl.ds(r, 1)], out_hbm.at[pl.ds(tok, 1)], ssem.at[slot]).start()
            return carry
        lax.fori_loop(0, nvalid_ref[j], scatter_row, 0)

        @pl.when(j >= 1)
        def _():
            wait_scatter(j - 1, 1 - slot)

        @pl.when(j == n_tiles - 1)
        def _():
            wait_scatter(j, slot)


def _int_from_comparisons(v, n_bits):
    out = jnp.zeros(v.shape, I32)
    for b in range(n_bits):
        bit_set = jnp.floor(v / 2.0 ** b) - 2.0 * jnp.floor(v / 2.0 ** (b + 1)) >= 0.5
        out = out + jnp.where(bit_set, 1 << b, 0)
    return out


def _moe(x1, route, gf, wg, wu, wd, *, tile):
    s, d = x1.shape
    ff = wg.shape[2]
    n_buckets = N_GROUPS * PAIRS_PER_GROUP
    max_tiles = (s + n_buckets * (tile - 1)) // tile
    bucket = _int_from_comparisons(route[:, ROUTE_BUCKET], 5)
    rank = _int_from_comparisons(route[:, ROUTE_RANK], 14)
    counts = jnp.zeros((n_buckets,), I32).at[bucket].add(1)
    padded = (counts + tile - 1) // tile * tile
    ends = jnp.cumsum(padded)
    starts = ends - padded
    dest = starts[bucket] + rank
    src = jnp.zeros((max_tiles * tile,), I32).at[dest].set(jnp.arange(s, dtype=I32))
    w_sorted = jnp.zeros((max_tiles * tile, ROUTE_COLS), F32).at[dest].set(route[:, :ROUTE_COLS])
    n_tiles = ends[-1] // tile
    tile_start = jnp.arange(max_tiles, dtype=I32) * tile
    first_row = jnp.minimum(tile_start, ends[-1] - tile)
    tile_bucket = jnp.sum((ends[None, :] <= first_row[:, None]).astype(I32), axis=1)
    nvalid = jnp.where(tile_start < ends[-1],
                       jnp.clip(counts[tile_bucket] - (tile_start - starts[tile_bucket]), 0, tile), 0).astype(I32)
    pair_lo = jnp.array([0, 0, 0, 1, 1, 2], I32)
    pair_hi = jnp.array([1, 2, 3, 2, 3, 3], I32)
    group = tile_bucket // PAIRS_PER_GROUP
    e_lo = group * EXPERTS_PER_GROUP + pair_lo[tile_bucket % PAIRS_PER_GROUP]
    e_hi = group * EXPERTS_PER_GROUP + pair_hi[tile_bucket % PAIRS_PER_GROUP]

    lo_w = lambda shape: pl.BlockSpec(shape, lambda j, src, elo, ehi, nv, nt: (elo[j], 0, 0))
    hi_w = lambda shape: pl.BlockSpec(shape, lambda j, src, elo, ehi, nv, nt: (ehi[j], 0, 0))
    grid_spec = pltpu.PrefetchScalarGridSpec(
        num_scalar_prefetch=5,
        grid=(max_tiles,),
        in_specs=[pl.BlockSpec(memory_space=pl.ANY),
                  pl.BlockSpec((tile, ROUTE_COLS), lambda j, *_: (j, 0)),
                  pl.BlockSpec((1, d), lambda j, *_: (0, 0)),
                  lo_w((1, d, ff)), lo_w((1, d, ff)), lo_w((1, ff, d)),
                  hi_w((1, d, ff)), hi_w((1, d, ff)), hi_w((1, ff, d))],
        out_specs=pl.BlockSpec(memory_space=pl.ANY),
        scratch_shapes=[pltpu.VMEM((2, tile, d), F32), pltpu.VMEM((2, tile, d), F32),
                        pltpu.SemaphoreType.DMA((2,)), pltpu.SemaphoreType.DMA((2,))],
    )
    return pl.pallas_call(
        functools.partial(_moe_kernel, tile=tile),
        grid_spec=grid_spec,
        out_shape=jax.ShapeDtypeStruct((s, d), F32),
        compiler_params=_cparams(("arbitrary",)),
        name="moe_experts",
    )(src, e_lo, e_hi, nvalid, n_tiles.reshape(1), x1, w_sorted, gf, wg, wu, wd, wg, wu, wd)


def _rope_inv_freq():
    def inv(dim):
        return ROPE_THETA ** (-jnp.arange(0, dim, 2, dtype=F32) / dim)
    return jnp.stack([jnp.tile(inv(HEAD_DIM), LANES // (HEAD_DIM // 2)),
                      jnp.tile(inv(IDX_DIM), LANES // (IDX_DIM // 2))])


def _layer(x, mem, pos, g_mix, w_in_all, layer, b_gate, w_pool_grp, pool_scale, q_norm_g, k_norm_g, g_mem, w_mem_kv,
           xq_norm_g, xk_norm_g, w_pool_out, w_attn_out, w_cross_out, w_o, g_ffn, w_router_group,
           w_router_expert, w_e_gate, w_e_up, w_e_down):
    s, d = x.shape
    row2 = lambda v: v.reshape(1, -1)
    k_sel = min(TOPK_MAX, s // 4)

    main, h = _main_proj(x, row2(g_mix), w_in_all, layer, tm=1024)
    n_gate_cols = -(-(GATE_SHIFT + N_BRANCH * d) // GATE_TN) * GATE_TN
    bias_window = jnp.pad(row2(b_gate), ((0, 0), (GATE_SHIFT, n_gate_cols - GATE_SHIFT - N_BRANCH * d)))
    gates = _gate_proj(h, w_in_all, layer, bias_window, tm=1024)

    q, k, v, xq, qi, ki, wib = _prep(main, pos.reshape(s, 1), _rope_inv_freq(), row2(q_norm_g),
                                     row2(k_norm_g), row2(xq_norm_g), tm=512)

    bd = jax.scipy.linalg.block_diag(*[w_pool_grp[g] for g in range(len(POOL_WINDOWS))])
    pool_pre = _pool(main, bd.astype(BF16), row2(pool_scale), tm=512)

    k_m, v_m = _memkv(mem, row2(g_mem), w_mem_kv.astype(BF16), row2(xk_norm_g))
    cross = _cross(xq, k_m, v_m, tm=512)

    attn = _dsa(qi, wib, q, ki, k, v, tq=256, kc=512, ka=1024, k_sel=k_sel)

    w_router = jnp.pad(jnp.concatenate([w_router_group, w_router_expert], axis=1),
                       ((0, 0), (0, LANES - N_GROUPS - N_EXPERTS)))
    x1, route = _merge(x, pool_pre, attn, cross, gates, w_pool_out.astype(BF16), w_attn_out.astype(BF16),
                       w_cross_out.astype(BF16), w_o.astype(BF16), row2(g_ffn), w_router.astype(BF16), tm=256)
    return _moe(x1, route, row2(g_ffn), w_e_gate.astype(BF16), w_e_up.astype(BF16), w_e_down.astype(BF16),
                tile=256)


def kernel(x, mem, positions, g_mix, w_in, b_gate, w_pool_grp, pool_scale, q_norm_g, k_norm_g, g_mem, w_mem_kv,
           xq_norm_g, xk_norm_g, w_pool_out, w_attn_out, w_cross_out, w_o, g_ffn, w_router_group,
           w_router_expert, w_e_gate, w_e_up, w_e_down):
    depth = g_mix.shape[0]
    w_in_t = jnp.swapaxes(w_in, 1, 2)
    outs = []
    for b in range(x.shape[0]):
        xb = x[b]
        for l in range(depth):
            xb = _layer(xb, mem[b], positions[b], g_mix[l], w_in_t, l, b_gate[l], w_pool_grp[l], pool_scale[l],
                        q_norm_g[l], k_norm_g[l], g_mem[l], w_mem_kv[l], xq_norm_g[l], xk_norm_g[l],
                        w_pool_out[l], w_attn_out[l], w_cross_out[l], w_o[l], g_ffn[l], w_router_group[l],
                        w_router_expert[l], w_e_gate[l], w_e_up[l], w_e_down[l])
        outs.append(xb)
    return jnp.stack(outs)
```

```python
import functools
import math

import jax
import jax.numpy as jnp
from jax import lax
from jax.experimental import pallas as pl
from jax.experimental.pallas import tpu as pltpu

F32 = jnp.float32
BF16 = jnp.bfloat16
I32 = jnp.int32
I16 = jnp.int16
I16_MIN = -32768

NORM_EPS = 1e-6
ROPE_THETA = 10000.0
HEAD_DIM = 128
ATTN_HEADS = 6
ATTN_W = ATTN_HEADS * HEAD_DIM
IDX_HEADS = 4
IDX_DIM = 64
XATTN_HEADS = 4
XATTN_W = XATTN_HEADS * HEAD_DIM
POOL_WINDOWS = (2, 4, 8, 16)
POOL_GROUP_W = 192
POOL_W = len(POOL_WINDOWS) * POOL_GROUP_W
POOL_HALO = 16
TOPK_MAX = 256
N_GROUPS = 4
EXPERTS_PER_GROUP = 4
N_EXPERTS = N_GROUPS * EXPERTS_PER_GROUP
N_BRANCH = 3
PAIRS_PER_GROUP = EXPERTS_PER_GROUP * (EXPERTS_PER_GROUP - 1) // 2
ROUTE_BUCKET, ROUTE_RANK, ROUTE_W_LO, ROUTE_W_HI, ROUTE_COLS = 0, 1, 2, 3, 4

LANES = 128
SUBLANES = 8
V7X_VMEM_LIMIT_BYTES = 56 * 1024 * 1024

COL_POOL = 0
COL_Q = COL_POOL + POOL_W
COL_K = COL_Q + ATTN_W
COL_V = COL_K + ATTN_W
COL_QI = COL_V + ATTN_W
COL_KI = COL_QI + IDX_HEADS * IDX_DIM
XQ_SHIFT = IDX_DIM + IDX_HEADS
GATE_COL0 = COL_KI + XQ_SHIFT + XATTN_W
MAIN_TN = 512
MAIN_W = -(-GATE_COL0 // MAIN_TN) * MAIN_TN
GATE_TN = 1280
GATE_WINDOW0 = GATE_COL0 // GATE_TN * GATE_TN
GATE_SHIFT = GATE_COL0 - GATE_WINDOW0
assert COL_QI % LANES == 0 and COL_KI % LANES == 0 and GATE_SHIFT < LANES
IDX_PITCH = IDX_HEADS * LANES + LANES

NEG_BIG = -1e30
INT_MIN = -2147483648
KEY_NEG_INF = -2139095041
SOFTMAX_SUM_MIN = 2.0 ** -64
SOFTMAX_SUM_MAX = 2.0 ** 100
NT_DIMS = (((1,), (1,)), ((), ()))


def _cparams(semantics):
    return pltpu.CompilerParams(dimension_semantics=semantics, vmem_limit_bytes=V7X_VMEM_LIMIT_BYTES)


def _load_once(hbm_refs, vmem_refs, sems):
    @pl.when(pl.program_id(0) == 0)
    def _():
        copies = [pltpu.make_async_copy(src, dst, sems.at[n])
                  for n, (src, dst) in enumerate(zip(hbm_refs, vmem_refs))]
        for c in copies:
            c.start()
        for c in copies:
            c.wait()


def _hbm_spec():
    return pl.BlockSpec(memory_space=pl.ANY)


def _rms_rows(x, g):
    ms = jnp.mean(x * x, axis=-1, keepdims=True)
    return x * lax.rsqrt(ms + NORM_EPS) * g


def _sigmoid(x):
    return 1.0 / (1.0 + jnp.exp(-x))


def _main_proj_kernel(x_ref, g_ref, w_ref, o_ref, h_ref):
    @pl.when(pl.program_id(1) == 0)
    def _():
        h_ref[...] = _rms_rows(x_ref[...], g_ref[...]).astype(BF16)

    o_ref[...] = lax.dot_general(h_ref[...], w_ref[...].astype(BF16), NT_DIMS, preferred_element_type=F32)


def _main_proj(x, g, w_in_all, layer, *, tm):
    s, d = x.shape
    return pl.pallas_call(
        _main_proj_kernel,
        grid=(s // tm, MAIN_W // MAIN_TN),
        in_specs=[pl.BlockSpec((tm, d), lambda i, j: (i, 0)),
                  pl.BlockSpec((1, d), lambda i, j: (0, 0)),
                  pl.BlockSpec((None, MAIN_TN, d), lambda i, j: (layer, j, 0))],
        out_specs=[pl.BlockSpec((tm, MAIN_TN), lambda i, j: (i, j)),
                   pl.BlockSpec((tm, d), lambda i, j: (i, 0))],
        out_shape=[jax.ShapeDtypeStruct((s, MAIN_W), F32), jax.ShapeDtypeStruct((s, d), BF16)],
        compiler_params=_cparams(("parallel", "arbitrary")),
        name="norm_proj_main",
    )(x, g, w_in_all)


def _gate_proj_kernel(h_ref, w_ref, b_ref, o_ref, wb_ref, *, n_cols):
    @pl.when(pl.program_id(1) == 0)
    def _():
        col = GATE_WINDOW0 + pl.program_id(0) * GATE_TN + lax.broadcasted_iota(I32, (GATE_TN, 1), 0)
        wb_ref[...] = jnp.where(col < n_cols, w_ref[...], 0.0).astype(BF16)

    acc = lax.dot_general(h_ref[...], wb_ref[...], NT_DIMS, preferred_element_type=F32)
    o_ref[...] = _sigmoid(acc + b_ref[...]).astype(BF16)


def _gate_proj(h, w_in_all, layer, bias_window, *, tm):
    s, d = h.shape
    n = bias_window.shape[1]
    first = GATE_WINDOW0 // GATE_TN
    return pl.pallas_call(
        functools.partial(_gate_proj_kernel, n_cols=w_in_all.shape[1]),
        grid=(n // GATE_TN, s // tm),
        in_specs=[pl.BlockSpec((tm, d), lambda j, i: (i, 0)),
                  pl.BlockSpec((None, GATE_TN, d), lambda j, i: (layer, first + j, 0)),
                  pl.BlockSpec((1, GATE_TN), lambda j, i: (0, j))],
        out_specs=pl.BlockSpec((tm, GATE_TN), lambda j, i: (i, j)),
        out_shape=jax.ShapeDtypeStruct((s, n), BF16),
        scratch_shapes=[pltpu.VMEM((GATE_TN, d), BF16)],
        compiler_params=_cparams(("parallel", "arbitrary")),
        name="gate_proj",
    )(h, w_in_all, bias_window)


def _prep_kernel(f_ref, pos_ref, invf_ref, qg_ref, kg_ref, xg_ref,
                 q_ref, k_ref, v_ref, xq_ref, qi_ref, ki_ref, wib_ref):
    tm = f_ref.shape[0]
    pos = pos_ref[...].astype(F32)
    lane = lax.broadcasted_iota(I32, (1, LANES), 1)
    ang = pos * invf_ref[0:1, :]
    cos_a = jnp.cos(ang)
    sin_a = jnp.where(lane < HEAD_DIM // 2, -jnp.sin(ang), jnp.sin(ang))
    ang_i = pos * invf_ref[1:2, :]
    low_i = (lane & (IDX_DIM - 1)) < IDX_DIM // 2
    cos_i = jnp.cos(ang_i)
    sin_i = jnp.where(low_i, -jnp.sin(ang_i), jnp.sin(ang_i))

    def rope_head(y):
        return y * cos_a + pltpu.roll(y, HEAD_DIM // 2, 1) * sin_a

    def rope_idx(y):
        partner = jnp.where(low_i, pltpu.roll(y, LANES - IDX_DIM // 2, 1), pltpu.roll(y, IDX_DIM // 2, 1))
        return y * cos_i + partner * sin_i

    for h in range(ATTN_HEADS):
        sl = slice(h * HEAD_DIM, (h + 1) * HEAD_DIM)
        qh = f_ref[:, COL_Q + h * HEAD_DIM:COL_Q + (h + 1) * HEAD_DIM]
        kh = f_ref[:, COL_K + h * HEAD_DIM:COL_K + (h + 1) * HEAD_DIM]
        q_ref[:, sl] = rope_head(_rms_rows(qh, qg_ref[...])).astype(BF16)
        k_ref[:, sl] = rope_head(_rms_rows(kh, kg_ref[...])).astype(BF16)
    v_ref[...] = f_ref[:, COL_V:COL_V + ATTN_W].astype(BF16)
    xq_all = f_ref[:, COL_KI:COL_KI + XATTN_W + LANES][:, XQ_SHIFT:XQ_SHIFT + XATTN_W]
    for h in range(XATTN_HEADS):
        sl = slice(h * HEAD_DIM, (h + 1) * HEAD_DIM)
        xq_ref[:, sl] = _rms_rows(xq_all[:, sl], xg_ref[...]).astype(BF16)
    first_half = lane < IDX_DIM
    for pair in range(IDX_HEADS // 2):
        both = rope_idx(f_ref[:, COL_QI + pair * LANES:COL_QI + (pair + 1) * LANES])
        qi_ref[:, (2 * pair) * LANES:(2 * pair + 1) * LANES] = jnp.where(first_half, both, 0.0).astype(BF16)
        qi_ref[:, (2 * pair + 1) * LANES:(2 * pair + 2) * LANES] = jnp.where(first_half, 0.0, both).astype(BF16)
    ki_tile = f_ref[:, COL_KI:COL_KI + LANES]
    ki = jnp.where(first_half, rope_idx(ki_tile), 0.0)
    ki_ref[...] = (ki + pltpu.roll(ki, IDX_DIM, 1)).astype(BF16)
    wi = ki_tile * (IDX_HEADS ** -0.5) * (IDX_DIM ** -0.5)
    for h in range(IDX_HEADS):
        wib_ref[:, h * LANES:(h + 1) * LANES] = jnp.broadcast_to(wi[:, IDX_DIM + h:IDX_DIM + h + 1], (tm, LANES))
    qi_ref[:, IDX_HEADS * LANES:] = jnp.zeros((tm, LANES), BF16)
    wib_ref[:, IDX_HEADS * LANES:] = jnp.zeros((tm, LANES), F32)


def _prep(main, pos_col, invf, qg, kg, xg, *, tm):
    s = main.shape[0]
    row = lambda w: pl.BlockSpec((tm, w), lambda i: (i, 0))
    small = lambda shape: pl.BlockSpec(shape, lambda i: (0, 0))
    outs = [(row(ATTN_W), (s, ATTN_W), BF16), (row(ATTN_W), (s, ATTN_W), BF16), (row(ATTN_W), (s, ATTN_W), BF16),
            (row(XATTN_W), (s, XATTN_W), BF16), (row(IDX_PITCH), (s, IDX_PITCH), BF16),
            (row(LANES), (s, LANES), BF16), (row(IDX_PITCH), (s, IDX_PITCH), F32)]
    return pl.pallas_call(
        _prep_kernel,
        grid=(s // tm,),
        in_specs=[row(MAIN_W), row(1), small((2, LANES)), small((1, LANES)), small((1, LANES)),
                  small((1, LANES))],
        out_specs=[spec for spec, _, _ in outs],
        out_shape=[jax.ShapeDtypeStruct(shape, dt) for _, shape, dt in outs],
        compiler_params=_cparams(("parallel",)),
        name="qknorm_rope",
    )(main, pos_col, invf, qg, kg, xg)


def _pool_kernel(u_ref, halo_ref, bd_ref, sc_ref, o_ref, buf_ref):
    i = pl.program_id(0)
    tm = u_ref.shape[0]
    u = u_ref[...]
    buf_ref[0:POOL_HALO, :] = jnp.where(i == 0, 0.0, halo_ref[...])
    buf_ref[POOL_HALO:POOL_HALO + tm, :] = u
    t = i * tm + lax.broadcasted_iota(I32, (tm, 1), 0)
    lane = lax.broadcasted_iota(I32, (1, POOL_W), 1)
    acc = u
    pooled = None
    for d in range(1, max(POOL_WINDOWS)):
        acc = acc + buf_ref[POOL_HALO - d:POOL_HALO - d + tm, :]
        if d + 1 in POOL_WINDOWS:
            g = POOL_WINDOWS.index(d + 1)
            mean = acc / jnp.minimum(t + 1, d + 1).astype(F32)
            in_group = (lane >= g * POOL_GROUP_W) & (lane < (g + 1) * POOL_GROUP_W)
            pooled = jnp.where(in_group, mean, 0.0 if pooled is None else pooled)
    p = (pooled - u).astype(BF16)
    y = jnp.dot(p, bd_ref[...], preferred_element_type=F32) * sc_ref[...]
    o_ref[...] = y.astype(BF16)


def _pool(front, bd, scale, *, tm):
    s = front.shape[0]
    halo_blocks = tm // POOL_HALO
    return pl.pallas_call(
        _pool_kernel,
        grid=(s // tm,),
        in_specs=[
            pl.BlockSpec((tm, POOL_W), lambda i: (i, 0)),
            pl.BlockSpec((POOL_HALO, POOL_W), lambda i: (jnp.maximum(i * halo_blocks - 1, 0), 0)),
            pl.BlockSpec((POOL_W, POOL_W), lambda i: (0, 0)),
            pl.BlockSpec((1, POOL_W), lambda i: (0, 0)),
        ],
        out_specs=pl.BlockSpec((tm, POOL_W), lambda i: (i, 0)),
        out_shape=jax.ShapeDtypeStruct((s, POOL_W), BF16),
        scratch_shapes=[pltpu.VMEM((tm + POOL_HALO, POOL_W), F32)],
        compiler_params=_cparams(("parallel",)),
        name="pool_mixer",
    )(front, front, bd, scale)


def _memkv_kernel(mem_ref, g_ref, w_ref, kg_ref, k_ref, v_ref):
    h = _rms_rows(mem_ref[...], g_ref[...]).astype(BF16)
    kv = jnp.dot(h, w_ref[...], preferred_element_type=F32)
    for hd in range(XATTN_HEADS):
        sl = slice(hd * HEAD_DIM, (hd + 1) * HEAD_DIM)
        k_ref[:, sl] = _rms_rows(kv[:, sl], kg_ref[...]).astype(BF16)
    v_ref[...] = kv[:, XATTN_W:].astype(BF16)


def _memkv(mem, g, w, kg):
    m = mem.shape[0]
    return pl.pallas_call(
        _memkv_kernel,
        out_shape=[jax.ShapeDtypeStruct((m, XATTN_W), BF16), jax.ShapeDtypeStruct((m, XATTN_W), BF16)],
        compiler_params=pltpu.CompilerParams(vmem_limit_bytes=V7X_VMEM_LIMIT_BYTES),
        name="mem_kv",
    )(mem, g, w, kg)


def _cross_kernel(xq_ref, k_ref, v_ref, o_ref):
    for h in range(XATTN_HEADS):
        sl = slice(h * HEAD_DIM, (h + 1) * HEAD_DIM)
        logits = lax.dot_general(xq_ref[:, sl], k_ref[:, sl], NT_DIMS,
                                 preferred_element_type=F32) * (HEAD_DIM ** -0.5)
        e = jnp.exp(logits - jnp.max(logits, axis=-1, keepdims=True))
        p = e / jnp.sum(e, axis=-1, keepdims=True)
        o_ref[:, sl] = jnp.dot(p.astype(BF16), v_ref[:, sl], preferred_element_type=F32).astype(BF16)


def _cross(xq, k_m, v_m, *, tm):
    s = xq.shape[0]
    m = k_m.shape[0]
    return pl.pallas_call(
        _cross_kernel,
        grid=(s // tm,),
        in_specs=[pl.BlockSpec((tm, XATTN_W), lambda i: (i, 0)),
                  pl.BlockSpec((m, XATTN_W), lambda i: (0, 0)),
                  pl.BlockSpec((m, XATTN_W), lambda i: (0, 0))],
        out_specs=pl.BlockSpec((tm, XATTN_W), lambda i: (i, 0)),
        out_shape=jax.ShapeDtypeStruct((s, XATTN_W), BF16),
        compiler_params=_cparams(("parallel",)),
        name="mem_cross_attn",
    )(xq, k_m, v_m)


def _key_to_f32(key):
    bits = key ^ ((key >> 31) & 0x7FFFFFFF)
    return lax.bitcast_convert_type(bits, F32)


def _dsa_kernel(qi_ref, wib_ref, q_ref, ki_hbm, k_hbm, v_hbm, tri_hbm, o_ref, sc_ref, hi_ref, lo_ref,
                ki_ref, k_ref, v_ref, tri_ref, load_sems, thr_ref, need_ref, *, tq, kc, ka, k_sel):
    _load_once((ki_hbm, k_hbm, v_hbm, tri_hbm), (ki_ref, k_ref, v_ref, tri_ref), load_sems)
    i = pl.program_id(0)
    n_chunks = ((i + 1) * tq + kc - 1) // kc
    t = i * tq + lax.broadcasted_iota(I32, (tq, 1), 0)
    slabs = kc // LANES

    def slab(c, j):
        return pl.ds(pl.multiple_of(c * kc, kc) + j * LANES, LANES)

    def score_chunk(c, carry):
        off = pl.multiple_of(c * ka, ka)
        ki_c = ki_ref[pl.ds(off, ka), :]
        dots = [lax.dot_general(qi_ref[:, h * LANES:(h + 1) * LANES], ki_c, NT_DIMS,
                                preferred_element_type=F32) for h in range(IDX_HEADS)]
        for j in range(ka // LANES):
            cols = pl.ds(off + j * LANES, LANES)
            s = jnp.zeros((tq, LANES), F32)
            for h in range(IDX_HEADS):
                s = s + jnp.maximum(dots[h][:, j * LANES:(j + 1) * LANES], 0.0) * wib_ref[:, h * LANES:(h + 1) * LANES]
            kpos = off + j * LANES + lax.broadcasted_iota(I32, (1, LANES), 1)
            s = jnp.where(kpos <= t, jnp.where(s == 0.0, 0.0, s), -jnp.inf)
            sc_ref[:, cols] = s
            bits = pltpu.bitcast(s, I32)
            key = bits ^ ((bits >> 31) & 0x7FFFFFFF)
            hi_ref[:, cols] = (key >> 16).astype(I16)
            lo_ref[:, cols] = ((key & 0xFFFF) + I16_MIN).astype(I16)
        return carry

    lax.fori_loop(0, (n_chunks * kc + ka - 1) // ka, score_chunk, 0)

    def count16(ref, cand16):
        def body(c, acc):
            for j in range(slabs):
                acc = acc + jnp.where(ref[:, slab(c, j)] >= cand16, jnp.int16(1), jnp.int16(0))
            return acc
        acc = lax.fori_loop(0, n_chunks, body, jnp.zeros((tq, LANES), I16))
        return jnp.sum(acc.astype(I32), axis=-1, keepdims=True)

    def search16(ref, counted_already):
        def step(b, best):
            cand = best + jnp.left_shift(jnp.int32(1), 15 - b)
            enough = count16(ref, cand.astype(I16)) + counted_already >= k_sel
            return jnp.where(enough, cand, best)
        return lax.fori_loop(0, 16, step, jnp.full((tq, LANES), I16_MIN, I32))

    hi_best = search16(hi_ref, 0)
    hi_best16 = hi_best.astype(I16)

    def split_chunk(c, acc):
        for j in range(slabs):
            h = hi_ref[:, slab(c, j)]
            lo_ref[:, slab(c, j)] = jnp.where(h == hi_best16, lo_ref[:, slab(c, j)], jnp.int16(I16_MIN))
            acc = acc + jnp.where(h > hi_best16, jnp.int16(1), jnp.int16(0))
        return acc

    above = lax.fori_loop(0, n_chunks, split_chunk, jnp.zeros((tq, LANES), I16))
    lo_best = search16(lo_ref, jnp.sum(above.astype(I32), axis=-1, keepdims=True))
    thr_key = jnp.left_shift(hi_best, 16) + (lo_best - I16_MIN)

    def count(pred):
        def body(c, acc):
            for j in range(slabs):
                acc = acc + pred(sc_ref[:, slab(c, j)]).astype(I32)
            return acc
        acc = lax.fori_loop(0, n_chunks, body, jnp.zeros((tq, LANES), I32))
        return jnp.sum(acc, axis=-1, keepdims=True)

    thr_found = _key_to_f32(thr_key)
    n_ge = count(lambda s: s >= thr_found)
    n_gt = count(lambda s: s > thr_found)
    thr_ref[...] = thr_found
    need_ref[...] = jnp.broadcast_to((k_sel - n_gt).astype(F32), (tq, LANES))
    n_wrong = jnp.sum(jnp.where((n_ge >= k_sel) & (n_gt < k_sel), 0.0, 1.0))

    @pl.when(n_wrong > 0.0)
    def _():
        def bisect(b, best):
            cand = best + jnp.left_shift(jnp.int32(1), 31 - b)
            cand_f = _key_to_f32(cand)
            enough = (count(lambda s: s >= cand_f) >= k_sel) | (cand < KEY_NEG_INF)
            return jnp.where(enough, cand, best)

        thr_slow = _key_to_f32(lax.fori_loop(0, 32, bisect, jnp.full((tq, LANES), INT_MIN, I32)))
        thr_ref[...] = thr_slow
        need_ref[...] = jnp.broadcast_to((k_sel - count(lambda s: s > thr_slow)).astype(F32), (tq, LANES))

    thr_col = thr_ref[:, 0:1]
    need = need_ref[:, 0:1]

    def mask_chunk(c, ties_before):
        off = pl.multiple_of(c * kc, kc)
        s = sc_ref[:, pl.ds(off, kc)]
        tie = s == thr_col
        prefix = jnp.dot(jnp.where(tie, 1.0, 0.0).astype(BF16), tri_ref[:, 0:kc], preferred_element_type=F32)
        kpos = off + lax.broadcasted_iota(I32, (1, kc), 1)
        take_tie = tie & (prefix + ties_before <= need)
        sel = ((s > thr_col) | take_tie) & (kpos <= t)
        sc_ref[:, pl.ds(off, kc)] = jnp.where(sel, 0.0, NEG_BIG)
        return ties_before + prefix[:, kc - 1:kc]

    lax.fori_loop(0, n_chunks, mask_chunk, jnp.zeros((tq, 1), F32))

    scale2 = (HEAD_DIM ** -0.5) * math.log2(math.e)
    heads = [slice(h * HEAD_DIM, (h + 1) * HEAD_DIM) for h in range(ATTN_HEADS)]

    assert ka in (kc, 2 * kc)
    if ka == 2 * kc:
        @pl.when(n_chunks % 2 == 1)
        def _():
            sc_ref[:, pl.ds(pl.multiple_of(n_chunks * kc, kc), kc)] = jnp.full((tq, kc), NEG_BIG, F32)

    def unshifted_chunk(c, carry):
        off = pl.multiple_of(c * ka, ka)
        bias = sc_ref[:, pl.ds(off, ka)]
        out = []
        for sl, (lane_sum, acc) in zip(heads, carry):
            p = jnp.exp2(lax.dot_general(q_ref[:, sl], k_ref[pl.ds(off, ka), sl], NT_DIMS,
                                         preferred_element_type=F32) * scale2 + bias)
            for j in range(ka // LANES):
                lane_sum = lane_sum + p[:, j * LANES:(j + 1) * LANES]
            acc = acc + jnp.dot(p.astype(BF16), v_ref[pl.ds(off, ka), sl], preferred_element_type=F32)
            out.append((lane_sum, acc))
        return tuple(out)

    zeros = jnp.zeros((tq, LANES), F32)
    result = lax.fori_loop(0, (n_chunks * kc + ka - 1) // ka, unshifted_chunk,
                           tuple((zeros, zeros) for _ in heads))
    in_range = None
    for sl, (lane_sum, acc) in zip(heads, result):
        row_sum = jnp.sum(lane_sum, axis=-1, keepdims=True)
        o_ref[:, sl] = (acc / row_sum).astype(BF16)
        ok = (row_sum >= SOFTMAX_SUM_MIN) & (row_sum <= SOFTMAX_SUM_MAX)
        in_range = ok if in_range is None else (in_range & ok)
    n_bad = jnp.sum(jnp.where(in_range, 0.0, 1.0))

    @pl.when(n_bad > 0.0)
    def _():
        _shifted_attention(q_ref, k_ref, v_ref, sc_ref, o_ref, n_chunks, tq=tq, kc=kc, scale2=scale2)


def _shifted_attention(q_ref, k_ref, v_ref, sc_ref, o_ref, n_chunks, *, tq, kc, scale2):
    slabs = kc // LANES
    for h in range(ATTN_HEADS):
        sl = slice(h * HEAD_DIM, (h + 1) * HEAD_DIM)

        def logits(c, sl=sl):
            off = pl.multiple_of(c * kc, kc)
            return lax.dot_general(q_ref[:, sl], k_ref[pl.ds(off, kc), sl], NT_DIMS,
                                   preferred_element_type=F32) * scale2 + sc_ref[:, pl.ds(off, kc)]

        def max_chunk(c, lane_max, logits=logits):
            s = logits(c)
            for j in range(slabs):
                lane_max = jnp.maximum(lane_max, s[:, j * LANES:(j + 1) * LANES])
            return lane_max

        lane_max = lax.fori_loop(0, n_chunks, max_chunk, jnp.full((tq, LANES), NEG_BIG, F32))
        row_max = jnp.max(lane_max, axis=-1, keepdims=True)

        def pv_chunk(c, carry, sl=sl, row_max=row_max, logits=logits):
            lane_sum, acc = carry
            off = pl.multiple_of(c * kc, kc)
            p = jnp.exp2(logits(c) - row_max)
            for j in range(slabs):
                lane_sum = lane_sum + p[:, j * LANES:(j + 1) * LANES]
            acc = acc + jnp.dot(p.astype(BF16), v_ref[pl.ds(off, kc), sl], preferred_element_type=F32)
            return lane_sum, acc

        lane_sum, acc = lax.fori_loop(0, n_chunks, pv_chunk,
                                      (jnp.zeros((tq, LANES), F32), jnp.zeros((tq, HEAD_DIM), F32)))
        o_ref[:, sl] = (acc / jnp.sum(lane_sum, axis=-1, keepdims=True)).astype(BF16)


def _dsa(qi, wib, q, ki, k, v, *, tq, kc, ka, k_sel):
    s = q.shape[0]
    tri = jnp.pad(jnp.triu(jnp.ones((kc, kc), BF16)), ((0, 0), (0, LANES)))
    row = lambda w: pl.BlockSpec((tq, w), lambda i: (i, 0))
    return pl.pallas_call(
        functools.partial(_dsa_kernel, tq=tq, kc=kc, ka=ka, k_sel=k_sel),
        grid=(s // tq,),
        in_specs=[row(IDX_PITCH), row(IDX_PITCH), row(ATTN_W),
                  _hbm_spec(), _hbm_spec(), _hbm_spec(), _hbm_spec()],
        out_specs=row(ATTN_W),
        out_shape=jax.ShapeDtypeStruct((s, ATTN_W), BF16),
        scratch_shapes=[pltpu.VMEM((tq, s + LANES), F32), pltpu.VMEM((tq, s + LANES), I16),
                        pltpu.VMEM((tq, s + LANES), I16),
                        pltpu.VMEM(ki.shape, BF16), pltpu.VMEM(k.shape, BF16), pltpu.VMEM(v.shape, BF16),
                        pltpu.VMEM(tri.shape, BF16), pltpu.SemaphoreType.DMA((4,)),
                        pltpu.VMEM((tq, LANES), F32), pltpu.VMEM((tq, LANES), F32)],
        compiler_params=_cparams(("arbitrary",)),
        name="dsa_attention",
    )(qi, wib, q, ki, k, v, tri)


def _route(logits):
    lane = lax.broadcasted_iota(I32, (1, LANES), 1).astype(F32)
    far = float(LANES)
    is_group = lane < N_GROUPS
    g_max = jnp.max(jnp.where(is_group, logits, -jnp.inf), axis=-1, keepdims=True)
    g_top = jnp.min(jnp.where(is_group & (logits == g_max), lane, far), axis=-1, keepdims=True)
    pg_top = 1.0 / jnp.sum(jnp.where(is_group, jnp.exp(logits - g_max), 0.0), axis=-1, keepdims=True)
    first = N_GROUPS + EXPERTS_PER_GROUP * g_top
    in_grp = (lane >= first) & (lane < first + EXPERTS_PER_GROUP)
    e_max = jnp.max(jnp.where(in_grp, logits, -jnp.inf), axis=-1, keepdims=True)
    e_exp = jnp.where(in_grp, jnp.exp(logits - e_max), 0.0)
    pe = jnp.where(in_grp, e_exp / jnp.sum(e_exp, axis=-1, keepdims=True), -1.0)
    p1 = jnp.max(pe, axis=-1, keepdims=True)
    e1 = jnp.min(jnp.where(pe == p1, lane, far), axis=-1, keepdims=True)
    pe_rest = jnp.where(lane == e1, -1.0, pe)
    p2 = jnp.max(pe_rest, axis=-1, keepdims=True)
    e2 = jnp.min(jnp.where(pe_rest == p2, lane, far), axis=-1, keepdims=True)
    den = p1 + p2
    w1 = pg_top * p1 / den
    w2 = pg_top * p2 / den
    lo = jnp.minimum(e1, e2) - first
    hi = jnp.maximum(e1, e2) - first
    pair = lo * (2 * EXPERTS_PER_GROUP - 1 - lo) * 0.5 + (hi - lo - 1.0)
    first_is_lo = e1 < e2
    return (g_top * PAIRS_PER_GROUP + pair, jnp.where(first_is_lo, w1, w2), jnp.where(first_is_lo, w2, w1))


def _merge_kernel(x_ref, p_ref, a_ref, c_ref, gt_ref, gf_ref, wpo_hbm, wao_hbm, wco_hbm, wo_hbm, wr_hbm, tri_hbm,
                  x1_ref, route_ref, cnt_ref, wpo_ref, wao_ref, wco_ref, wo_ref, wr_ref, tri_ref, load_sems):
    _load_once((wpo_hbm, wao_hbm, wco_hbm, wo_hbm, wr_hbm, tri_hbm),
               (wpo_ref, wao_ref, wco_ref, wo_ref, wr_ref, tri_ref), load_sems)

    @pl.when(pl.program_id(0) == 0)
    def _():
        cnt_ref[...] = jnp.zeros_like(cnt_ref)

    d = x_ref.shape[1]
    def gate(b):
        window = gt_ref[:, b * d:(b + 1) * d + LANES].astype(F32)
        return window[:, GATE_SHIFT:GATE_SHIFT + d]

    merged = gate(0) * jnp.dot(p_ref[...], wpo_ref[...], preferred_element_type=F32)
    merged += gate(1) * jnp.dot(a_ref[...], wao_ref[...], preferred_element_type=F32)
    merged += gate(2) * jnp.dot(c_ref[...], wco_ref[...], preferred_element_type=F32)
    x1 = x_ref[...] + jnp.dot(merged.astype(BF16), wo_ref[...], preferred_element_type=F32)
    x1_ref[...] = x1
    h2 = _rms_rows(x1, gf_ref[...]).astype(BF16)
    bucket, w_lo, w_hi = _route(jnp.dot(h2, wr_ref[...], preferred_element_type=F32))
    lane = lax.broadcasted_iota(I32, (1, LANES), 1).astype(F32)
    onehot = lane == bucket
    before = jnp.dot(tri_ref[...], jnp.where(onehot, 1.0, 0.0).astype(BF16), preferred_element_type=F32)
    rank = jnp.sum(jnp.where(onehot, before + cnt_ref[...], 0.0), axis=-1, keepdims=True)
    cnt_ref[...] += jnp.sum(jnp.where(onehot, 1.0, 0.0), axis=0, keepdims=True)
    route_ref[...] = jnp.where(lane == ROUTE_BUCKET, bucket,
                               jnp.where(lane == ROUTE_RANK, rank,
                                         jnp.where(lane == ROUTE_W_LO, w_lo,
                                                   jnp.where(lane == ROUTE_W_HI, w_hi, 0.0))))


def _merge(x, p, a, c, gates, wpo, wao, wco, wo, gf, wr, *, tm):
    s, d = x.shape
    row = lambda w: pl.BlockSpec((tm, w), lambda i: (i, 0))
    tri = jnp.tril(jnp.ones((tm, tm), BF16), -1)
    weights = (wpo, wao, wco, wo, wr, tri)
    return pl.pallas_call(
        _merge_kernel,
        grid=(s // tm,),
        in_specs=[row(d), row(POOL_W), row(ATTN_W), row(XATTN_W), row(gates.shape[1]),
                  pl.BlockSpec(gf.shape, lambda i: (0, 0))] + [_hbm_spec()] * len(weights),
        out_specs=[row(d), row(LANES)],
        out_shape=[jax.ShapeDtypeStruct((s, d), F32), jax.ShapeDtypeStruct((s, LANES), F32)],
        scratch_shapes=[pltpu.VMEM((1, LANES), F32)] + [pltpu.VMEM(w.shape, BF16) for w in weights]
                       + [pltpu.SemaphoreType.DMA((len(weights),))],
        compiler_params=_cparams(("arbitrary",)),
        name="gated_merge_router",
    )(x, p, a, c, gates, gf, *weights)


def _moe_kernel(src_ref, elo_ref, ehi_ref, nvalid_ref, ntiles_ref,
                x1_hbm, w_ref, gf_ref, wg_lo, wu_lo, wd_lo, wg_hi, wu_hi, wd_hi,
                out_hbm, xbuf, obuf, gsem, ssem, *, tile):
    j = pl.program_id(0)
    n_tiles = ntiles_ref[0]
    slot = lax.rem(j, 2)

    def start_gather(t, s):
        def body(r, carry):
            tok = src_ref[t * tile + r]
            pltpu.make_async_copy(x1_hbm.at[pl.ds(tok, 1)], xbuf.at[s, pl.ds(r, 1)], gsem.at[s]).start()
            return carry
        lax.fori_loop(0, tile, body, 0, unroll=8)

    def wait_scatter(t, s):
        n = nvalid_ref[t]
        n_whole = pl.multiple_of(lax.shift_left(lax.shift_right_logical(n, 3), 3), SUBLANES)

        @pl.when(n_whole > 0)
        def _():
            pltpu.make_async_copy(obuf.at[s, pl.ds(0, n_whole)], out_hbm.at[pl.ds(0, n_whole)], ssem.at[s]).wait()

        def wait_row(r, carry):
            pltpu.make_async_copy(obuf.at[s, pl.ds(0, 1)], out_hbm.at[pl.ds(0, 1)], ssem.at[s]).wait()
            return carry
        lax.fori_loop(0, n - n_whole, wait_row, 0)

    @pl.when(j == 0)
    def _():
        start_gather(0, 0)

    @pl.when(j + 1 < n_tiles)
    def _():
        start_gather(j + 1, 1 - slot)

    @pl.when(j < n_tiles)
    def _():
        pltpu.make_async_copy(x1_hbm.at[pl.ds(0, tile)], xbuf.at[slot], gsem.at[slot]).wait()
        x1 = xbuf[slot]
        h = _rms_rows(x1, gf_ref[...]).astype(BF16)
        y = x1
        for w_col, wg, wu, wd in ((ROUTE_W_LO, wg_lo, wu_lo, wd_lo), (ROUTE_W_HI, wg_hi, wu_hi, wd_hi)):
            a = jnp.dot(h, wg[0], preferred_element_type=F32)
            b = jnp.dot(h, wu[0], preferred_element_type=F32)
            act = (a * _sigmoid(a)) * b * w_ref[:, w_col:w_col + 1]
            y = y + jnp.dot(act.astype(BF16), wd[0], preferred_element_type=F32)
        obuf[slot] = y

        def scatter_row(r, carry):
            tok = src_ref[j * tile + r]
            pltpu.make_async_copy(obuf.at[slot, pl.ds(r, 1)], out_hbm.at[pl.ds(tok, 1)], ssem.at[slot]).start()
            return carry
        lax.fori_loop(0, nvalid_ref[j], scatter_row, 0)

        @pl.when(j >= 1)
        def _():
            wait_scatter(j - 1, 1 - slot)

        @pl.when(j == n_tiles - 1)
        def _():
            wait_scatter(j, slot)


def _int_from_comparisons(v, n_bits):
    out = jnp.zeros(v.shape, I32)
    for b in range(n_bits):
        bit_set = jnp.floor(v / 2.0 ** b) - 2.0 * jnp.floor(v / 2.0 ** (b + 1)) >= 0.5
        out = out + jnp.where(bit_set, 1 << b, 0)
    return out


def _moe(x1, route, gf, wg, wu, wd, *, tile):
    s, d = x1.shape
    ff = wg.shape[2]
    n_buckets = N_GROUPS * PAIRS_PER_GROUP
    max_tiles = (s + n_buckets * (tile - 1)) // tile
    bucket = _int_from_comparisons(route[:, ROUTE_BUCKET], 5)
    rank = _int_from_comparisons(route[:, ROUTE_RANK], 14)
    counts = jnp.zeros((n_buckets,), I32).at[bucket].add(1)
    padded = (counts + tile - 1) // tile * tile
    ends = jnp.cumsum(padded)
    starts = ends - padded
    dest = starts[bucket] + rank
    src = jnp.zeros((max_tiles * tile,), I32).at[dest].set(jnp.arange(s, dtype=I32))
    w_sorted = jnp.zeros((max_tiles * tile, ROUTE_COLS), F32).at[dest].set(route[:, :ROUTE_COLS])
    n_tiles = ends[-1] // tile
    tile_start = jnp.arange(max_tiles, dtype=I32) * tile
    first_row = jnp.minimum(tile_start, ends[-1] - tile)
    tile_bucket = jnp.sum((ends[None, :] <= first_row[:, None]).astype(I32), axis=1)
    nvalid = jnp.where(tile_start < ends[-1],
                       jnp.clip(counts[tile_bucket] - (tile_start - starts[tile_bucket]), 0, tile), 0).astype(I32)
    pair_lo = jnp.array([0, 0, 0, 1, 1, 2], I32)
    pair_hi = jnp.array([1, 2, 3, 2, 3, 3], I32)
    group = tile_bucket // PAIRS_PER_GROUP
    e_lo = group * EXPERTS_PER_GROUP + pair_lo[tile_bucket % PAIRS_PER_GROUP]
    e_hi = group * EXPERTS_PER_GROUP + pair_hi[tile_bucket % PAIRS_PER_GROUP]

    lo_w = lambda shape: pl.BlockSpec(shape, lambda j, src, elo, ehi, nv, nt: (elo[j], 0, 0))
    hi_w = lambda shape: pl.BlockSpec(shape, lambda j, src, elo, ehi, nv, nt: (ehi[j], 0, 0))
    grid_spec = pltpu.PrefetchScalarGridSpec(
        num_scalar_prefetch=5,
        grid=(max_tiles,),
        in_specs=[pl.BlockSpec(memory_space=pl.ANY),
                  pl.BlockSpec((tile, ROUTE_COLS), lambda j, *_: (j, 0)),
                  pl.BlockSpec((1, d), lambda j, *_: (0, 0)),
                  lo_w((1, d, ff)), lo_w((1, d, ff)), lo_w((1, ff, d)),
                  hi_w((1, d, ff)), hi_w((1, d, ff)), hi_w((1, ff, d))],
        out_specs=pl.BlockSpec(memory_space=pl.ANY),
        scratch_shapes=[pltpu.VMEM((2, tile, d), F32), pltpu.VMEM((2, tile, d), F32),
                        pltpu.SemaphoreType.DMA((2,)), pltpu.SemaphoreType.DMA((2,))],
    )
    return pl.pallas_call(
        functools.partial(_moe_kernel, tile=tile),
        grid_spec=grid_spec,
        out_shape=jax.ShapeDtypeStruct((s, d), F32),
        compiler_params=_cparams(("arbitrary",)),
        name="moe_experts",
    )(src, e_lo, e_hi, nvalid, n_tiles.reshape(1), x1, w_sorted, gf, wg, wu, wd, wg, wu, wd)


def _rope_inv_freq():
    def inv(dim):
        return ROPE_THETA ** (-jnp.arange(0, dim, 2, dtype=F32) / dim)
    return jnp.stack([jnp.tile(inv(HEAD_DIM), LANES // (HEAD_DIM // 2)),
                      jnp.tile(inv(IDX_DIM), LANES // (IDX_DIM // 2))])


def _layer(x, mem, pos, g_mix, w_in_all, layer, b_gate, w_pool_grp, pool_scale, q_norm_g, k_norm_g, g_mem, w_mem_kv,
           xq_norm_g, xk_norm_g, w_pool_out, w_attn_out, w_cross_out, w_o, g_ffn, w_router_group,
           w_router_expert, w_e_gate, w_e_up, w_e_down):
    s, d = x.shape
    row2 = lambda v: v.reshape(1, -1)
    k_sel = min(TOPK_MAX, s // 4)

    main, h = _main_proj(x, row2(g_mix), w_in_all, layer, tm=1024)
    n_gate_cols = -(-(GATE_SHIFT + N_BRANCH * d) // GATE_TN) * GATE_TN
    bias_window = jnp.pad(row2(b_gate), ((0, 0), (GATE_SHIFT, n_gate_cols - GATE_SHIFT - N_BRANCH * d)))
    gates = _gate_proj(h, w_in_all, layer, bias_window, tm=1024)

    q, k, v, xq, qi, ki, wib = _prep(main, pos.reshape(s, 1), _rope_inv_freq(), row2(q_norm_g),
                                     row2(k_norm_g), row2(xq_norm_g), tm=512)

    bd = jax.scipy.linalg.block_diag(*[w_pool_grp[g] for g in range(len(POOL_WINDOWS))])
    pool_pre = _pool(main, bd.astype(BF16), row2(pool_scale), tm=512)

    k_m, v_m = _memkv(mem, row2(g_mem), w_mem_kv.astype(BF16), row2(xk_norm_g))
    cross = _cross(xq, k_m, v_m, tm=512)

    attn = _dsa(qi, wib, q, ki, k, v, tq=256, kc=512, ka=1024, k_sel=k_sel)

    w_router = jnp.pad(jnp.concatenate([w_router_group, w_router_expert], axis=1),
                       ((0, 0), (0, LANES - N_GROUPS - N_EXPERTS)))
    x1, route = _merge(x, pool_pre, attn, cross, gates, w_pool_out.astype(BF16), w_attn_out.astype(BF16),
                       w_cross_out.astype(BF16), w_o.astype(BF16), row2(g_ffn), w_router.astype(BF16), tm=256)
    return _moe(x1, route, row2(g_ffn), w_e_gate.astype(BF16), w_e_up.astype(BF16), w_e_down.astype(BF16),
                tile=128)


def kernel(x, mem, positions, g_mix, w_in, b_gate, w_pool_grp, pool_scale, q_norm_g, k_norm_g, g_mem, w_mem_kv,
           xq_norm_g, xk_norm_g, w_pool_out, w_attn_out, w_cross_out, w_o, g_ffn, w_router_group,
           w_router_expert, w_e_gate, w_e_up, w_e_down):
    depth = g_mix.shape[0]
    w_in_t = jnp.swapaxes(w_in, 1, 2)
    outs = []
    for b in range(x.shape[0]):
        xb = x[b]
        for l in range(depth):
            xb = _layer(xb, mem[b], positions[b], g_mix[l], w_in_t, l, b_gate[l], w_pool_grp[l], pool_scale[l],
                        q_norm_g[l], k_norm_g[l], g_mem[l], w_mem_kv[l], xq_norm_g[l], xk_norm_g[l],
                        w_pool_out[l], w_attn_out[l], w_cross_out[l], w_o[l], g_ffn[l], w_router_group[l],
                        w_router_expert[l], w_e_gate[l], w_e_up[l], w_e_down[l])
        outs.append(xb)
    return jnp.stack(outs)
```

```python
import functools
import math

import jax
import jax.numpy as jnp
from jax import lax
from jax.experimental import pallas as pl
from jax.experimental.pallas import tpu as pltpu

F32 = jnp.float32
BF16 = jnp.bfloat16
I32 = jnp.int32
I16 = jnp.int16
I16_MIN = -32768

NORM_EPS = 1e-6
ROPE_THETA = 10000.0
HEAD_DIM = 128
ATTN_HEADS = 6
ATTN_W = ATTN_HEADS * HEAD_DIM
IDX_HEADS = 4
IDX_DIM = 64
XATTN_HEADS = 4
XATTN_W = XATTN_HEADS * HEAD_DIM
POOL_WINDOWS = (2, 4, 8, 16)
POOL_GROUP_W = 192
POOL_W = len(POOL_WINDOWS) * POOL_GROUP_W
POOL_HALO = 16
TOPK_MAX = 256
N_GROUPS = 4
EXPERTS_PER_GROUP = 4
N_EXPERTS = N_GROUPS * EXPERTS_PER_GROUP
N_BRANCH = 3
PAIRS_PER_GROUP = EXPERTS_PER_GROUP * (EXPERTS_PER_GROUP - 1) // 2
ROUTE_BUCKET, ROUTE_RANK, ROUTE_W_LO, ROUTE_W_HI, ROUTE_COLS = 0, 1, 2, 3, 4

LANES = 128
SUBLANES = 8
V7X_VMEM_LIMIT_BYTES = 56 * 1024 * 1024

COL_POOL = 0
COL_Q = COL_POOL + POOL_W
COL_K = COL_Q + ATTN_W
COL_V = COL_K + ATTN_W
COL_QI = COL_V + ATTN_W
COL_KI = COL_QI + IDX_HEADS * IDX_DIM
XQ_SHIFT = IDX_DIM + IDX_HEADS
GATE_COL0 = COL_KI + XQ_SHIFT + XATTN_W
MAIN_TN = 512
MAIN_W = -(-GATE_COL0 // MAIN_TN) * MAIN_TN
GATE_TN = 1280
GATE_WINDOW0 = GATE_COL0 // GATE_TN * GATE_TN
GATE_SHIFT = GATE_COL0 - GATE_WINDOW0
assert COL_QI % LANES == 0 and COL_KI % LANES == 0 and GATE_SHIFT < LANES
IDX_PITCH = IDX_HEADS * LANES + LANES

NEG_BIG = -1e30
INT_MIN = -2147483648
KEY_NEG_INF = -2139095041
SOFTMAX_SUM_MIN = 2.0 ** -64
SOFTMAX_SUM_MAX = 2.0 ** 100
NT_DIMS = (((1,), (1,)), ((), ()))


def _cparams(semantics):
    return pltpu.CompilerParams(dimension_semantics=semantics, vmem_limit_bytes=V7X_VMEM_LIMIT_BYTES)


def _load_once(hbm_refs, vmem_refs, sems):
    @pl.when(pl.program_id(0) == 0)
    def _():
        copies = [pltpu.make_async_copy(src, dst, sems.at[n])
                  for n, (src, dst) in enumerate(zip(hbm_refs, vmem_refs))]
        for c in copies:
            c.start()
        for c in copies:
            c.wait()


def _hbm_spec():
    return pl.BlockSpec(memory_space=pl.ANY)


def _rms_rows(x, g):
    ms = jnp.mean(x * x, axis=-1, keepdims=True)
    return x * lax.rsqrt(ms + NORM_EPS) * g


def _sigmoid(x):
    return 1.0 / (1.0 + jnp.exp(-x))


def _main_proj_kernel(x_ref, g_ref, w_ref, o_ref, h_ref):
    @pl.when(pl.program_id(1) == 0)
    def _():
        h_ref[...] = _rms_rows(x_ref[...], g_ref[...]).astype(BF16)

    o_ref[...] = lax.dot_general(h_ref[...], w_ref[...].astype(BF16), NT_DIMS, preferred_element_type=F32)


def _main_proj(x, g, w_in_all, layer, *, tm):
    s, d = x.shape
    return pl.pallas_call(
        _main_proj_kernel,
        grid=(s // tm, MAIN_W // MAIN_TN),
        in_specs=[pl.BlockSpec((tm, d), lambda i, j: (i, 0)),
                  pl.BlockSpec((1, d), lambda i, j: (0, 0)),
                  pl.BlockSpec((None, MAIN_TN, d), lambda i, j: (layer, j, 0))],
        out_specs=[pl.BlockSpec((tm, MAIN_TN), lambda i, j: (i, j)),
                   pl.BlockSpec((tm, d), lambda i, j: (i, 0))],
        out_shape=[jax.ShapeDtypeStruct((s, MAIN_W), F32), jax.ShapeDtypeStruct((s, d), BF16)],
        compiler_params=_cparams(("parallel", "arbitrary")),
        name="norm_proj_main",
    )(x, g, w_in_all)


def _gate_proj_kernel(h_ref, w_ref, b_ref, o_ref, wb_ref, *, n_cols):
    @pl.when(pl.program_id(1) == 0)
    def _():
        col = GATE_WINDOW0 + pl.program_id(0) * GATE_TN + lax.broadcasted_iota(I32, (GATE_TN, 1), 0)
        wb_ref[...] = jnp.where(col < n_cols, w_ref[...], 0.0).astype(BF16)

    acc = lax.dot_general(h_ref[...], wb_ref[...], NT_DIMS, preferred_element_type=F32)
    o_ref[...] = _sigmoid(acc + b_ref[...]).astype(BF16)


def _gate_proj(h, w_in_all, layer, bias_window, *, tm):
    s, d = h.shape
    n = bias_window.shape[1]
    first = GATE_WINDOW0 // GATE_TN
    return pl.pallas_call(
        functools.partial(_gate_proj_kernel, n_cols=w_in_all.shape[1]),
        grid=(n // GATE_TN, s // tm),
        in_specs=[pl.BlockSpec((tm, d), lambda j, i: (i, 0)),
                  pl.BlockSpec((None, GATE_TN, d), lambda j, i: (layer, first + j, 0)),
                  pl.BlockSpec((1, GATE_TN), lambda j, i: (0, j))],
        out_specs=pl.BlockSpec((tm, GATE_TN), lambda j, i: (i, j)),
        out_shape=jax.ShapeDtypeStruct((s, n), BF16),
        scratch_shapes=[pltpu.VMEM((GATE_TN, d), BF16)],
        compiler_params=_cparams(("parallel", "arbitrary")),
        name="gate_proj",
    )(h, w_in_all, bias_window)


def _prep_kernel(f_ref, pos_ref, invf_ref, qg_ref, kg_ref, xg_ref,
                 q_ref, k_ref, v_ref, xq_ref, qi_ref, ki_ref, wib_ref):
    tm = f_ref.shape[0]
    pos = pos_ref[...].astype(F32)
    lane = lax.broadcasted_iota(I32, (1, LANES), 1)
    ang = pos * invf_ref[0:1, :]
    cos_a = jnp.cos(ang)
    sin_a = jnp.where(lane < HEAD_DIM // 2, -jnp.sin(ang), jnp.sin(ang))
    ang_i = pos * invf_ref[1:2, :]
    low_i = (lane & (IDX_DIM - 1)) < IDX_DIM // 2
    cos_i = jnp.cos(ang_i)
    sin_i = jnp.where(low_i, -jnp.sin(ang_i), jnp.sin(ang_i))

    def rope_head(y):
        return y * cos_a + pltpu.roll(y, HEAD_DIM // 2, 1) * sin_a

    def rope_idx(y):
        partner = jnp.where(low_i, pltpu.roll(y, LANES - IDX_DIM // 2, 1), pltpu.roll(y, IDX_DIM // 2, 1))
        return y * cos_i + partner * sin_i

    for h in range(ATTN_HEADS):
        sl = slice(h * HEAD_DIM, (h + 1) * HEAD_DIM)
        qh = f_ref[:, COL_Q + h * HEAD_DIM:COL_Q + (h + 1) * HEAD_DIM]
        kh = f_ref[:, COL_K + h * HEAD_DIM:COL_K + (h + 1) * HEAD_DIM]
        q_ref[:, sl] = rope_head(_rms_rows(qh, qg_ref[...])).astype(BF16)
        k_ref[:, sl] = rope_head(_rms_rows(kh, kg_ref[...])).astype(BF16)
    v_ref[...] = f_ref[:, COL_V:COL_V + ATTN_W].astype(BF16)
    xq_all = f_ref[:, COL_KI:COL_KI + XATTN_W + LANES][:, XQ_SHIFT:XQ_SHIFT + XATTN_W]
    for h in range(XATTN_HEADS):
        sl = slice(h * HEAD_DIM, (h + 1) * HEAD_DIM)
        xq_ref[:, sl] = _rms_rows(xq_all[:, sl], xg_ref[...]).astype(BF16)
    first_half = lane < IDX_DIM
    for pair in range(IDX_HEADS // 2):
        both = rope_idx(f_ref[:, COL_QI + pair * LANES:COL_QI + (pair + 1) * LANES])
        qi_ref[:, (2 * pair) * LANES:(2 * pair + 1) * LANES] = jnp.where(first_half, both, 0.0).astype(BF16)
        qi_ref[:, (2 * pair + 1) * LANES:(2 * pair + 2) * LANES] = jnp.where(first_half, 0.0, both).astype(BF16)
    ki_tile = f_ref[:, COL_KI:COL_KI + LANES]
    ki = jnp.where(first_half, rope_idx(ki_tile), 0.0)
    ki_ref[...] = (ki + pltpu.roll(ki, IDX_DIM, 1)).astype(BF16)
    wi = ki_tile * (IDX_HEADS ** -0.5) * (IDX_DIM ** -0.5)
    for h in range(IDX_HEADS):
        wib_ref[:, h * LANES:(h + 1) * LANES] = jnp.broadcast_to(wi[:, IDX_DIM + h:IDX_DIM + h + 1], (tm, LANES))
    qi_ref[:, IDX_HEADS * LANES:] = jnp.zeros((tm, LANES), BF16)
    wib_ref[:, IDX_HEADS * LANES:] = jnp.zeros((tm, LANES), F32)


def _prep(main, pos_col, invf, qg, kg, xg, *, tm):
    s = main.shape[0]
    row = lambda w: pl.BlockSpec((tm, w), lambda i: (i, 0))
    small = lambda shape: pl.BlockSpec(shape, lambda i: (0, 0))
    outs = [(row(ATTN_W), (s, ATTN_W), BF16), (row(ATTN_W), (s, ATTN_W), BF16), (row(ATTN_W), (s, ATTN_W), BF16),
            (row(XATTN_W), (s, XATTN_W), BF16), (row(IDX_PITCH), (s, IDX_PITCH), BF16),
            (row(LANES), (s, LANES), BF16), (row(IDX_PITCH), (s, IDX_PITCH), F32)]
    return pl.pallas_call(
        _prep_kernel,
        grid=(s // tm,),
        in_specs=[row(MAIN_W), row(1), small((2, LANES)), small((1, LANES)), small((1, LANES)),
                  small((1, LANES))],
        out_specs=[spec for spec, _, _ in outs],
        out_shape=[jax.ShapeDtypeStruct(shape, dt) for _, shape, dt in outs],
        compiler_params=_cparams(("parallel",)),
        name="qknorm_rope",
    )(main, pos_col, invf, qg, kg, xg)


def _pool_kernel(u_ref, halo_ref, bd_ref, sc_ref, o_ref, buf_ref):
    i = pl.program_id(0)
    tm = u_ref.shape[0]
    u = u_ref[...]
    buf_ref[0:POOL_HALO, :] = jnp.where(i == 0, 0.0, halo_ref[...])
    buf_ref[POOL_HALO:POOL_HALO + tm, :] = u
    t = i * tm + lax.broadcasted_iota(I32, (tm, 1), 0)
    lane = lax.broadcasted_iota(I32, (1, POOL_W), 1)
    acc = u
    pooled = None
    for d in range(1, max(POOL_WINDOWS)):
        acc = acc + buf_ref[POOL_HALO - d:POOL_HALO - d + tm, :]
        if d + 1 in POOL_WINDOWS:
            g = POOL_WINDOWS.index(d + 1)
            mean = acc / jnp.minimum(t + 1, d + 1).astype(F32)
            in_group = (lane >= g * POOL_GROUP_W) & (lane < (g + 1) * POOL_GROUP_W)
            pooled = jnp.where(in_group, mean, 0.0 if pooled is None else pooled)
    p = (pooled - u).astype(BF16)
    y = jnp.dot(p, bd_ref[...], preferred_element_type=F32) * sc_ref[...]
    o_ref[...] = y.astype(BF16)


def _pool(front, bd, scale, *, tm):
    s = front.shape[0]
    halo_blocks = tm // POOL_HALO
    return pl.pallas_call(
        _pool_kernel,
        grid=(s // tm,),
        in_specs=[
            pl.BlockSpec((tm, POOL_W), lambda i: (i, 0)),
            pl.BlockSpec((POOL_HALO, POOL_W), lambda i: (jnp.maximum(i * halo_blocks - 1, 0), 0)),
            pl.BlockSpec((POOL_W, POOL_W), lambda i: (0, 0)),
            pl.BlockSpec((1, POOL_W), lambda i: (0, 0)),
        ],
        out_specs=pl.BlockSpec((tm, POOL_W), lambda i: (i, 0)),
        out_shape=jax.ShapeDtypeStruct((s, POOL_W), BF16),
        scratch_shapes=[pltpu.VMEM((tm + POOL_HALO, POOL_W), F32)],
        compiler_params=_cparams(("parallel",)),
        name="pool_mixer",
    )(front, front, bd, scale)


def _memkv_kernel(mem_ref, g_ref, w_ref, kg_ref, k_ref, v_ref):
    h = _rms_rows(mem_ref[...], g_ref[...]).astype(BF16)
    kv = jnp.dot(h, w_ref[...], preferred_element_type=F32)
    for hd in range(XATTN_HEADS):
        sl = slice(hd * HEAD_DIM, (hd + 1) * HEAD_DIM)
        k_ref[:, sl] = _rms_rows(kv[:, sl], kg_ref[...]).astype(BF16)
    v_ref[...] = kv[:, XATTN_W:].astype(BF16)


def _memkv(mem, g, w, kg):
    m = mem.shape[0]
    return pl.pallas_call(
        _memkv_kernel,
        out_shape=[jax.ShapeDtypeStruct((m, XATTN_W), BF16), jax.ShapeDtypeStruct((m, XATTN_W), BF16)],
        compiler_params=pltpu.CompilerParams(vmem_limit_bytes=V7X_VMEM_LIMIT_BYTES),
        name="mem_kv",
    )(mem, g, w, kg)


def _cross_kernel(xq_ref, k_ref, v_ref, o_ref):
    for h in range(XATTN_HEADS):
        sl = slice(h * HEAD_DIM, (h + 1) * HEAD_DIM)
        logits = lax.dot_general(xq_ref[:, sl], k_ref[:, sl], NT_DIMS,
                                 preferred_element_type=F32) * (HEAD_DIM ** -0.5)
        e = jnp.exp(logits - jnp.max(logits, axis=-1, keepdims=True))
        p = e / jnp.sum(e, axis=-1, keepdims=True)
        o_ref[:, sl] = jnp.dot(p.astype(BF16), v_ref[:, sl], preferred_element_type=F32).astype(BF16)


def _cross(xq, k_m, v_m, *, tm):
    s = xq.shape[0]
    m = k_m.shape[0]
    return pl.pallas_call(
        _cross_kernel,
        grid=(s // tm,),
        in_specs=[pl.BlockSpec((tm, XATTN_W), lambda i: (i, 0)),
                  pl.BlockSpec((m, XATTN_W), lambda i: (0, 0)),
                  pl.BlockSpec((m, XATTN_W), lambda i: (0, 0))],
        out_specs=pl.BlockSpec((tm, XATTN_W), lambda i: (i, 0)),
        out_shape=jax.ShapeDtypeStruct((s, XATTN_W), BF16),
        compiler_params=_cparams(("parallel",)),
        name="mem_cross_attn",
    )(xq, k_m, v_m)


def _key_to_f32(key):
    bits = key ^ ((key >> 31) & 0x7FFFFFFF)
    return lax.bitcast_convert_type(bits, F32)


def _dsa_kernel(qi_ref, wib_ref, q_ref, ki_hbm, k_hbm, v_hbm, tri_hbm, o_ref, sc_ref, hi_ref, lo_ref,
                ki_ref, k_ref, v_ref, tri_ref, load_sems, thr_ref, need_ref, *, tq, kc, ka, k_sel):
    _load_once((ki_hbm, k_hbm, v_hbm, tri_hbm), (ki_ref, k_ref, v_ref, tri_ref), load_sems)
    i = pl.program_id(0)
    n_chunks = ((i + 1) * tq + kc - 1) // kc
    t = i * tq + lax.broadcasted_iota(I32, (tq, 1), 0)
    slabs = kc // LANES

    def slab(c, j):
        return pl.ds(pl.multiple_of(c * kc, kc) + j * LANES, LANES)

    def score_chunk(c, carry):
        off = pl.multiple_of(c * ka, ka)
        ki_c = ki_ref[pl.ds(off, ka), :]
        dots = [lax.dot_general(qi_ref[:, h * LANES:(h + 1) * LANES], ki_c, NT_DIMS,
                                preferred_element_type=F32) for h in range(IDX_HEADS)]
        for j in range(ka // LANES):
            cols = pl.ds(off + j * LANES, LANES)
            s = jnp.zeros((tq, LANES), F32)
            for h in range(IDX_HEADS):
                s = s + jnp.maximum(dots[h][:, j * LANES:(j + 1) * LANES], 0.0) * wib_ref[:, h * LANES:(h + 1) * LANES]
            kpos = off + j * LANES + lax.broadcasted_iota(I32, (1, LANES), 1)
            s = jnp.where(kpos <= t, jnp.where(s == 0.0, 0.0, s), -jnp.inf)
            sc_ref[:, cols] = s
            bits = pltpu.bitcast(s, I32)
            key = bits ^ ((bits >> 31) & 0x7FFFFFFF)
            hi_ref[:, cols] = (key >> 16).astype(I16)
            lo_ref[:, cols] = ((key & 0xFFFF) + I16_MIN).astype(I16)
        return carry

    lax.fori_loop(0, (n_chunks * kc + ka - 1) // ka, score_chunk, 0)

    def count16(ref, cand16):
        def body(c, acc):
            for j in range(slabs):
                acc = acc + jnp.where(ref[:, slab(c, j)] >= cand16, jnp.int16(1), jnp.int16(0))
            return acc
        acc = lax.fori_loop(0, n_chunks, body, jnp.zeros((tq, LANES), I16))
        return jnp.sum(acc.astype(I32), axis=-1, keepdims=True)

    def search16(ref, counted_already):
        def step(b, best):
            cand = best + jnp.left_shift(jnp.int32(1), 15 - b)
            enough = count16(ref, cand.astype(I16)) + counted_already >= k_sel
            return jnp.where(enough, cand, best)
        return lax.fori_loop(0, 16, step, jnp.full((tq, LANES), I16_MIN, I32))

    hi_best = search16(hi_ref, 0)
    hi_best16 = hi_best.astype(I16)

    def split_chunk(c, acc):
        for j in range(slabs):
            h = hi_ref[:, slab(c, j)]
            lo_ref[:, slab(c, j)] = jnp.where(h == hi_best16, lo_ref[:, slab(c, j)], jnp.int16(I16_MIN))
            acc = acc + jnp.where(h > hi_best16, jnp.int16(1), jnp.int16(0))
        return acc

    above = lax.fori_loop(0, n_chunks, split_chunk, jnp.zeros((tq, LANES), I16))
    lo_best = search16(lo_ref, jnp.sum(above.astype(I32), axis=-1, keepdims=True))
    thr_key = jnp.left_shift(hi_best, 16) + (lo_best - I16_MIN)

    def count(pred):
        def body(c, acc):
            for j in range(slabs):
                acc = acc + pred(sc_ref[:, slab(c, j)]).astype(I32)
            return acc
        acc = lax.fori_loop(0, n_chunks, body, jnp.zeros((tq, LANES), I32))
        return jnp.sum(acc, axis=-1, keepdims=True)

    thr_found = _key_to_f32(thr_key)
    n_ge = count(lambda s: s >= thr_found)
    n_gt = count(lambda s: s > thr_found)
    thr_ref[...] = thr_found
    need_ref[...] = jnp.broadcast_to((k_sel - n_gt).astype(F32), (tq, LANES))
    n_wrong = jnp.sum(jnp.where((n_ge >= k_sel) & (n_gt < k_sel), 0.0, 1.0))

    @pl.when(n_wrong > 0.0)
    def _():
        def bisect(b, best):
            cand = best + jnp.left_shift(jnp.int32(1), 31 - b)
            cand_f = _key_to_f32(cand)
            enough = (count(lambda s: s >= cand_f) >= k_sel) | (cand < KEY_NEG_INF)
            return jnp.where(enough, cand, best)

        thr_slow = _key_to_f32(lax.fori_loop(0, 32, bisect, jnp.full((tq, LANES), INT_MIN, I32)))
        thr_ref[...] = thr_slow
        need_ref[...] = jnp.broadcast_to((k_sel - count(lambda s: s > thr_slow)).astype(F32), (tq, LANES))

    thr_col = thr_ref[:, 0:1]
    need = need_ref[:, 0:1]

    def mask_chunk(c, ties_before):
        off = pl.multiple_of(c * kc, kc)
        s = sc_ref[:, pl.ds(off, kc)]
        tie = s == thr_col
        prefix = jnp.dot(jnp.where(tie, 1.0, 0.0).astype(BF16), tri_ref[:, 0:kc], preferred_element_type=F32)
        kpos = off + lax.broadcasted_iota(I32, (1, kc), 1)
        take_tie = tie & (prefix + ties_before <= need)
        sel = ((s > thr_col) | take_tie) & (kpos <= t)
        sc_ref[:, pl.ds(off, kc)] = jnp.where(sel, 0.0, NEG_BIG)
        return ties_before + prefix[:, kc - 1:kc]

    lax.fori_loop(0, n_chunks, mask_chunk, jnp.zeros((tq, 1), F32))

    scale2 = (HEAD_DIM ** -0.5) * math.log2(math.e)
    heads = [slice(h * HEAD_DIM, (h + 1) * HEAD_DIM) for h in range(ATTN_HEADS)]

    assert ka in (kc, 2 * kc)
    if ka == 2 * kc:
        @pl.when(n_chunks % 2 == 1)
        def _():
            sc_ref[:, pl.ds(pl.multiple_of(n_chunks * kc, kc), kc)] = jnp.full((tq, kc), NEG_BIG, F32)

    def unshifted_chunk(c, carry):
        off = pl.multiple_of(c * ka, ka)
        bias = sc_ref[:, pl.ds(off, ka)]
        out = []
        for sl, (lane_sum, acc) in zip(heads, carry):
            p = jnp.exp2(lax.dot_general(q_ref[:, sl], k_ref[pl.ds(off, ka), sl], NT_DIMS,
                                         preferred_element_type=F32) * scale2 + bias)
            for j in range(ka // LANES):
                lane_sum = lane_sum + p[:, j * LANES:(j + 1) * LANES]
            acc = acc + jnp.dot(p.astype(BF16), v_ref[pl.ds(off, ka), sl], preferred_element_type=F32)
            out.append((lane_sum, acc))
        return tuple(out)

    zeros = jnp.zeros((tq, LANES), F32)
    result = lax.fori_loop(0, (n_chunks * kc + ka - 1) // ka, unshifted_chunk,
                           tuple((zeros, zeros) for _ in heads))
    in_range = None
    for sl, (lane_sum, acc) in zip(heads, result):
        row_sum = jnp.sum(lane_sum, axis=-1, keepdims=True)
        o_ref[:, sl] = (acc / row_sum).astype(BF16)
        ok = (row_sum >= SOFTMAX_SUM_MIN) & (row_sum <= SOFTMAX_SUM_MAX)
        in_range = ok if in_range is None else (in_range & ok)
    n_bad = jnp.sum(jnp.where(in_range, 0.0, 1.0))

    @pl.when(n_bad > 0.0)
    def _():
        _shifted_attention(q_ref, k_ref, v_ref, sc_ref, o_ref, n_chunks, tq=tq, kc=kc, scale2=scale2)


def _shifted_attention(q_ref, k_ref, v_ref, sc_ref, o_ref, n_chunks, *, tq, kc, scale2):
    slabs = kc // LANES
    for h in range(ATTN_HEADS):
        sl = slice(h * HEAD_DIM, (h + 1) * HEAD_DIM)

        def logits(c, sl=sl):
            off = pl.multiple_of(c * kc, kc)
            return lax.dot_general(q_ref[:, sl], k_ref[pl.ds(off, kc), sl], NT_DIMS,
                                   preferred_element_type=F32) * scale2 + sc_ref[:, pl.ds(off, kc)]

        def max_chunk(c, lane_max, logits=logits):
            s = logits(c)
            for j in range(slabs):
                lane_max = jnp.maximum(lane_max, s[:, j * LANES:(j + 1) * LANES])
            return lane_max

        lane_max = lax.fori_loop(0, n_chunks, max_chunk, jnp.full((tq, LANES), NEG_BIG, F32))
        row_max = jnp.max(lane_max, axis=-1, keepdims=True)

        def pv_chunk(c, carry, sl=sl, row_max=row_max, logits=logits):
            lane_sum, acc = carry
            off = pl.multiple_of(c * kc, kc)
            p = jnp.exp2(logits(c) - row_max)
            for j in range(slabs):
                lane_sum = lane_sum + p[:, j * LANES:(j + 1) * LANES]
            acc = acc + jnp.dot(p.astype(BF16), v_ref[pl.ds(off, kc), sl], preferred_element_type=F32)
            return lane_sum, acc

        lane_sum, acc = lax.fori_loop(0, n_chunks, pv_chunk,
                                      (jnp.zeros((tq, LANES), F32), jnp.zeros((tq, HEAD_DIM), F32)))
        o_ref[:, sl] = (acc / jnp.sum(lane_sum, axis=-1, keepdims=True)).astype(BF16)


def _dsa(qi, wib, q, ki, k, v, *, tq, kc, ka, k_sel):
    s = q.shape[0]
    tri = jnp.pad(jnp.triu(jnp.ones((kc, kc), BF16)), ((0, 0), (0, LANES)))
    row = lambda w: pl.BlockSpec((tq, w), lambda i: (i, 0))
    return pl.pallas_call(
        functools.partial(_dsa_kernel, tq=tq, kc=kc, ka=ka, k_sel=k_sel),
        grid=(s // tq,),
        in_specs=[row(IDX_PITCH), row(IDX_PITCH), row(ATTN_W),
                  _hbm_spec(), _hbm_spec(), _hbm_spec(), _hbm_spec()],
        out_specs=row(ATTN_W),
        out_shape=jax.ShapeDtypeStruct((s, ATTN_W), BF16),
        scratch_shapes=[pltpu.VMEM((tq, s + LANES), F32), pltpu.VMEM((tq, s + LANES), I16),
                        pltpu.VMEM((tq, s + LANES), I16),
                        pltpu.VMEM(ki.shape, BF16), pltpu.VMEM(k.shape, BF16), pltpu.VMEM(v.shape, BF16),
                        pltpu.VMEM(tri.shape, BF16), pltpu.SemaphoreType.DMA((4,)),
                        pltpu.VMEM((tq, LANES), F32), pltpu.VMEM((tq, LANES), F32)],
        compiler_params=_cparams(("arbitrary",)),
        name="dsa_attention",
    )(qi, wib, q, ki, k, v, tri)


def _route(logits):
    lane = lax.broadcasted_iota(I32, (1, LANES), 1).astype(F32)
    far = float(LANES)
    is_group = lane < N_GROUPS
    g_max = jnp.max(jnp.where(is_group, logits, -jnp.inf), axis=-1, keepdims=True)
    g_top = jnp.min(jnp.where(is_group & (logits == g_max), lane, far), axis=-1, keepdims=True)
    pg_top = 1.0 / jnp.sum(jnp.where(is_group, jnp.exp(logits - g_max), 0.0), axis=-1, keepdims=True)
    first = N_GROUPS + EXPERTS_PER_GROUP * g_top
    in_grp = (lane >= first) & (lane < first + EXPERTS_PER_GROUP)
    e_max = jnp.max(jnp.where(in_grp, logits, -jnp.inf), axis=-1, keepdims=True)
    e_exp = jnp.where(in_grp, jnp.exp(logits - e_max), 0.0)
    pe = jnp.where(in_grp, e_exp / jnp.sum(e_exp, axis=-1, keepdims=True), -1.0)
    p1 = jnp.max(pe, axis=-1, keepdims=True)
    e1 = jnp.min(jnp.where(pe == p1, lane, far), axis=-1, keepdims=True)
    pe_rest = jnp.where(lane == e1, -1.0, pe)
    p2 = jnp.max(pe_rest, axis=-1, keepdims=True)
    e2 = jnp.min(jnp.where(pe_rest == p2, lane, far), axis=-1, keepdims=True)
    den = p1 + p2
    w1 = pg_top * p1 / den
    w2 = pg_top * p2 / den
    lo = jnp.minimum(e1, e2) - first
    hi = jnp.maximum(e1, e2) - first
    pair = lo * (2 * EXPERTS_PER_GROUP - 1 - lo) * 0.5 + (hi - lo - 1.0)
    first_is_lo = e1 < e2
    return (g_top * PAIRS_PER_GROUP + pair, jnp.where(first_is_lo, w1, w2), jnp.where(first_is_lo, w2, w1))


def _merge_kernel(x_ref, p_ref, a_ref, c_ref, gt_ref, gf_ref, wpo_hbm, wao_hbm, wco_hbm, wo_hbm, wr_hbm, tri_hbm,
                  x1_ref, route_ref, cnt_ref, wpo_ref, wao_ref, wco_ref, wo_ref, wr_ref, tri_ref, load_sems):
    _load_once((wpo_hbm, wao_hbm, wco_hbm, wo_hbm, wr_hbm, tri_hbm),
               (wpo_ref, wao_ref, wco_ref, wo_ref, wr_ref, tri_ref), load_sems)

    @pl.when(pl.program_id(0) == 0)
    def _():
        cnt_ref[...] = jnp.zeros_like(cnt_ref)

    d = x_ref.shape[1]
    def gate(b):
        window = gt_ref[:, b * d:(b + 1) * d + LANES].astype(F32)
        return window[:, GATE_SHIFT:GATE_SHIFT + d]

    merged = gate(0) * jnp.dot(p_ref[...], wpo_ref[...], preferred_element_type=F32)
    merged += gate(1) * jnp.dot(a_ref[...], wao_ref[...], preferred_element_type=F32)
    merged += gate(2) * jnp.dot(c_ref[...], wco_ref[...], preferred_element_type=F32)
    x1 = x_ref[...] + jnp.dot(merged.astype(BF16), wo_ref[...], preferred_element_type=F32)
    x1_ref[...] = x1
    h2 = _rms_rows(x1, gf_ref[...]).astype(BF16)
    bucket, w_lo, w_hi = _route(jnp.dot(h2, wr_ref[...], preferred_element_type=F32))
    lane = lax.broadcasted_iota(I32, (1, LANES), 1).astype(F32)
    onehot = lane == bucket
    before = jnp.dot(tri_ref[...], jnp.where(onehot, 1.0, 0.0).astype(BF16), preferred_element_type=F32)
    rank = jnp.sum(jnp.where(onehot, before + cnt_ref[...], 0.0), axis=-1, keepdims=True)
    cnt_ref[...] += jnp.sum(jnp.where(onehot, 1.0, 0.0), axis=0, keepdims=True)
    route_ref[...] = jnp.where(lane == ROUTE_BUCKET, bucket,
                               jnp.where(lane == ROUTE_RANK, rank,
                                         jnp.where(lane == ROUTE_W_LO, w_lo,
                                                   jnp.where(lane == ROUTE_W_HI, w_hi, 0.0))))


def _merge(x, p, a, c, gates, wpo, wao, wco, wo, gf, wr, *, tm):
    s, d = x.shape
    row = lambda w: pl.BlockSpec((tm, w), lambda i: (i, 0))
    tri = jnp.tril(jnp.ones((tm, tm), BF16), -1)
    weights = (wpo, wao, wco, wo, wr, tri)
    return pl.pallas_call(
        _merge_kernel,
        grid=(s // tm,),
        in_specs=[row(d), row(POOL_W), row(ATTN_W), row(XATTN_W), row(gates.shape[1]),
                  pl.BlockSpec(gf.shape, lambda i: (0, 0))] + [_hbm_spec()] * len(weights),
        out_specs=[row(d), row(LANES)],
        out_shape=[jax.ShapeDtypeStruct((s, d), F32), jax.ShapeDtypeStruct((s, LANES), F32)],
        scratch_shapes=[pltpu.VMEM((1, LANES), F32)] + [pltpu.VMEM(w.shape, BF16) for w in weights]
                       + [pltpu.SemaphoreType.DMA((len(weights),))],
        compiler_params=_cparams(("arbitrary",)),
        name="gated_merge_router",
    )(x, p, a, c, gates, gf, *weights)


def _moe_kernel(src_ref, elo_ref, ehi_ref, nvalid_ref, ntiles_ref,
                x1_hbm, w_ref, gf_ref, wg_lo, wu_lo, wd_lo, wg_hi, wu_hi, wd_hi,
                out_hbm, xbuf, obuf, gsem, ssem, *, tile):
    j = pl.program_id(0)
    n_tiles = ntiles_ref[0]
    slot = lax.rem(j, 2)

    def start_gather(t, s):
        def body(r, carry):
            tok = src_ref[t * tile + r]
            pltpu.make_async_copy(x1_hbm.at[pl.ds(tok, 1)], xbuf.at[s, pl.ds(r, 1)], gsem.at[s]).start()
            return carry
        lax.fori_loop(0, tile, body, 0, unroll=8)

    def wait_scatter(t, s):
        n = nvalid_ref[t]
        n_whole = pl.multiple_of(lax.shift_left(lax.shift_right_logical(n, 3), 3), SUBLANES)

        @pl.when(n_whole > 0)
        def _():
            pltpu.make_async_copy(obuf.at[s, pl.ds(0, n_whole)], out_hbm.at[pl.ds(0, n_whole)], ssem.at[s]).wait()

        def wait_row(r, carry):
            pltpu.make_async_copy(obuf.at[s, pl.ds(0, 1)], out_hbm.at[pl.ds(0, 1)], ssem.at[s]).wait()
            return carry
        lax.fori_loop(0, n - n_whole, wait_row, 0)

    @pl.when(j == 0)
    def _():
        start_gather(0, 0)

    @pl.when(j + 1 < n_tiles)
    def _():
        start_gather(j + 1, 1 - slot)

    @pl.when(j < n_tiles)
    def _():
        pltpu.make_async_copy(x1_hbm.at[pl.ds(0, tile)], xbuf.at[slot], gsem.at[slot]).wait()
        x1 = xbuf[slot]
        h = _rms_rows(x1, gf_ref[...]).astype(BF16)
        y = x1
        for w_col, wg, wu, wd in ((ROUTE_W_LO, wg_lo, wu_lo, wd_lo), (ROUTE_W_HI, wg_hi, wu_hi, wd_hi)):
            a = jnp.dot(h, wg[0], preferred_element_type=F32)
            b = jnp.dot(h, wu[0], preferred_element_type=F32)
            act = (a * _sigmoid(a)) * b * w_ref[:, w_col:w_col + 1]
            y = y + jnp.dot(act.astype(BF16), wd[0], preferred_element_type=F32)
        obuf[slot] = y

        def scatter_row(r, carry):
            tok = src_ref[j * tile + r]
            pltpu.make_async_copy(obuf.at[slot, pl.ds(r, 1)], out_hbm.at[pl.ds(tok, 1)], ssem.at[slot]).start()
            return carry
        lax.fori_loop(0, nvalid_ref[j], scatter_row, 0)

        @pl.when(j >= 1)
        def _():
            wait_scatter(j - 1, 1 - slot)

        @pl.when(j == n_tiles - 1)
        def _():
            wait_scatter(j, slot)


def _int_from_comparisons(v, n_bits):
    out = jnp.zeros(v.shape, I32)
    for b in range(n_bits):
        bit_set = jnp.floor(v / 2.0 ** b) - 2.0 * jnp.floor(v / 2.0 ** (b + 1)) >= 0.5
        out = out + jnp.where(bit_set, 1 << b, 0)
    return out


def _moe(x1, route, gf, wg, wu, wd, *, tile):
    s, d = x1.shape
    ff = wg.shape[2]
    n_buckets = N_GROUPS * PAIRS_PER_GROUP
    max_tiles = (s + n_buckets * (tile - 1)) // tile
    bucket = _int_from_comparisons(route[:, ROUTE_BUCKET], 5)
    rank = _int_from_comparisons(route[:, ROUTE_RANK], 14)
    counts = jnp.zeros((n_buckets,), I32).at[bucket].add(1)
    padded = (counts + tile - 1) // tile * tile
    ends = jnp.cumsum(padded)
    starts = ends - padded
    dest = starts[bucket] + rank
    src = jnp.zeros((max_tiles * tile,), I32).at[dest].set(jnp.arange(s, dtype=I32))
    w_sorted = jnp.zeros((max_tiles * tile, ROUTE_COLS), F32).at[dest].set(route[:, :ROUTE_COLS])
    n_tiles = ends[-1] // tile
    tile_start = jnp.arange(max_tiles, dtype=I32) * tile
    first_row = jnp.minimum(tile_start, ends[-1] - tile)
    tile_bucket = jnp.sum((ends[None, :] <= first_row[:, None]).astype(I32), axis=1)
    nvalid = jnp.where(tile_start < ends[-1],
                       jnp.clip(counts[tile_bucket] - (tile_start - starts[tile_bucket]), 0, tile), 0).astype(I32)
    pair_lo = jnp.array([0, 0, 0, 1, 1, 2], I32)
    pair_hi = jnp.array([1, 2, 3, 2, 3, 3], I32)
    group = tile_bucket // PAIRS_PER_GROUP
    e_lo = group * EXPERTS_PER_GROUP + pair_lo[tile_bucket % PAIRS_PER_GROUP]
    e_hi = group * EXPERTS_PER_GROUP + pair_hi[tile_bucket % PAIRS_PER_GROUP]

    lo_w = lambda shape: pl.BlockSpec(shape, lambda j, src, elo, ehi, nv, nt: (elo[j], 0, 0))
    hi_w = lambda shape: pl.BlockSpec(shape, lambda j, src, elo, ehi, nv, nt: (ehi[j], 0, 0))
    grid_spec = pltpu.PrefetchScalarGridSpec(
        num_scalar_prefetch=5,
        grid=(max_tiles,),
        in_specs=[pl.BlockSpec(memory_space=pl.ANY),
                  pl.BlockSpec((tile, ROUTE_COLS), lambda j, *_: (j, 0)),
                  pl.BlockSpec((1, d), lambda j, *_: (0, 0)),
                  lo_w((1, d, ff)), lo_w((1, d, ff)), lo_w((1, ff, d)),
                  hi_w((1, d, ff)), hi_w((1, d, ff)), hi_w((1, ff, d))],
        out_specs=pl.BlockSpec(memory_space=pl.ANY),
        scratch_shapes=[pltpu.VMEM((2, tile, d), F32), pltpu.VMEM((2, tile, d), F32),
                        pltpu.SemaphoreType.DMA((2,)), pltpu.SemaphoreType.DMA((2,))],
    )
    return pl.pallas_call(
        functools.partial(_moe_kernel, tile=tile),
        grid_spec=grid_spec,
        out_shape=jax.ShapeDtypeStruct((s, d), F32),
        compiler_params=_cparams(("arbitrary",)),
        name="moe_experts",
    )(src, e_lo, e_hi, nvalid, n_tiles.reshape(1), x1, w_sorted, gf, wg, wu, wd, wg, wu, wd)


def _rope_inv_freq():
    def inv(dim):
        return ROPE_THETA ** (-jnp.arange(0, dim, 2, dtype=F32) / dim)
    return jnp.stack([jnp.tile(inv(HEAD_DIM), LANES // (HEAD_DIM // 2)),
                      jnp.tile(inv(IDX_DIM), LANES // (IDX_DIM // 2))])


def _layer(x, mem, pos, g_mix, w_in_all, layer, b_gate, w_pool_grp, pool_scale, q_norm_g, k_norm_g, g_mem, w_mem_kv,
           xq_norm_g, xk_norm_g, w_pool_out, w_attn_out, w_cross_out, w_o, g_ffn, w_router_group,
           w_router_expert, w_e_gate, w_e_up, w_e_down):
    s, d = x.shape
    row2 = lambda v: v.reshape(1, -1)
    k_sel = min(TOPK_MAX, s // 4)

    main, h = _main_proj(x, row2(g_mix), w_in_all, layer, tm=1024)
    n_gate_cols = -(-(GATE_SHIFT + N_BRANCH * d) // GATE_TN) * GATE_TN
    bias_window = jnp.pad(row2(b_gate), ((0, 0), (GATE_SHIFT, n_gate_cols - GATE_SHIFT - N_BRANCH * d)))
    gates = _gate_proj(h, w_in_all, layer, bias_window, tm=1024)

    q, k, v, xq, qi, ki, wib = _prep(main, pos.reshape(s, 1), _rope_inv_freq(), row2(q_norm_g),
                                     row2(k_norm_g), row2(xq_norm_g), tm=512)

    bd = jax.scipy.linalg.block_diag(*[w_pool_grp[g] for g in range(len(POOL_WINDOWS))])
    pool_pre = _pool(main, bd.astype(BF16), row2(pool_scale), tm=512)

    k_m, v_m = _memkv(mem, row2(g_mem), w_mem_kv.astype(BF16), row2(xk_norm_g))
    cross = _cross(xq, k_m, v_m, tm=512)

    attn = _dsa(qi, wib, q, ki, k, v, tq=256, kc=512, ka=1024, k_sel=k_sel)

    w_router = jnp.pad(jnp.concatenate([w_router_group, w_router_expert], axis=1),
                       ((0, 0), (0, LANES - N_GROUPS - N_EXPERTS)))
    x1, route = _merge(x, pool_pre, attn, cross, gates, w_pool_out.astype(BF16), w_attn_out.astype(BF16),
                       w_cross_out.astype(BF16), w_o.astype(BF16), row2(g_ffn), w_router.astype(BF16), tm=256)
    return _moe(x1, route, row2(g_ffn), w_e_gate.astype(BF16), w_e_up.astype(BF16), w_e_down.astype(BF16),
                tile=192)


def kernel(x, mem, positions, g_mix, w_in, b_gate, w_pool_grp, pool_scale, q_norm_g, k_norm_g, g_mem, w_mem_kv,
           xq_norm_g, xk_norm_g, w_pool_out, w_attn_out, w_cross_out, w_o, g_ffn, w_router_group,
           w_router_expert, w_e_gate, w_e_up, w_e_down):
    depth = g_mix.shape[0]
    w_in_t = jnp.swapaxes(w_in, 1, 2)
    outs = []
    for b in range(x.shape[0]):
        xb = x[b]
        for l in range(depth):
            xb = _layer(xb, mem[b], positions[b], g_mix[l], w_in_t, l, b_gate[l], w_pool_grp[l], pool_scale[l],
                        q_norm_g[l], k_norm_g[l], g_mem[l], w_mem_kv[l], xq_norm_g[l], xk_norm_g[l],
                        w_pool_out[l], w_attn_out[l], w_cross_out[l], w_o[l], g_ffn[l], w_router_group[l],
                        w_router_expert[l], w_e_gate[l], w_e_up[l], w_e_down[l])
        outs.append(xb)
    return jnp.stack(outs)
```

```python
import functools
import math
from typing import NamedTuple

import jax
import jax.numpy as jnp
from jax import lax
from jax.experimental import pallas as pl
from jax.experimental.pallas import tpu as pltpu

F32 = jnp.float32
BF16 = jnp.bfloat16
I32 = jnp.int32
I16 = jnp.int16
I16_MIN = -32768

NORM_EPS = 1e-6
ROPE_THETA = 10000.0
HEAD_DIM = 128
ATTN_HEADS = 6
ATTN_W = ATTN_HEADS * HEAD_DIM
IDX_HEADS = 4
IDX_DIM = 64
XATTN_HEADS = 4
XATTN_W = XATTN_HEADS * HEAD_DIM
POOL_WINDOWS = (2, 4, 8, 16)
POOL_GROUP_W = 192
POOL_W = len(POOL_WINDOWS) * POOL_GROUP_W
POOL_HALO = 16
TOPK_MAX = 256
N_GROUPS = 4
EXPERTS_PER_GROUP = 4
N_EXPERTS = N_GROUPS * EXPERTS_PER_GROUP
N_BRANCH = 3
PAIRS_PER_GROUP = EXPERTS_PER_GROUP * (EXPERTS_PER_GROUP - 1) // 2
ROUTE_BUCKET, ROUTE_RANK, ROUTE_W_LO, ROUTE_W_HI, ROUTE_COLS = 0, 1, 2, 3, 4

LANES = 128
SUBLANES = 8
V7X_VMEM_LIMIT_BYTES = 56 * 1024 * 1024

COL_POOL = 0
COL_Q = COL_POOL + POOL_W
COL_K = COL_Q + ATTN_W
COL_V = COL_K + ATTN_W
COL_QI = COL_V + ATTN_W
COL_KI = COL_QI + IDX_HEADS * IDX_DIM
XQ_SHIFT = IDX_DIM + IDX_HEADS
GATE_COL0 = COL_KI + XQ_SHIFT + XATTN_W
MAIN_TN = 512
MAIN_W = -(-GATE_COL0 // MAIN_TN) * MAIN_TN
GATE_TN = 1280
GATE_WINDOW0 = GATE_COL0 // GATE_TN * GATE_TN
GATE_SHIFT = GATE_COL0 - GATE_WINDOW0
assert COL_QI % LANES == 0 and COL_KI % LANES == 0 and GATE_SHIFT < LANES
IDX_PITCH = IDX_HEADS * LANES + LANES


class _Tiles(NamedTuple):
    main_rows: int = 1024
    gate_rows: int = 1024
    prep_rows: int = 512
    pool_rows: int = 512
    cross_rows: int = 512
    dsa_queries: int = 256
    dsa_key_chunk: int = 512
    dsa_key_step: int = 1024
    merge_rows: int = 256
    moe_rows: int = 192


TILES = _Tiles()

NEG_BIG = -1e30
INT_MIN = -2147483648
KEY_NEG_INF = -2139095041
SOFTMAX_SUM_MIN = 2.0 ** -64
SOFTMAX_SUM_MAX = 2.0 ** 100
NT_DIMS = (((1,), (1,)), ((), ()))


def _cparams(semantics):
    return pltpu.CompilerParams(dimension_semantics=semantics, vmem_limit_bytes=V7X_VMEM_LIMIT_BYTES)


def _load_once(hbm_refs, vmem_refs, sems):
    @pl.when(pl.program_id(0) == 0)
    def _():
        copies = [pltpu.make_async_copy(src, dst, sems.at[n])
                  for n, (src, dst) in enumerate(zip(hbm_refs, vmem_refs))]
        for c in copies:
            c.start()
        for c in copies:
            c.wait()


def _hbm_spec():
    return pl.BlockSpec(memory_space=pl.ANY)


def _rms_rows(x, g):
    ms = jnp.mean(x * x, axis=-1, keepdims=True)
    return x * lax.rsqrt(ms + NORM_EPS) * g


def _sigmoid(x):
    return 1.0 / (1.0 + jnp.exp(-x))


def _main_proj_kernel(x_ref, g_ref, w_ref, o_ref, h_ref):
    @pl.when(pl.program_id(1) == 0)
    def _():
        h_ref[...] = _rms_rows(x_ref[...], g_ref[...]).astype(BF16)

    o_ref[...] = lax.dot_general(h_ref[...], w_ref[...].astype(BF16), NT_DIMS, preferred_element_type=F32)


def _main_proj(x, g, w_in_all, layer, *, tm):
    s, d = x.shape
    return pl.pallas_call(
        _main_proj_kernel,
        grid=(s // tm, MAIN_W // MAIN_TN),
        in_specs=[pl.BlockSpec((tm, d), lambda i, j: (i, 0)),
                  pl.BlockSpec((1, d), lambda i, j: (0, 0)),
                  pl.BlockSpec((None, MAIN_TN, d), lambda i, j: (layer, j, 0))],
        out_specs=[pl.BlockSpec((tm, MAIN_TN), lambda i, j: (i, j)),
                   pl.BlockSpec((tm, d), lambda i, j: (i, 0))],
        out_shape=[jax.ShapeDtypeStruct((s, MAIN_W), F32), jax.ShapeDtypeStruct((s, d), BF16)],
        compiler_params=_cparams(("parallel", "arbitrary")),
        name="norm_proj_main",
    )(x, g, w_in_all)


def _gate_proj_kernel(h_ref, w_ref, b_ref, o_ref, wb_ref, *, n_cols):
    @pl.when(pl.program_id(1) == 0)
    def _():
        col = GATE_WINDOW0 + pl.program_id(0) * GATE_TN + lax.broadcasted_iota(I32, (GATE_TN, 1), 0)
        wb_ref[...] = jnp.where(col < n_cols, w_ref[...], 0.0).astype(BF16)

    acc = lax.dot_general(h_ref[...], wb_ref[...], NT_DIMS, preferred_element_type=F32)
    o_ref[...] = _sigmoid(acc + b_ref[...]).astype(BF16)


def _gate_proj(h, w_in_all, layer, bias_window, *, tm):
    s, d = h.shape
    n = bias_window.shape[1]
    first = GATE_WINDOW0 // GATE_TN
    return pl.pallas_call(
        functools.partial(_gate_proj_kernel, n_cols=w_in_all.shape[1]),
        grid=(n // GATE_TN, s // tm),
        in_specs=[pl.BlockSpec((tm, d), lambda j, i: (i, 0)),
                  pl.BlockSpec((None, GATE_TN, d), lambda j, i: (layer, first + j, 0)),
                  pl.BlockSpec((1, GATE_TN), lambda j, i: (0, j))],
        out_specs=pl.BlockSpec((tm, GATE_TN), lambda j, i: (i, j)),
        out_shape=jax.ShapeDtypeStruct((s, n), BF16),
        scratch_shapes=[pltpu.VMEM((GATE_TN, d), BF16)],
        compiler_params=_cparams(("parallel", "arbitrary")),
        name="gate_proj",
    )(h, w_in_all, bias_window)


def _prep_kernel(f_ref, pos_ref, invf_ref, qg_ref, kg_ref, xg_ref,
                 q_ref, k_ref, v_ref, xq_ref, qi_ref, ki_ref, wib_ref):
    tm = f_ref.shape[0]
    pos = pos_ref[...].astype(F32)
    lane = lax.broadcasted_iota(I32, (1, LANES), 1)
    ang = pos * invf_ref[0:1, :]
    cos_a = jnp.cos(ang)
    sin_a = jnp.where(lane < HEAD_DIM // 2, -jnp.sin(ang), jnp.sin(ang))
    ang_i = pos * invf_ref[1:2, :]
    low_i = (lane & (IDX_DIM - 1)) < IDX_DIM // 2
    cos_i = jnp.cos(ang_i)
    sin_i = jnp.where(low_i, -jnp.sin(ang_i), jnp.sin(ang_i))

    def rope_head(y):
        return y * cos_a + pltpu.roll(y, HEAD_DIM // 2, 1) * sin_a

    def rope_idx(y):
        partner = jnp.where(low_i, pltpu.roll(y, LANES - IDX_DIM // 2, 1), pltpu.roll(y, IDX_DIM // 2, 1))
        return y * cos_i + partner * sin_i

    for h in range(ATTN_HEADS):
        sl = slice(h * HEAD_DIM, (h + 1) * HEAD_DIM)
        qh = f_ref[:, COL_Q + h * HEAD_DIM:COL_Q + (h + 1) * HEAD_DIM]
        kh = f_ref[:, COL_K + h * HEAD_DIM:COL_K + (h + 1) * HEAD_DIM]
        q_ref[:, sl] = rope_head(_rms_rows(qh, qg_ref[...])).astype(BF16)
        k_ref[:, sl] = rope_head(_rms_rows(kh, kg_ref[...])).astype(BF16)
    v_ref[...] = f_ref[:, COL_V:COL_V + ATTN_W].astype(BF16)
    xq_all = f_ref[:, COL_KI:COL_KI + XATTN_W + LANES][:, XQ_SHIFT:XQ_SHIFT + XATTN_W]
    for h in range(XATTN_HEADS):
        sl = slice(h * HEAD_DIM, (h + 1) * HEAD_DIM)
        xq_ref[:, sl] = _rms_rows(xq_all[:, sl], xg_ref[...]).astype(BF16)
    first_half = lane < IDX_DIM
    for pair in range(IDX_HEADS // 2):
        both = rope_idx(f_ref[:, COL_QI + pair * LANES:COL_QI + (pair + 1) * LANES])
        qi_ref[:, (2 * pair) * LANES:(2 * pair + 1) * LANES] = jnp.where(first_half, both, 0.0).astype(BF16)
        qi_ref[:, (2 * pair + 1) * LANES:(2 * pair + 2) * LANES] = jnp.where(first_half, 0.0, both).astype(BF16)
    ki_tile = f_ref[:, COL_KI:COL_KI + LANES]
    ki = jnp.where(first_half, rope_idx(ki_tile), 0.0)
    ki_ref[...] = (ki + pltpu.roll(ki, IDX_DIM, 1)).astype(BF16)
    wi = ki_tile * (IDX_HEADS ** -0.5) * (IDX_DIM ** -0.5)
    for h in range(IDX_HEADS):
        wib_ref[:, h * LANES:(h + 1) * LANES] = jnp.broadcast_to(wi[:, IDX_DIM + h:IDX_DIM + h + 1], (tm, LANES))
    qi_ref[:, IDX_HEADS * LANES:] = jnp.zeros((tm, LANES), BF16)
    wib_ref[:, IDX_HEADS * LANES:] = jnp.zeros((tm, LANES), F32)


def _prep(main, pos_col, invf, qg, kg, xg, *, tm):
    s = main.shape[0]
    row = lambda w: pl.BlockSpec((tm, w), lambda i: (i, 0))
    small = lambda shape: pl.BlockSpec(shape, lambda i: (0, 0))
    outs = [(row(ATTN_W), (s, ATTN_W), BF16), (row(ATTN_W), (s, ATTN_W), BF16), (row(ATTN_W), (s, ATTN_W), BF16),
            (row(XATTN_W), (s, XATTN_W), BF16), (row(IDX_PITCH), (s, IDX_PITCH), BF16),
            (row(LANES), (s, LANES), BF16), (row(IDX_PITCH), (s, IDX_PITCH), F32)]
    return pl.pallas_call(
        _prep_kernel,
        grid=(s // tm,),
        in_specs=[row(MAIN_W), row(1), small((2, LANES)), small((1, LANES)), small((1, LANES)),
                  small((1, LANES))],
        out_specs=[spec for spec, _, _ in outs],
        out_shape=[jax.ShapeDtypeStruct(shape, dt) for _, shape, dt in outs],
        compiler_params=_cparams(("parallel",)),
        name="qknorm_rope",
    )(main, pos_col, invf, qg, kg, xg)


def _pool_kernel(u_ref, halo_ref, bd_ref, sc_ref, o_ref, buf_ref):
    i = pl.program_id(0)
    tm = u_ref.shape[0]
    u = u_ref[...]
    buf_ref[0:POOL_HALO, :] = jnp.where(i == 0, 0.0, halo_ref[...])
    buf_ref[POOL_HALO:POOL_HALO + tm, :] = u
    t = i * tm + lax.broadcasted_iota(I32, (tm, 1), 0)
    lane = lax.broadcasted_iota(I32, (1, POOL_W), 1)
    acc = u
    pooled = None
    for d in range(1, max(POOL_WINDOWS)):
        acc = acc + buf_ref[POOL_HALO - d:POOL_HALO - d + tm, :]
        if d + 1 in POOL_WINDOWS:
            g = POOL_WINDOWS.index(d + 1)
            mean = acc / jnp.minimum(t + 1, d + 1).astype(F32)
            in_group = (lane >= g * POOL_GROUP_W) & (lane < (g + 1) * POOL_GROUP_W)
            pooled = jnp.where(in_group, mean, 0.0 if pooled is None else pooled)
    p = (pooled - u).astype(BF16)
    y = jnp.dot(p, bd_ref[...], preferred_element_type=F32) * sc_ref[...]
    o_ref[...] = y.astype(BF16)


def _pool(front, bd, scale, *, tm):
    s = front.shape[0]
    halo_blocks = tm // POOL_HALO
    return pl.pallas_call(
        _pool_kernel,
        grid=(s // tm,),
        in_specs=[
            pl.BlockSpec((tm, POOL_W), lambda i: (i, 0)),
            pl.BlockSpec((POOL_HALO, POOL_W), lambda i: (jnp.maximum(i * halo_blocks - 1, 0), 0)),
            pl.BlockSpec((POOL_W, POOL_W), lambda i: (0, 0)),
            pl.BlockSpec((1, POOL_W), lambda i: (0, 0)),
        ],
        out_specs=pl.BlockSpec((tm, POOL_W), lambda i: (i, 0)),
        out_shape=jax.ShapeDtypeStruct((s, POOL_W), BF16),
        scratch_shapes=[pltpu.VMEM((tm + POOL_HALO, POOL_W), F32)],
        compiler_params=_cparams(("parallel",)),
        name="pool_mixer",
    )(front, front, bd, scale)


def _memkv_kernel(mem_ref, g_ref, w_ref, kg_ref, k_ref, v_ref):
    h = _rms_rows(mem_ref[...], g_ref[...]).astype(BF16)
    kv = jnp.dot(h, w_ref[...], preferred_element_type=F32)
    for hd in range(XATTN_HEADS):
        sl = slice(hd * HEAD_DIM, (hd + 1) * HEAD_DIM)
        k_ref[:, sl] = _rms_rows(kv[:, sl], kg_ref[...]).astype(BF16)
    v_ref[...] = kv[:, XATTN_W:].astype(BF16)


def _memkv(mem, g, w, kg):
    m = mem.shape[0]
    return pl.pallas_call(
        _memkv_kernel,
        out_shape=[jax.ShapeDtypeStruct((m, XATTN_W), BF16), jax.ShapeDtypeStruct((m, XATTN_W), BF16)],
        compiler_params=pltpu.CompilerParams(vmem_limit_bytes=V7X_VMEM_LIMIT_BYTES),
        name="mem_kv",
    )(mem, g, w, kg)


def _cross_kernel(xq_ref, k_ref, v_ref, o_ref):
    for h in range(XATTN_HEADS):
        sl = slice(h * HEAD_DIM, (h + 1) * HEAD_DIM)
        logits = lax.dot_general(xq_ref[:, sl], k_ref[:, sl], NT_DIMS,
                                 preferred_element_type=F32) * (HEAD_DIM ** -0.5)
        e = jnp.exp(logits - jnp.max(logits, axis=-1, keepdims=True))
        p = e / jnp.sum(e, axis=-1, keepdims=True)
        o_ref[:, sl] = jnp.dot(p.astype(BF16), v_ref[:, sl], preferred_element_type=F32).astype(BF16)


def _cross(xq, k_m, v_m, *, tm):
    s = xq.shape[0]
    m = k_m.shape[0]
    return pl.pallas_call(
        _cross_kernel,
        grid=(s // tm,),
        in_specs=[pl.BlockSpec((tm, XATTN_W), lambda i: (i, 0)),
                  pl.BlockSpec((m, XATTN_W), lambda i: (0, 0)),
                  pl.BlockSpec((m, XATTN_W), lambda i: (0, 0))],
        out_specs=pl.BlockSpec((tm, XATTN_W), lambda i: (i, 0)),
        out_shape=jax.ShapeDtypeStruct((s, XATTN_W), BF16),
        compiler_params=_cparams(("parallel",)),
        name="mem_cross_attn",
    )(xq, k_m, v_m)


def _key_to_f32(key):
    bits = key ^ ((key >> 31) & 0x7FFFFFFF)
    return lax.bitcast_convert_type(bits, F32)


def _dsa_kernel(qi_ref, wib_ref, q_ref, ki_hbm, k_hbm, v_hbm, tri_hbm, o_ref, sc_ref, hi_ref, lo_ref,
                ki_ref, k_ref, v_ref, tri_ref, load_sems, thr_ref, need_ref, *, tq, kc, ka, k_sel):
    _load_once((ki_hbm, k_hbm, v_hbm, tri_hbm), (ki_ref, k_ref, v_ref, tri_ref), load_sems)
    i = pl.program_id(0)
    n_chunks = ((i + 1) * tq + kc - 1) // kc
    t = i * tq + lax.broadcasted_iota(I32, (tq, 1), 0)
    slabs = kc // LANES

    def slab(c, j):
        return pl.ds(pl.multiple_of(c * kc, kc) + j * LANES, LANES)

    def score_chunk(c, carry):
        off = pl.multiple_of(c * ka, ka)
        ki_c = ki_ref[pl.ds(off, ka), :]
        dots = [lax.dot_general(qi_ref[:, h * LANES:(h + 1) * LANES], ki_c, NT_DIMS,
                                preferred_element_type=F32) for h in range(IDX_HEADS)]
        for j in range(ka // LANES):
            cols = pl.ds(off + j * LANES, LANES)
            s = jnp.zeros((tq, LANES), F32)
            for h in range(IDX_HEADS):
                s = s + jnp.maximum(dots[h][:, j * LANES:(j + 1) * LANES], 0.0) * wib_ref[:, h * LANES:(h + 1) * LANES]
            kpos = off + j * LANES + lax.broadcasted_iota(I32, (1, LANES), 1)
            s = jnp.where(kpos <= t, jnp.where(s == 0.0, 0.0, s), -jnp.inf)
            sc_ref[:, cols] = s
            bits = pltpu.bitcast(s, I32)
            key = bits ^ ((bits >> 31) & 0x7FFFFFFF)
            hi_ref[:, cols] = (key >> 16).astype(I16)
            lo_ref[:, cols] = ((key & 0xFFFF) + I16_MIN).astype(I16)
        return carry

    lax.fori_loop(0, (n_chunks * kc + ka - 1) // ka, score_chunk, 0)

    def count16(ref, cand16):
        def body(c, acc):
            for j in range(slabs):
                acc = acc + jnp.where(ref[:, slab(c, j)] >= cand16, jnp.int16(1), jnp.int16(0))
            return acc
        acc = lax.fori_loop(0, n_chunks, body, jnp.zeros((tq, LANES), I16))
        return jnp.sum(acc.astype(I32), axis=-1, keepdims=True)

    def search16(ref, counted_already):
        def step(b, best):
            cand = best + jnp.left_shift(jnp.int32(1), 15 - b)
            enough = count16(ref, cand.astype(I16)) + counted_already >= k_sel
            return jnp.where(enough, cand, best)
        return lax.fori_loop(0, 16, step, jnp.full((tq, LANES), I16_MIN, I32))

    hi_best = search16(hi_ref, 0)
    hi_best16 = hi_best.astype(I16)

    def split_chunk(c, acc):
        for j in range(slabs):
            h = hi_ref[:, slab(c, j)]
            lo_ref[:, slab(c, j)] = jnp.where(h == hi_best16, lo_ref[:, slab(c, j)], jnp.int16(I16_MIN))
            acc = acc + jnp.where(h > hi_best16, jnp.int16(1), jnp.int16(0))
        return acc

    above = lax.fori_loop(0, n_chunks, split_chunk, jnp.zeros((tq, LANES), I16))
    lo_best = search16(lo_ref, jnp.sum(above.astype(I32), axis=-1, keepdims=True))
    thr_key = jnp.left_shift(hi_best, 16) + (lo_best - I16_MIN)

    def count(pred):
        def body(c, acc):
            for j in range(slabs):
                acc = acc + pred(sc_ref[:, slab(c, j)]).astype(I32)
            return acc
        acc = lax.fori_loop(0, n_chunks, body, jnp.zeros((tq, LANES), I32))
        return jnp.sum(acc, axis=-1, keepdims=True)

    thr_found = _key_to_f32(thr_key)
    n_ge = count(lambda s: s >= thr_found)
    n_gt = count(lambda s: s > thr_found)
    thr_ref[...] = thr_found
    need_ref[...] = jnp.broadcast_to((k_sel - n_gt).astype(F32), (tq, LANES))
    n_wrong = jnp.sum(jnp.where((n_ge >= k_sel) & (n_gt < k_sel), 0.0, 1.0))

    @pl.when(n_wrong > 0.0)
    def _():
        def bisect(b, best):
            cand = best + jnp.left_shift(jnp.int32(1), 31 - b)
            cand_f = _key_to_f32(cand)
            enough = (count(lambda s: s >= cand_f) >= k_sel) | (cand < KEY_NEG_INF)
            return jnp.where(enough, cand, best)

        thr_slow = _key_to_f32(lax.fori_loop(0, 32, bisect, jnp.full((tq, LANES), INT_MIN, I32)))
        thr_ref[...] = thr_slow
        need_ref[...] = jnp.broadcast_to((k_sel - count(lambda s: s > thr_slow)).astype(F32), (tq, LANES))

    thr_col = thr_ref[:, 0:1]
    need = need_ref[:, 0:1]

    def mask_chunk(c, ties_before):
        off = pl.multiple_of(c * kc, kc)
        s = sc_ref[:, pl.ds(off, kc)]
        tie = s == thr_col
        prefix = jnp.dot(jnp.where(tie, 1.0, 0.0).astype(BF16), tri_ref[:, 0:kc], preferred_element_type=F32)
        kpos = off + lax.broadcasted_iota(I32, (1, kc), 1)
        take_tie = tie & (prefix + ties_before <= need)
        sel = ((s > thr_col) | take_tie) & (kpos <= t)
        sc_ref[:, pl.ds(off, kc)] = jnp.where(sel, 0.0, NEG_BIG)
        return ties_before + prefix[:, kc - 1:kc]

    lax.fori_loop(0, n_chunks, mask_chunk, jnp.zeros((tq, 1), F32))

    scale2 = (HEAD_DIM ** -0.5) * math.log2(math.e)
    heads = [slice(h * HEAD_DIM, (h + 1) * HEAD_DIM) for h in range(ATTN_HEADS)]

    assert ka in (kc, 2 * kc)
    if ka == 2 * kc:
        @pl.when(n_chunks % 2 == 1)
        def _():
            sc_ref[:, pl.ds(pl.multiple_of(n_chunks * kc, kc), kc)] = jnp.full((tq, kc), NEG_BIG, F32)

    def unshifted_chunk(c, carry):
        off = pl.multiple_of(c * ka, ka)
        bias = sc_ref[:, pl.ds(off, ka)]
        out = []
        for sl, (lane_sum, acc) in zip(heads, carry):
            p = jnp.exp2(lax.dot_general(q_ref[:, sl], k_ref[pl.ds(off, ka), sl], NT_DIMS,
                                         preferred_element_type=F32) * scale2 + bias)
            for j in range(ka // LANES):
                lane_sum = lane_sum + p[:, j * LANES:(j + 1) * LANES]
            acc = acc + jnp.dot(p.astype(BF16), v_ref[pl.ds(off, ka), sl], preferred_element_type=F32)
            out.append((lane_sum, acc))
        return tuple(out)

    zeros = jnp.zeros((tq, LANES), F32)
    result = lax.fori_loop(0, (n_chunks * kc + ka - 1) // ka, unshifted_chunk,
                           tuple((zeros, zeros) for _ in heads))
    in_range = None
    for sl, (lane_sum, acc) in zip(heads, result):
        row_sum = jnp.sum(lane_sum, axis=-1, keepdims=True)
        o_ref[:, sl] = (acc / row_sum).astype(BF16)
        ok = (row_sum >= SOFTMAX_SUM_MIN) & (row_sum <= SOFTMAX_SUM_MAX)
        in_range = ok if in_range is None else (in_range & ok)
    n_bad = jnp.sum(jnp.where(in_range, 0.0, 1.0))

    @pl.when(n_bad > 0.0)
    def _():
        _shifted_attention(q_ref, k_ref, v_ref, sc_ref, o_ref, n_chunks, tq=tq, kc=kc, scale2=scale2)


def _shifted_attention(q_ref, k_ref, v_ref, sc_ref, o_ref, n_chunks, *, tq, kc, scale2):
    slabs = kc // LANES
    for h in range(ATTN_HEADS):
        sl = slice(h * HEAD_DIM, (h + 1) * HEAD_DIM)

        def logits(c, sl=sl):
            off = pl.multiple_of(c * kc, kc)
            return lax.dot_general(q_ref[:, sl], k_ref[pl.ds(off, kc), sl], NT_DIMS,
                                   preferred_element_type=F32) * scale2 + sc_ref[:, pl.ds(off, kc)]

        def max_chunk(c, lane_max, logits=logits):
            s = logits(c)
            for j in range(slabs):
                lane_max = jnp.maximum(lane_max, s[:, j * LANES:(j + 1) * LANES])
            return lane_max

        lane_max = lax.fori_loop(0, n_chunks, max_chunk, jnp.full((tq, LANES), NEG_BIG, F32))
        row_max = jnp.max(lane_max, axis=-1, keepdims=True)

        def pv_chunk(c, carry, sl=sl, row_max=row_max, logits=logits):
            lane_sum, acc = carry
            off = pl.multiple_of(c * kc, kc)
            p = jnp.exp2(logits(c) - row_max)
            for j in range(slabs):
                lane_sum = lane_sum + p[:, j * LANES:(j + 1) * LANES]
            acc = acc + jnp.dot(p.astype(BF16), v_ref[pl.ds(off, kc), sl], preferred_element_type=F32)
            return lane_sum, acc

        lane_sum, acc = lax.fori_loop(0, n_chunks, pv_chunk,
                                      (jnp.zeros((tq, LANES), F32), jnp.zeros((tq, HEAD_DIM), F32)))
        o_ref[:, sl] = (acc / jnp.sum(lane_sum, axis=-1, keepdims=True)).astype(BF16)


def _dsa(qi, wib, q, ki, k, v, *, tq, kc, ka, k_sel):
    s = q.shape[0]
    tri = jnp.pad(jnp.triu(jnp.ones((kc, kc), BF16)), ((0, 0), (0, LANES)))
    row = lambda w: pl.BlockSpec((tq, w), lambda i: (i, 0))
    return pl.pallas_call(
        functools.partial(_dsa_kernel, tq=tq, kc=kc, ka=ka, k_sel=k_sel),
        grid=(s // tq,),
        in_specs=[row(IDX_PITCH), row(IDX_PITCH), row(ATTN_W),
                  _hbm_spec(), _hbm_spec(), _hbm_spec(), _hbm_spec()],
        out_specs=row(ATTN_W),
        out_shape=jax.ShapeDtypeStruct((s, ATTN_W), BF16),
        scratch_shapes=[pltpu.VMEM((tq, s + LANES), F32), pltpu.VMEM((tq, s + LANES), I16),
                        pltpu.VMEM((tq, s + LANES), I16),
                        pltpu.VMEM(ki.shape, BF16), pltpu.VMEM(k.shape, BF16), pltpu.VMEM(v.shape, BF16),
                        pltpu.VMEM(tri.shape, BF16), pltpu.SemaphoreType.DMA((4,)),
                        pltpu.VMEM((tq, LANES), F32), pltpu.VMEM((tq, LANES), F32)],
        compiler_params=_cparams(("arbitrary",)),
        name="dsa_attention",
    )(qi, wib, q, ki, k, v, tri)


def _route(logits):
    lane = lax.broadcasted_iota(I32, (1, LANES), 1).astype(F32)
    far = float(LANES)
    is_group = lane < N_GROUPS
    g_max = jnp.max(jnp.where(is_group, logits, -jnp.inf), axis=-1, keepdims=True)
    g_top = jnp.min(jnp.where(is_group & (logits == g_max), lane, far), axis=-1, keepdims=True)
    pg_top = 1.0 / jnp.sum(jnp.where(is_group, jnp.exp(logits - g_max), 0.0), axis=-1, keepdims=True)
    first = N_GROUPS + EXPERTS_PER_GROUP * g_top
    in_grp = (lane >= first) & (lane < first + EXPERTS_PER_GROUP)
    e_max = jnp.max(jnp.where(in_grp, logits, -jnp.inf), axis=-1, keepdims=True)
    e_exp = jnp.where(in_grp, jnp.exp(logits - e_max), 0.0)
    pe = jnp.where(in_grp, e_exp / jnp.sum(e_exp, axis=-1, keepdims=True), -1.0)
    p1 = jnp.max(pe, axis=-1, keepdims=True)
    e1 = jnp.min(jnp.where(pe == p1, lane, far), axis=-1, keepdims=True)
    pe_rest = jnp.where(lane == e1, -1.0, pe)
    p2 = jnp.max(pe_rest, axis=-1, keepdims=True)
    e2 = jnp.min(jnp.where(pe_rest == p2, lane, far), axis=-1, keepdims=True)
    den = p1 + p2
    w1 = pg_top * p1 / den
    w2 = pg_top * p2 / den
    lo = jnp.minimum(e1, e2) - first
    hi = jnp.maximum(e1, e2) - first
    pair = lo * (2 * EXPERTS_PER_GROUP - 1 - lo) * 0.5 + (hi - lo - 1.0)
    first_is_lo = e1 < e2
    return (g_top * PAIRS_PER_GROUP + pair, jnp.where(first_is_lo, w1, w2), jnp.where(first_is_lo, w2, w1))


def _merge_kernel(x_ref, p_ref, a_ref, c_ref, gt_ref, gf_ref, wpo_hbm, wao_hbm, wco_hbm, wo_hbm, wr_hbm, tri_hbm,
                  x1_ref, route_ref, cnt_ref, wpo_ref, wao_ref, wco_ref, wo_ref, wr_ref, tri_ref, load_sems):
    _load_once((wpo_hbm, wao_hbm, wco_hbm, wo_hbm, wr_hbm, tri_hbm),
               (wpo_ref, wao_ref, wco_ref, wo_ref, wr_ref, tri_ref), load_sems)

    @pl.when(pl.program_id(0) == 0)
    def _():
        cnt_ref[...] = jnp.zeros_like(cnt_ref)

    d = x_ref.shape[1]
    def gate(b):
        window = gt_ref[:, b * d:(b + 1) * d + LANES].astype(F32)
        return window[:, GATE_SHIFT:GATE_SHIFT + d]

    merged = gate(0) * jnp.dot(p_ref[...], wpo_ref[...], preferred_element_type=F32)
    merged += gate(1) * jnp.dot(a_ref[...], wao_ref[...], preferred_element_type=F32)
    merged += gate(2) * jnp.dot(c_ref[...], wco_ref[...], preferred_element_type=F32)
    x1 = x_ref[...] + jnp.dot(merged.astype(BF16), wo_ref[...], preferred_element_type=F32)
    x1_ref[...] = x1
    h2 = _rms_rows(x1, gf_ref[...]).astype(BF16)
    bucket, w_lo, w_hi = _route(jnp.dot(h2, wr_ref[...], preferred_element_type=F32))
    lane = lax.broadcasted_iota(I32, (1, LANES), 1).astype(F32)
    onehot = lane == bucket
    before = jnp.dot(tri_ref[...], jnp.where(onehot, 1.0, 0.0).astype(BF16), preferred_element_type=F32)
    rank = jnp.sum(jnp.where(onehot, before + cnt_ref[...], 0.0), axis=-1, keepdims=True)
    cnt_ref[...] += jnp.sum(jnp.where(onehot, 1.0, 0.0), axis=0, keepdims=True)
    route_ref[...] = jnp.where(lane == ROUTE_BUCKET, bucket,
                               jnp.where(lane == ROUTE_RANK, rank,
                                         jnp.where(lane == ROUTE_W_LO, w_lo,
                                                   jnp.where(lane == ROUTE_W_HI, w_hi, 0.0))))


def _merge(x, p, a, c, gates, wpo, wao, wco, wo, gf, wr, *, tm):
    s, d = x.shape
    row = lambda w: pl.BlockSpec((tm, w), lambda i: (i, 0))
    tri = jnp.tril(jnp.ones((tm, tm), BF16), -1)
    weights = (wpo, wao, wco, wo, wr, tri)
    return pl.pallas_call(
        _merge_kernel,
        grid=(s // tm,),
        in_specs=[row(d), row(POOL_W), row(ATTN_W), row(XATTN_W), row(gates.shape[1]),
                  pl.BlockSpec(gf.shape, lambda i: (0, 0))] + [_hbm_spec()] * len(weights),
        out_specs=[row(d), row(LANES)],
        out_shape=[jax.ShapeDtypeStruct((s, d), F32), jax.ShapeDtypeStruct((s, LANES), F32)],
        scratch_shapes=[pltpu.VMEM((1, LANES), F32)] + [pltpu.VMEM(w.shape, BF16) for w in weights]
                       + [pltpu.SemaphoreType.DMA((len(weights),))],
        compiler_params=_cparams(("arbitrary",)),
        name="gated_merge_router",
    )(x, p, a, c, gates, gf, *weights)


def _moe_kernel(src_ref, elo_ref, ehi_ref, nvalid_ref, ntiles_ref,
                x1_hbm, w_ref, gf_ref, wg_lo, wu_lo, wd_lo, wg_hi, wu_hi, wd_hi,
                out_hbm, xbuf, obuf, gsem, ssem, *, tile):
    j = pl.program_id(0)
    n_tiles = ntiles_ref[0]
    slot = lax.rem(j, 2)

    def start_gather(t, s):
        def body(r, carry):
            tok = src_ref[t * tile + r]
            pltpu.make_async_copy(x1_hbm.at[pl.ds(tok, 1)], xbuf.at[s, pl.ds(r, 1)], gsem.at[s]).start()
            return carry
        lax.fori_loop(0, tile, body, 0, unroll=8)

    def wait_scatter(t, s):
        n = nvalid_ref[t]
        n_whole = pl.multiple_of(lax.shift_left(lax.shift_right_logical(n, 3), 3), SUBLANES)

        @pl.when(n_whole > 0)
        def _():
            pltpu.make_async_copy(obuf.at[s, pl.ds(0, n_whole)], out_hbm.at[pl.ds(0, n_whole)], ssem.at[s]).wait()

        def wait_row(r, carry):
            pltpu.make_async_copy(obuf.at[s, pl.ds(0, 1)], out_hbm.at[pl.ds(0, 1)], ssem.at[s]).wait()
            return carry
        lax.fori_loop(0, n - n_whole, wait_row, 0)

    @pl.when(j == 0)
    def _():
        start_gather(0, 0)

    @pl.when(j + 1 < n_tiles)
    def _():
        start_gather(j + 1, 1 - slot)

    @pl.when(j < n_tiles)
    def _():
        pltpu.make_async_copy(x1_hbm.at[pl.ds(0, tile)], xbuf.at[slot], gsem.at[slot]).wait()
        x1 = xbuf[slot]
        h = _rms_rows(x1, gf_ref[...]).astype(BF16)
        y = x1
        for w_col, wg, wu, wd in ((ROUTE_W_LO, wg_lo, wu_lo, wd_lo), (ROUTE_W_HI, wg_hi, wu_hi, wd_hi)):
            a = jnp.dot(h, wg[0], preferred_element_type=F32)
            b = jnp.dot(h, wu[0], preferred_element_type=F32)
            act = (a * _sigmoid(a)) * b * w_ref[:, w_col:w_col + 1]
            y = y + jnp.dot(act.astype(BF16), wd[0], preferred_element_type=F32)
        obuf[slot] = y

        def scatter_row(r, carry):
            tok = src_ref[j * tile + r]
            pltpu.make_async_copy(obuf.at[slot, pl.ds(r, 1)], out_hbm.at[pl.ds(tok, 1)], ssem.at[slot]).start()
            return carry
        lax.fori_loop(0, nvalid_ref[j], scatter_row, 0)

        @pl.when(j >= 1)
        def _():
            wait_scatter(j - 1, 1 - slot)

        @pl.when(j == n_tiles - 1)
        def _():
            wait_scatter(j, slot)


def _int_from_comparisons(v, n_bits):
    out = jnp.zeros(v.shape, I32)
    for b in range(n_bits):
        bit_set = jnp.floor(v / 2.0 ** b) - 2.0 * jnp.floor(v / 2.0 ** (b + 1)) >= 0.5
        out = out + jnp.where(bit_set, 1 << b, 0)
    return out


def _moe(x1, route, gf, wg, wu, wd, *, tile):
    s, d = x1.shape
    ff = wg.shape[2]
    n_buckets = N_GROUPS * PAIRS_PER_GROUP
    max_tiles = (s + n_buckets * (tile - 1)) // tile
    bucket = _int_from_comparisons(route[:, ROUTE_BUCKET], 5)
    rank = _int_from_comparisons(route[:, ROUTE_RANK], 14)
    counts = jnp.zeros((n_buckets,), I32).at[bucket].add(1)
    padded = (counts + tile - 1) // tile * tile
    ends = jnp.cumsum(padded)
    starts = ends - padded
    dest = starts[bucket] + rank
    src = jnp.zeros((max_tiles * tile,), I32).at[dest].set(jnp.arange(s, dtype=I32))
    w_sorted = jnp.zeros((max_tiles * tile, ROUTE_COLS), F32).at[dest].set(route[:, :ROUTE_COLS])
    n_tiles = ends[-1] // tile
    tile_start = jnp.arange(max_tiles, dtype=I32) * tile
    first_row = jnp.minimum(tile_start, ends[-1] - tile)
    tile_bucket = jnp.sum((ends[None, :] <= first_row[:, None]).astype(I32), axis=1)
    nvalid = jnp.where(tile_start < ends[-1],
                       jnp.clip(counts[tile_bucket] - (tile_start - starts[tile_bucket]), 0, tile), 0).astype(I32)
    pair_lo = jnp.array([0, 0, 0, 1, 1, 2], I32)
    pair_hi = jnp.array([1, 2, 3, 2, 3, 3], I32)
    group = tile_bucket // PAIRS_PER_GROUP
    e_lo = group * EXPERTS_PER_GROUP + pair_lo[tile_bucket % PAIRS_PER_GROUP]
    e_hi = group * EXPERTS_PER_GROUP + pair_hi[tile_bucket % PAIRS_PER_GROUP]

    lo_w = lambda shape: pl.BlockSpec(shape, lambda j, src, elo, ehi, nv, nt: (elo[j], 0, 0))
    hi_w = lambda shape: pl.BlockSpec(shape, lambda j, src, elo, ehi, nv, nt: (ehi[j], 0, 0))
    grid_spec = pltpu.PrefetchScalarGridSpec(
        num_scalar_prefetch=5,
        grid=(max_tiles,),
        in_specs=[pl.BlockSpec(memory_space=pl.ANY),
                  pl.BlockSpec((tile, ROUTE_COLS), lambda j, *_: (j, 0)),
                  pl.BlockSpec((1, d), lambda j, *_: (0, 0)),
                  lo_w((1, d, ff)), lo_w((1, d, ff)), lo_w((1, ff, d)),
                  hi_w((1, d, ff)), hi_w((1, d, ff)), hi_w((1, ff, d))],
        out_specs=pl.BlockSpec(memory_space=pl.ANY),
        scratch_shapes=[pltpu.VMEM((2, tile, d), F32), pltpu.VMEM((2, tile, d), F32),
                        pltpu.SemaphoreType.DMA((2,)), pltpu.SemaphoreType.DMA((2,))],
    )
    return pl.pallas_call(
        functools.partial(_moe_kernel, tile=tile),
        grid_spec=grid_spec,
        out_shape=jax.ShapeDtypeStruct((s, d), F32),
        compiler_params=_cparams(("arbitrary",)),
        name="moe_experts",
    )(src, e_lo, e_hi, nvalid, n_tiles.reshape(1), x1, w_sorted, gf, wg, wu, wd, wg, wu, wd)


def _rope_inv_freq():
    def inv(dim):
        return ROPE_THETA ** (-jnp.arange(0, dim, 2, dtype=F32) / dim)
    return jnp.stack([jnp.tile(inv(HEAD_DIM), LANES // (HEAD_DIM // 2)),
                      jnp.tile(inv(IDX_DIM), LANES // (IDX_DIM // 2))])


def _layer(x, mem, pos, g_mix, w_in_all, layer, b_gate, w_pool_grp, pool_scale, q_norm_g, k_norm_g, g_mem, w_mem_kv,
           xq_norm_g, xk_norm_g, w_pool_out, w_attn_out, w_cross_out, w_o, g_ffn, w_router_group,
           w_router_expert, w_e_gate, w_e_up, w_e_down):
    s, d = x.shape
    row2 = lambda v: v.reshape(1, -1)
    k_sel = min(TOPK_MAX, s // 4)

    main, h = _main_proj(x, row2(g_mix), w_in_all, layer, tm=TILES.main_rows)
    n_gate_cols = -(-(GATE_SHIFT + N_BRANCH * d) // GATE_TN) * GATE_TN
    bias_window = jnp.pad(row2(b_gate), ((0, 0), (GATE_SHIFT, n_gate_cols - GATE_SHIFT - N_BRANCH * d)))
    gates = _gate_proj(h, w_in_all, layer, bias_window, tm=TILES.gate_rows)

    q, k, v, xq, qi, ki, wib = _prep(main, pos.reshape(s, 1), _rope_inv_freq(), row2(q_norm_g),
                                     row2(k_norm_g), row2(xq_norm_g), tm=TILES.prep_rows)

    bd = jax.scipy.linalg.block_diag(*[w_pool_grp[g] for g in range(len(POOL_WINDOWS))])
    pool_pre = _pool(main, bd.astype(BF16), row2(pool_scale), tm=TILES.pool_rows)

    k_m, v_m = _memkv(mem, row2(g_mem), w_mem_kv.astype(BF16), row2(xk_norm_g))
    cross = _cross(xq, k_m, v_m, tm=TILES.cross_rows)

    attn = _dsa(qi, wib, q, ki, k, v, tq=TILES.dsa_queries, kc=TILES.dsa_key_chunk, ka=TILES.dsa_key_step,
                k_sel=k_sel)

    w_router = jnp.pad(jnp.concatenate([w_router_group, w_router_expert], axis=1),
                       ((0, 0), (0, LANES - N_GROUPS - N_EXPERTS)))
    x1, route = _merge(x, pool_pre, attn, cross, gates, w_pool_out.astype(BF16), w_attn_out.astype(BF16),
                       w_cross_out.astype(BF16), w_o.astype(BF16), row2(g_ffn), w_router.astype(BF16),
                       tm=TILES.merge_rows)
    return _moe(x1, route, row2(g_ffn), w_e_gate.astype(BF16), w_e_up.astype(BF16), w_e_down.astype(BF16),
                tile=TILES.moe_rows)


def kernel(x, mem, positions, g_mix, w_in, b_gate, w_pool_grp, pool_scale, q_norm_g, k_norm_g, g_mem, w_mem_kv,
           xq_norm_g, xk_norm_g, w_pool_out, w_attn_out, w_cross_out, w_o, g_ffn, w_router_group,
           w_router_expert, w_e_gate, w_e_up, w_e_down):
    depth = g_mix.shape[0]
    w_in_t = jnp.swapaxes(w_in, 1, 2)
    outs = []
    for b in range(x.shape[0]):
        xb = x[b]
        for l in range(depth):
            xb = _layer(xb, mem[b], positions[b], g_mix[l], w_in_t, l, b_gate[l], w_pool_grp[l], pool_scale[l],
                        q_norm_g[l], k_norm_g[l], g_mem[l], w_mem_kv[l], xq_norm_g[l], xk_norm_g[l],
                        w_pool_out[l], w_attn_out[l], w_cross_out[l], w_o[l], g_ffn[l], w_router_group[l],
                        w_router_expert[l], w_e_gate[l], w_e_up[l], w_e_down[l])
        outs.append(xb)
    return jnp.stack(outs)
```

```python
import functools
import math
from typing import NamedTuple

import jax
import jax.numpy as jnp
from jax import lax
from jax.experimental import pallas as pl
from jax.experimental.pallas import tpu as pltpu

F32 = jnp.float32
BF16 = jnp.bfloat16
I32 = jnp.int32
I16 = jnp.int16
I16_MIN = -32768

NORM_EPS = 1e-6
ROPE_THETA = 10000.0
HEAD_DIM = 128
ATTN_HEADS = 6
ATTN_W = ATTN_HEADS * HEAD_DIM
IDX_HEADS = 4
IDX_DIM = 64
XATTN_HEADS = 4
XATTN_W = XATTN_HEADS * HEAD_DIM
POOL_WINDOWS = (2, 4, 8, 16)
POOL_GROUP_W = 192
POOL_W = len(POOL_WINDOWS) * POOL_GROUP_W
POOL_HALO = 16
TOPK_MAX = 256
N_GROUPS = 4
EXPERTS_PER_GROUP = 4
N_EXPERTS = N_GROUPS * EXPERTS_PER_GROUP
N_BRANCH = 3
PAIRS_PER_GROUP = EXPERTS_PER_GROUP * (EXPERTS_PER_GROUP - 1) // 2
ROUTE_BUCKET, ROUTE_RANK, ROUTE_W_LO, ROUTE_W_HI, ROUTE_COLS = 0, 1, 2, 3, 4

LANES = 128
SUBLANES = 8
GATHER_GROUP = 8
V7X_VMEM_LIMIT_BYTES = 56 * 1024 * 1024

COL_POOL = 0
COL_Q = COL_POOL + POOL_W
COL_K = COL_Q + ATTN_W
COL_V = COL_K + ATTN_W
COL_QI = COL_V + ATTN_W
COL_KI = COL_QI + IDX_HEADS * IDX_DIM
XQ_SHIFT = IDX_DIM + IDX_HEADS
GATE_COL0 = COL_KI + XQ_SHIFT + XATTN_W
MAIN_TN = 512
MAIN_W = -(-GATE_COL0 // MAIN_TN) * MAIN_TN
GATE_TN = 1280
GATE_WINDOW0 = GATE_COL0 // GATE_TN * GATE_TN
GATE_SHIFT = GATE_COL0 - GATE_WINDOW0
assert COL_QI % LANES == 0 and COL_KI % LANES == 0 and GATE_SHIFT < LANES
IDX_PITCH = IDX_HEADS * LANES + LANES


class _Tiles(NamedTuple):
    main_rows: int = 1024
    gate_rows: int = 1024
    prep_rows: int = 512
    pool_rows: int = 512
    cross_rows: int = 512
    dsa_queries: int = 256
    dsa_key_chunk: int = 512
    dsa_key_step: int = 1024
    merge_rows: int = 256
    moe_rows: int = 192


TILES = _Tiles()

NEG_BIG = -1e30
INT_MIN = -2147483648
KEY_NEG_INF = -2139095041
SOFTMAX_SUM_MIN = 2.0 ** -64
SOFTMAX_SUM_MAX = 2.0 ** 100
NT_DIMS = (((1,), (1,)), ((), ()))


def _cparams(semantics):
    return pltpu.CompilerParams(dimension_semantics=semantics, vmem_limit_bytes=V7X_VMEM_LIMIT_BYTES)


def _load_once(hbm_refs, vmem_refs, sems):
    @pl.when(pl.program_id(0) == 0)
    def _():
        copies = [pltpu.make_async_copy(src, dst, sems.at[n])
                  for n, (src, dst) in enumerate(zip(hbm_refs, vmem_refs))]
        for c in copies:
            c.start()
        for c in copies:
            c.wait()


def _hbm_spec():
    return pl.BlockSpec(memory_space=pl.ANY)


def _rms_rows(x, g):
    ms = jnp.mean(x * x, axis=-1, keepdims=True)
    return x * lax.rsqrt(ms + NORM_EPS) * g


def _sigmoid(x):
    return 1.0 / (1.0 + jnp.exp(-x))


def _main_proj_kernel(x_ref, g_ref, w_ref, o_ref, h_ref):
    @pl.when(pl.program_id(1) == 0)
    def _():
        h_ref[...] = _rms_rows(x_ref[...], g_ref[...]).astype(BF16)

    o_ref[...] = lax.dot_general(h_ref[...], w_ref[...].astype(BF16), NT_DIMS, preferred_element_type=F32)


def _main_proj(x, g, w_in_all, layer, *, tm):
    s, d = x.shape
    return pl.pallas_call(
        _main_proj_kernel,
        grid=(s // tm, MAIN_W // MAIN_TN),
        in_specs=[pl.BlockSpec((tm, d), lambda i, j: (i, 0)),
                  pl.BlockSpec((1, d), lambda i, j: (0, 0)),
                  pl.BlockSpec((None, MAIN_TN, d), lambda i, j: (layer, j, 0))],
        out_specs=[pl.BlockSpec((tm, MAIN_TN), lambda i, j: (i, j)),
                   pl.BlockSpec((tm, d), lambda i, j: (i, 0))],
        out_shape=[jax.ShapeDtypeStruct((s, MAIN_W), F32), jax.ShapeDtypeStruct((s, d), BF16)],
        compiler_params=_cparams(("parallel", "arbitrary")),
        name="norm_proj_main",
    )(x, g, w_in_all)


def _gate_proj_kernel(h_ref, w_ref, b_ref, o_ref, wb_ref, *, n_cols):
    @pl.when(pl.program_id(1) == 0)
    def _():
        col = GATE_WINDOW0 + pl.program_id(0) * GATE_TN + lax.broadcasted_iota(I32, (GATE_TN, 1), 0)
        wb_ref[...] = jnp.where(col < n_cols, w_ref[...], 0.0).astype(BF16)

    acc = lax.dot_general(h_ref[...], wb_ref[...], NT_DIMS, preferred_element_type=F32)
    o_ref[...] = _sigmoid(acc + b_ref[...]).astype(BF16)


def _gate_proj(h, w_in_all, layer, bias_window, *, tm):
    s, d = h.shape
    n = bias_window.shape[1]
    first = GATE_WINDOW0 // GATE_TN
    return pl.pallas_call(
        functools.partial(_gate_proj_kernel, n_cols=w_in_all.shape[1]),
        grid=(n // GATE_TN, s // tm),
        in_specs=[pl.BlockSpec((tm, d), lambda j, i: (i, 0)),
                  pl.BlockSpec((None, GATE_TN, d), lambda j, i: (layer, first + j, 0)),
                  pl.BlockSpec((1, GATE_TN), lambda j, i: (0, j))],
        out_specs=pl.BlockSpec((tm, GATE_TN), lambda j, i: (i, j)),
        out_shape=jax.ShapeDtypeStruct((s, n), BF16),
        scratch_shapes=[pltpu.VMEM((GATE_TN, d), BF16)],
        compiler_params=_cparams(("parallel", "arbitrary")),
        name="gate_proj",
    )(h, w_in_all, bias_window)


def _prep_kernel(f_ref, pos_ref, invf_ref, qg_ref, kg_ref, xg_ref,
                 q_ref, k_ref, v_ref, xq_ref, qi_ref, ki_ref, wib_ref):
    tm = f_ref.shape[0]
    pos = pos_ref[...].astype(F32)
    lane = lax.broadcasted_iota(I32, (1, LANES), 1)
    ang = pos * invf_ref[0:1, :]
    cos_a = jnp.cos(ang)
    sin_a = jnp.where(lane < HEAD_DIM // 2, -jnp.sin(ang), jnp.sin(ang))
    ang_i = pos * invf_ref[1:2, :]
    low_i = (lane & (IDX_DIM - 1)) < IDX_DIM // 2
    cos_i = jnp.cos(ang_i)
    sin_i = jnp.where(low_i, -jnp.sin(ang_i), jnp.sin(ang_i))

    def rope_head(y):
        return y * cos_a + pltpu.roll(y, HEAD_DIM // 2, 1) * sin_a

    def rope_idx(y):
        partner = jnp.where(low_i, pltpu.roll(y, LANES - IDX_DIM // 2, 1), pltpu.roll(y, IDX_DIM // 2, 1))
        return y * cos_i + partner * sin_i

    for h in range(ATTN_HEADS):
        sl = slice(h * HEAD_DIM, (h + 1) * HEAD_DIM)
        qh = f_ref[:, COL_Q + h * HEAD_DIM:COL_Q + (h + 1) * HEAD_DIM]
        kh = f_ref[:, COL_K + h * HEAD_DIM:COL_K + (h + 1) * HEAD_DIM]
        q_ref[:, sl] = rope_head(_rms_rows(qh, qg_ref[...])).astype(BF16)
        k_ref[:, sl] = rope_head(_rms_rows(kh, kg_ref[...])).astype(BF16)
    v_ref[...] = f_ref[:, COL_V:COL_V + ATTN_W].astype(BF16)
    xq_all = f_ref[:, COL_KI:COL_KI + XATTN_W + LANES][:, XQ_SHIFT:XQ_SHIFT + XATTN_W]
    for h in range(XATTN_HEADS):
        sl = slice(h * HEAD_DIM, (h + 1) * HEAD_DIM)
        xq_ref[:, sl] = _rms_rows(xq_all[:, sl], xg_ref[...]).astype(BF16)
    first_half = lane < IDX_DIM
    for pair in range(IDX_HEADS // 2):
        both = rope_idx(f_ref[:, COL_QI + pair * LANES:COL_QI + (pair + 1) * LANES])
        qi_ref[:, (2 * pair) * LANES:(2 * pair + 1) * LANES] = jnp.where(first_half, both, 0.0).astype(BF16)
        qi_ref[:, (2 * pair + 1) * LANES:(2 * pair + 2) * LANES] = jnp.where(first_half, 0.0, both).astype(BF16)
    ki_tile = f_ref[:, COL_KI:COL_KI + LANES]
    ki = jnp.where(first_half, rope_idx(ki_tile), 0.0)
    ki_ref[...] = (ki + pltpu.roll(ki, IDX_DIM, 1)).astype(BF16)
    wi = ki_tile * (IDX_HEADS ** -0.5) * (IDX_DIM ** -0.5)
    for h in range(IDX_HEADS):
        wib_ref[:, h * LANES:(h + 1) * LANES] = jnp.broadcast_to(wi[:, IDX_DIM + h:IDX_DIM + h + 1], (tm, LANES))
    qi_ref[:, IDX_HEADS * LANES:] = jnp.zeros((tm, LANES), BF16)
    wib_ref[:, IDX_HEADS * LANES:] = jnp.zeros((tm, LANES), F32)


def _prep(main, pos_col, invf, qg, kg, xg, *, tm):
    s = main.shape[0]
    row = lambda w: pl.BlockSpec((tm, w), lambda i: (i, 0))
    small = lambda shape: pl.BlockSpec(shape, lambda i: (0, 0))
    outs = [(row(ATTN_W), (s, ATTN_W), BF16), (row(ATTN_W), (s, ATTN_W), BF16), (row(ATTN_W), (s, ATTN_W), BF16),
            (row(XATTN_W), (s, XATTN_W), BF16), (row(IDX_PITCH), (s, IDX_PITCH), BF16),
            (row(LANES), (s, LANES), BF16), (row(IDX_PITCH), (s, IDX_PITCH), F32)]
    return pl.pallas_call(
        _prep_kernel,
        grid=(s // tm,),
        in_specs=[row(MAIN_W), row(1), small((2, LANES)), small((1, LANES)), small((1, LANES)),
                  small((1, LANES))],
        out_specs=[spec for spec, _, _ in outs],
        out_shape=[jax.ShapeDtypeStruct(shape, dt) for _, shape, dt in outs],
        compiler_params=_cparams(("parallel",)),
        name="qknorm_rope",
    )(main, pos_col, invf, qg, kg, xg)


def _pool_kernel(u_ref, halo_ref, bd_ref, sc_ref, o_ref, buf_ref):
    i = pl.program_id(0)
    tm = u_ref.shape[0]
    u = u_ref[...]
    buf_ref[0:POOL_HALO, :] = jnp.where(i == 0, 0.0, halo_ref[...])
    buf_ref[POOL_HALO:POOL_HALO + tm, :] = u
    t = i * tm + lax.broadcasted_iota(I32, (tm, 1), 0)
    lane = lax.broadcasted_iota(I32, (1, POOL_W), 1)
    acc = u
    pooled = None
    for d in range(1, max(POOL_WINDOWS)):
        acc = acc + buf_ref[POOL_HALO - d:POOL_HALO - d + tm, :]
        if d + 1 in POOL_WINDOWS:
            g = POOL_WINDOWS.index(d + 1)
            mean = acc / jnp.minimum(t + 1, d + 1).astype(F32)
            in_group = (lane >= g * POOL_GROUP_W) & (lane < (g + 1) * POOL_GROUP_W)
            pooled = jnp.where(in_group, mean, 0.0 if pooled is None else pooled)
    p = (pooled - u).astype(BF16)
    y = jnp.dot(p, bd_ref[...], preferred_element_type=F32) * sc_ref[...]
    o_ref[...] = y.astype(BF16)


def _pool(front, bd, scale, *, tm):
    s = front.shape[0]
    halo_blocks = tm // POOL_HALO
    return pl.pallas_call(
        _pool_kernel,
        grid=(s // tm,),
        in_specs=[
            pl.BlockSpec((tm, POOL_W), lambda i: (i, 0)),
            pl.BlockSpec((POOL_HALO, POOL_W), lambda i: (jnp.maximum(i * halo_blocks - 1, 0), 0)),
            pl.BlockSpec((POOL_W, POOL_W), lambda i: (0, 0)),
            pl.BlockSpec((1, POOL_W), lambda i: (0, 0)),
        ],
        out_specs=pl.BlockSpec((tm, POOL_W), lambda i: (i, 0)),
        out_shape=jax.ShapeDtypeStruct((s, POOL_W), BF16),
        scratch_shapes=[pltpu.VMEM((tm + POOL_HALO, POOL_W), F32)],
        compiler_params=_cparams(("parallel",)),
        name="pool_mixer",
    )(front, front, bd, scale)


def _memkv_kernel(mem_ref, g_ref, w_ref, kg_ref, k_ref, v_ref):
    h = _rms_rows(mem_ref[...], g_ref[...]).astype(BF16)
    kv = jnp.dot(h, w_ref[...], preferred_element_type=F32)
    for hd in range(XATTN_HEADS):
        sl = slice(hd * HEAD_DIM, (hd + 1) * HEAD_DIM)
        k_ref[:, sl] = _rms_rows(kv[:, sl], kg_ref[...]).astype(BF16)
    v_ref[...] = kv[:, XATTN_W:].astype(BF16)


def _memkv(mem, g, w, kg):
    m = mem.shape[0]
    return pl.pallas_call(
        _memkv_kernel,
        out_shape=[jax.ShapeDtypeStruct((m, XATTN_W), BF16), jax.ShapeDtypeStruct((m, XATTN_W), BF16)],
        compiler_params=pltpu.CompilerParams(vmem_limit_bytes=V7X_VMEM_LIMIT_BYTES),
        name="mem_kv",
    )(mem, g, w, kg)


def _cross_kernel(xq_ref, k_ref, v_ref, o_ref):
    for h in range(XATTN_HEADS):
        sl = slice(h * HEAD_DIM, (h + 1) * HEAD_DIM)
        logits = lax.dot_general(xq_ref[:, sl], k_ref[:, sl], NT_DIMS,
                                 preferred_element_type=F32) * (HEAD_DIM ** -0.5)
        e = jnp.exp(logits - jnp.max(logits, axis=-1, keepdims=True))
        p = e / jnp.sum(e, axis=-1, keepdims=True)
        o_ref[:, sl] = jnp.dot(p.astype(BF16), v_ref[:, sl], preferred_element_type=F32).astype(BF16)


def _cross(xq, k_m, v_m, *, tm):
    s = xq.shape[0]
    m = k_m.shape[0]
    return pl.pallas_call(
        _cross_kernel,
        grid=(s // tm,),
        in_specs=[pl.BlockSpec((tm, XATTN_W), lambda i: (i, 0)),
                  pl.BlockSpec((m, XATTN_W), lambda i: (0, 0)),
                  pl.BlockSpec((m, XATTN_W), lambda i: (0, 0))],
        out_specs=pl.BlockSpec((tm, XATTN_W), lambda i: (i, 0)),
        out_shape=jax.ShapeDtypeStruct((s, XATTN_W), BF16),
        compiler_params=_cparams(("parallel",)),
        name="mem_cross_attn",
    )(xq, k_m, v_m)


def _key_to_f32(key):
    bits = key ^ ((key >> 31) & 0x7FFFFFFF)
    return lax.bitcast_convert_type(bits, F32)


def _dsa_kernel(qi_ref, wib_ref, q_ref, ki_hbm, k_hbm, v_hbm, tri_hbm, o_ref, sc_ref, hi_ref, lo_ref,
                ki_ref, k_ref, v_ref, tri_ref, load_sems, thr_ref, need_ref, *, tq, kc, ka, k_sel):
    _load_once((ki_hbm, k_hbm, v_hbm, tri_hbm), (ki_ref, k_ref, v_ref, tri_ref), load_sems)
    i = pl.program_id(0)
    n_chunks = ((i + 1) * tq + kc - 1) // kc
    t = i * tq + lax.broadcasted_iota(I32, (tq, 1), 0)
    slabs = kc // LANES

    def slab(c, j):
        return pl.ds(pl.multiple_of(c * kc, kc) + j * LANES, LANES)

    def score_chunk(c, carry):
        off = pl.multiple_of(c * ka, ka)
        ki_c = ki_ref[pl.ds(off, ka), :]
        dots = [lax.dot_general(qi_ref[:, h * LANES:(h + 1) * LANES], ki_c, NT_DIMS,
                                preferred_element_type=F32) for h in range(IDX_HEADS)]
        for j in range(ka // LANES):
            cols = pl.ds(off + j * LANES, LANES)
            s = jnp.zeros((tq, LANES), F32)
            for h in range(IDX_HEADS):
                s = s + jnp.maximum(dots[h][:, j * LANES:(j + 1) * LANES], 0.0) * wib_ref[:, h * LANES:(h + 1) * LANES]
            kpos = off + j * LANES + lax.broadcasted_iota(I32, (1, LANES), 1)
            s = jnp.where(kpos <= t, jnp.where(s == 0.0, 0.0, s), -jnp.inf)
            sc_ref[:, cols] = s
            bits = pltpu.bitcast(s, I32)
            key = bits ^ ((bits >> 31) & 0x7FFFFFFF)
            hi_ref[:, cols] = (key >> 16).astype(I16)
            lo_ref[:, cols] = ((key & 0xFFFF) + I16_MIN).astype(I16)
        return carry

    lax.fori_loop(0, (n_chunks * kc + ka - 1) // ka, score_chunk, 0)

    def count16(ref, cand16):
        def body(c, acc):
            for j in range(slabs):
                acc = acc + jnp.where(ref[:, slab(c, j)] >= cand16, jnp.int16(1), jnp.int16(0))
            return acc
        acc = lax.fori_loop(0, n_chunks, body, jnp.zeros((tq, LANES), I16))
        return jnp.sum(acc.astype(I32), axis=-1, keepdims=True)

    def search16(ref, counted_already):
        def step(b, best):
            cand = best + jnp.left_shift(jnp.int32(1), 15 - b)
            enough = count16(ref, cand.astype(I16)) + counted_already >= k_sel
            return jnp.where(enough, cand, best)
        return lax.fori_loop(0, 16, step, jnp.full((tq, LANES), I16_MIN, I32))

    hi_best = search16(hi_ref, 0)
    hi_best16 = hi_best.astype(I16)

    def split_chunk(c, acc):
        for j in range(slabs):
            h = hi_ref[:, slab(c, j)]
            lo_ref[:, slab(c, j)] = jnp.where(h == hi_best16, lo_ref[:, slab(c, j)], jnp.int16(I16_MIN))
            acc = acc + jnp.where(h > hi_best16, jnp.int16(1), jnp.int16(0))
        return acc

    above = lax.fori_loop(0, n_chunks, split_chunk, jnp.zeros((tq, LANES), I16))
    lo_best = search16(lo_ref, jnp.sum(above.astype(I32), axis=-1, keepdims=True))
    thr_key = jnp.left_shift(hi_best, 16) + (lo_best - I16_MIN)

    def count(pred):
        def body(c, acc):
            for j in range(slabs):
                acc = acc + pred(sc_ref[:, slab(c, j)]).astype(I32)
            return acc
        acc = lax.fori_loop(0, n_chunks, body, jnp.zeros((tq, LANES), I32))
        return jnp.sum(acc, axis=-1, keepdims=True)

    thr_found = _key_to_f32(thr_key)
    n_ge = count(lambda s: s >= thr_found)
    n_gt = count(lambda s: s > thr_found)
    thr_ref[...] = thr_found
    need_ref[...] = jnp.broadcast_to((k_sel - n_gt).astype(F32), (tq, LANES))
    n_wrong = jnp.sum(jnp.where((n_ge >= k_sel) & (n_gt < k_sel), 0.0, 1.0))

    @pl.when(n_wrong > 0.0)
    def _():
        def bisect(b, best):
            cand = best + jnp.left_shift(jnp.int32(1), 31 - b)
            cand_f = _key_to_f32(cand)
            enough = (count(lambda s: s >= cand_f) >= k_sel) | (cand < KEY_NEG_INF)
            return jnp.where(enough, cand, best)

        thr_slow = _key_to_f32(lax.fori_loop(0, 32, bisect, jnp.full((tq, LANES), INT_MIN, I32)))
        thr_ref[...] = thr_slow
        need_ref[...] = jnp.broadcast_to((k_sel - count(lambda s: s > thr_slow)).astype(F32), (tq, LANES))

    thr_col = thr_ref[:, 0:1]
    need = need_ref[:, 0:1]

    def mask_chunk(c, ties_before):
        off = pl.multiple_of(c * kc, kc)
        s = sc_ref[:, pl.ds(off, kc)]
        tie = s == thr_col
        prefix = jnp.dot(jnp.where(tie, 1.0, 0.0).astype(BF16), tri_ref[:, 0:kc], preferred_element_type=F32)
        kpos = off + lax.broadcasted_iota(I32, (1, kc), 1)
        take_tie = tie & (prefix + ties_before <= need)
        sel = ((s > thr_col) | take_tie) & (kpos <= t)
        sc_ref[:, pl.ds(off, kc)] = jnp.where(sel, 0.0, NEG_BIG)
        return ties_before + prefix[:, kc - 1:kc]

    lax.fori_loop(0, n_chunks, mask_chunk, jnp.zeros((tq, 1), F32))

    scale2 = (HEAD_DIM ** -0.5) * math.log2(math.e)
    heads = [slice(h * HEAD_DIM, (h + 1) * HEAD_DIM) for h in range(ATTN_HEADS)]

    assert ka in (kc, 2 * kc)
    if ka == 2 * kc:
        @pl.when(n_chunks % 2 == 1)
        def _():
            sc_ref[:, pl.ds(pl.multiple_of(n_chunks * kc, kc), kc)] = jnp.full((tq, kc), NEG_BIG, F32)

    def unshifted_chunk(c, carry):
        off = pl.multiple_of(c * ka, ka)
        bias = sc_ref[:, pl.ds(off, ka)]
        out = []
        for sl, (lane_sum, acc) in zip(heads, carry):
            p = jnp.exp2(lax.dot_general(q_ref[:, sl], k_ref[pl.ds(off, ka), sl], NT_DIMS,
                                         preferred_element_type=F32) * scale2 + bias)
            for j in range(ka // LANES):
                lane_sum = lane_sum + p[:, j * LANES:(j + 1) * LANES]
            acc = acc + jnp.dot(p.astype(BF16), v_ref[pl.ds(off, ka), sl], preferred_element_type=F32)
            out.append((lane_sum, acc))
        return tuple(out)

    zeros = jnp.zeros((tq, LANES), F32)
    result = lax.fori_loop(0, (n_chunks * kc + ka - 1) // ka, unshifted_chunk,
                           tuple((zeros, zeros) for _ in heads))
    in_range = None
    for sl, (lane_sum, acc) in zip(heads, result):
        row_sum = jnp.sum(lane_sum, axis=-1, keepdims=True)
        o_ref[:, sl] = (acc / row_sum).astype(BF16)
        ok = (row_sum >= SOFTMAX_SUM_MIN) & (row_sum <= SOFTMAX_SUM_MAX)
        in_range = ok if in_range is None else (in_range & ok)
    n_bad = jnp.sum(jnp.where(in_range, 0.0, 1.0))

    @pl.when(n_bad > 0.0)
    def _():
        _shifted_attention(q_ref, k_ref, v_ref, sc_ref, o_ref, n_chunks, tq=tq, kc=kc, scale2=scale2)


def _shifted_attention(q_ref, k_ref, v_ref, sc_ref, o_ref, n_chunks, *, tq, kc, scale2):
    slabs = kc // LANES
    for h in range(ATTN_HEADS):
        sl = slice(h * HEAD_DIM, (h + 1) * HEAD_DIM)

        def logits(c, sl=sl):
            off = pl.multiple_of(c * kc, kc)
            return lax.dot_general(q_ref[:, sl], k_ref[pl.ds(off, kc), sl], NT_DIMS,
                                   preferred_element_type=F32) * scale2 + sc_ref[:, pl.ds(off, kc)]

        def max_chunk(c, lane_max, logits=logits):
            s = logits(c)
            for j in range(slabs):
                lane_max = jnp.maximum(lane_max, s[:, j * LANES:(j + 1) * LANES])
            return lane_max

        lane_max = lax.fori_loop(0, n_chunks, max_chunk, jnp.full((tq, LANES), NEG_BIG, F32))
        row_max = jnp.max(lane_max, axis=-1, keepdims=True)

        def pv_chunk(c, carry, sl=sl, row_max=row_max, logits=logits):
            lane_sum, acc = carry
            off = pl.multiple_of(c * kc, kc)
            p = jnp.exp2(logits(c) - row_max)
            for j in range(slabs):
                lane_sum = lane_sum + p[:, j * LANES:(j + 1) * LANES]
            acc = acc + jnp.dot(p.astype(BF16), v_ref[pl.ds(off, kc), sl], preferred_element_type=F32)
            return lane_sum, acc

        lane_sum, acc = lax.fori_loop(0, n_chunks, pv_chunk,
                                      (jnp.zeros((tq, LANES), F32), jnp.zeros((tq, HEAD_DIM), F32)))
        o_ref[:, sl] = (acc / jnp.sum(lane_sum, axis=-1, keepdims=True)).astype(BF16)


def _dsa(qi, wib, q, ki, k, v, *, tq, kc, ka, k_sel):
    s = q.shape[0]
    tri = jnp.pad(jnp.triu(jnp.ones((kc, kc), BF16)), ((0, 0), (0, LANES)))
    row = lambda w: pl.BlockSpec((tq, w), lambda i: (i, 0))
    return pl.pallas_call(
        functools.partial(_dsa_kernel, tq=tq, kc=kc, ka=ka, k_sel=k_sel),
        grid=(s // tq,),
        in_specs=[row(IDX_PITCH), row(IDX_PITCH), row(ATTN_W),
                  _hbm_spec(), _hbm_spec(), _hbm_spec(), _hbm_spec()],
        out_specs=row(ATTN_W),
        out_shape=jax.ShapeDtypeStruct((s, ATTN_W), BF16),
        scratch_shapes=[pltpu.VMEM((tq, s + LANES), F32), pltpu.VMEM((tq, s + LANES), I16),
                        pltpu.VMEM((tq, s + LANES), I16),
                        pltpu.VMEM(ki.shape, BF16), pltpu.VMEM(k.shape, BF16), pltpu.VMEM(v.shape, BF16),
                        pltpu.VMEM(tri.shape, BF16), pltpu.SemaphoreType.DMA((4,)),
                        pltpu.VMEM((tq, LANES), F32), pltpu.VMEM((tq, LANES), F32)],
        compiler_params=_cparams(("arbitrary",)),
        name="dsa_attention",
    )(qi, wib, q, ki, k, v, tri)


def _route(logits):
    lane = lax.broadcasted_iota(I32, (1, LANES), 1).astype(F32)
    far = float(LANES)
    is_group = lane < N_GROUPS
    g_max = jnp.max(jnp.where(is_group, logits, -jnp.inf), axis=-1, keepdims=True)
    g_top = jnp.min(jnp.where(is_group & (logits == g_max), lane, far), axis=-1, keepdims=True)
    pg_top = 1.0 / jnp.sum(jnp.where(is_group, jnp.exp(logits - g_max), 0.0), axis=-1, keepdims=True)
    first = N_GROUPS + EXPERTS_PER_GROUP * g_top
    in_grp = (lane >= first) & (lane < first + EXPERTS_PER_GROUP)
    e_max = jnp.max(jnp.where(in_grp, logits, -jnp.inf), axis=-1, keepdims=True)
    e_exp = jnp.where(in_grp, jnp.exp(logits - e_max), 0.0)
    pe = jnp.where(in_grp, e_exp / jnp.sum(e_exp, axis=-1, keepdims=True), -1.0)
    p1 = jnp.max(pe, axis=-1, keepdims=True)
    e1 = jnp.min(jnp.where(pe == p1, lane, far), axis=-1, keepdims=True)
    pe_rest = jnp.where(lane == e1, -1.0, pe)
    p2 = jnp.max(pe_rest, axis=-1, keepdims=True)
    e2 = jnp.min(jnp.where(pe_rest == p2, lane, far), axis=-1, keepdims=True)
    den = p1 + p2
    w1 = pg_top * p1 / den
    w2 = pg_top * p2 / den
    lo = jnp.minimum(e1, e2) - first
    hi = jnp.maximum(e1, e2) - first
    pair = lo * (2 * EXPERTS_PER_GROUP - 1 - lo) * 0.5 + (hi - lo - 1.0)
    first_is_lo = e1 < e2
    return (g_top * PAIRS_PER_GROUP + pair, jnp.where(first_is_lo, w1, w2), jnp.where(first_is_lo, w2, w1))


def _merge_kernel(x_ref, p_ref, a_ref, c_ref, gt_ref, gf_ref, wpo_hbm, wao_hbm, wco_hbm, wo_hbm, wr_hbm, tri_hbm,
                  x1_ref, route_ref, cnt_ref, wpo_ref, wao_ref, wco_ref, wo_ref, wr_ref, tri_ref, load_sems):
    _load_once((wpo_hbm, wao_hbm, wco_hbm, wo_hbm, wr_hbm, tri_hbm),
               (wpo_ref, wao_ref, wco_ref, wo_ref, wr_ref, tri_ref), load_sems)

    @pl.when(pl.program_id(0) == 0)
    def _():
        cnt_ref[...] = jnp.zeros_like(cnt_ref)

    d = x_ref.shape[1]
    def gate(b):
        window = gt_ref[:, b * d:(b + 1) * d + LANES].astype(F32)
        return window[:, GATE_SHIFT:GATE_SHIFT + d]

    merged = gate(0) * jnp.dot(p_ref[...], wpo_ref[...], preferred_element_type=F32)
    merged += gate(1) * jnp.dot(a_ref[...], wao_ref[...], preferred_element_type=F32)
    merged += gate(2) * jnp.dot(c_ref[...], wco_ref[...], preferred_element_type=F32)
    x1 = x_ref[...] + jnp.dot(merged.astype(BF16), wo_ref[...], preferred_element_type=F32)
    x1_ref[...] = x1
    h2 = _rms_rows(x1, gf_ref[...]).astype(BF16)
    bucket, w_lo, w_hi = _route(jnp.dot(h2, wr_ref[...], preferred_element_type=F32))
    lane = lax.broadcasted_iota(I32, (1, LANES), 1).astype(F32)
    onehot = lane == bucket
    before = jnp.dot(tri_ref[...], jnp.where(onehot, 1.0, 0.0).astype(BF16), preferred_element_type=F32)
    rank = jnp.sum(jnp.where(onehot, before + cnt_ref[...], 0.0), axis=-1, keepdims=True)
    cnt_ref[...] += jnp.sum(jnp.where(onehot, 1.0, 0.0), axis=0, keepdims=True)
    route_ref[...] = jnp.where(lane == ROUTE_BUCKET, bucket,
                               jnp.where(lane == ROUTE_RANK, rank,
                                         jnp.where(lane == ROUTE_W_LO, w_lo,
                                                   jnp.where(lane == ROUTE_W_HI, w_hi, 0.0))))


def _merge(x, p, a, c, gates, wpo, wao, wco, wo, gf, wr, *, tm):
    s, d = x.shape
    row = lambda w: pl.BlockSpec((tm, w), lambda i: (i, 0))
    tri = jnp.tril(jnp.ones((tm, tm), BF16), -1)
    weights = (wpo, wao, wco, wo, wr, tri)
    return pl.pallas_call(
        _merge_kernel,
        grid=(s // tm,),
        in_specs=[row(d), row(POOL_W), row(ATTN_W), row(XATTN_W), row(gates.shape[1]),
                  pl.BlockSpec(gf.shape, lambda i: (0, 0))] + [_hbm_spec()] * len(weights),
        out_specs=[row(d), row(LANES)],
        out_shape=[jax.ShapeDtypeStruct((s, d), F32), jax.ShapeDtypeStruct((s, LANES), F32)],
        scratch_shapes=[pltpu.VMEM((1, LANES), F32)] + [pltpu.VMEM(w.shape, BF16) for w in weights]
                       + [pltpu.SemaphoreType.DMA((len(weights),))],
        compiler_params=_cparams(("arbitrary",)),
        name="gated_merge_router",
    )(x, p, a, c, gates, gf, *weights)


def _moe_kernel(src_ref, elo_ref, ehi_ref, nvalid_ref, ntiles_ref,
                x1_hbm, w_ref, gf_ref, wg_lo, wu_lo, wd_lo, wg_hi, wu_hi, wd_hi,
                out_hbm, xbuf, obuf, gsem, ssem, *, tile):
    j = pl.program_id(0)
    n_tiles = ntiles_ref[0]
    slot = lax.rem(j, 2)

    def start_gather(t, s):
        def body(g, carry):
            for u in range(GATHER_GROUP):
                r = g * GATHER_GROUP + u
                tok = src_ref[t * tile + r]
                pltpu.make_async_copy(x1_hbm.at[pl.ds(tok, 1)], xbuf.at[s, pl.ds(r, 1)],
                                      gsem.at[s]).start(priority=u % 2)
            return carry
        lax.fori_loop(0, tile // GATHER_GROUP, body, 0)

    def wait_scatter(t, s):
        n = nvalid_ref[t]
        n_whole = pl.multiple_of(lax.shift_left(lax.shift_right_logical(n, 3), 3), SUBLANES)

        @pl.when(n_whole > 0)
        def _():
            pltpu.make_async_copy(obuf.at[s, pl.ds(0, n_whole)], out_hbm.at[pl.ds(0, n_whole)], ssem.at[s]).wait()

        def wait_row(r, carry):
            pltpu.make_async_copy(obuf.at[s, pl.ds(0, 1)], out_hbm.at[pl.ds(0, 1)], ssem.at[s]).wait()
            return carry
        lax.fori_loop(0, n - n_whole, wait_row, 0)

    @pl.when(j == 0)
    def _():
        start_gather(0, 0)

    @pl.when(j + 1 < n_tiles)
    def _():
        start_gather(j + 1, 1 - slot)

    @pl.when(j < n_tiles)
    def _():
        pltpu.make_async_copy(x1_hbm.at[pl.ds(0, tile)], xbuf.at[slot], gsem.at[slot]).wait()
        x1 = xbuf[slot]
        h = _rms_rows(x1, gf_ref[...]).astype(BF16)
        y = x1
        for w_col, wg, wu, wd in ((ROUTE_W_LO, wg_lo, wu_lo, wd_lo), (ROUTE_W_HI, wg_hi, wu_hi, wd_hi)):
            a = jnp.dot(h, wg[0], preferred_element_type=F32)
            b = jnp.dot(h, wu[0], preferred_element_type=F32)
            act = (a * _sigmoid(a)) * b * w_ref[:, w_col:w_col + 1]
            y = y + jnp.dot(act.astype(BF16), wd[0], preferred_element_type=F32)
        obuf[slot] = y

        def scatter_row(r, carry):
            tok = src_ref[j * tile + r]
            pltpu.make_async_copy(obuf.at[slot, pl.ds(r, 1)], out_hbm.at[pl.ds(tok, 1)], ssem.at[slot]).start()
            return carry
        lax.fori_loop(0, nvalid_ref[j], scatter_row, 0)

        @pl.when(j >= 1)
        def _():
            wait_scatter(j - 1, 1 - slot)

        @pl.when(j == n_tiles - 1)
        def _():
            wait_scatter(j, slot)


def _int_from_comparisons(v, n_bits):
    out = jnp.zeros(v.shape, I32)
    for b in range(n_bits):
        bit_set = jnp.floor(v / 2.0 ** b) - 2.0 * jnp.floor(v / 2.0 ** (b + 1)) >= 0.5
        out = out + jnp.where(bit_set, 1 << b, 0)
    return out


def _moe(x1, route, gf, wg, wu, wd, *, tile):
    s, d = x1.shape
    ff = wg.shape[2]
    n_buckets = N_GROUPS * PAIRS_PER_GROUP
    max_tiles = (s + n_buckets * (tile - 1)) // tile
    bucket = _int_from_comparisons(route[:, ROUTE_BUCKET], 5)
    rank = _int_from_comparisons(route[:, ROUTE_RANK], 14)
    counts = jnp.zeros((n_buckets,), I32).at[bucket].add(1)
    padded = (counts + tile - 1) // tile * tile
    ends = jnp.cumsum(padded)
    starts = ends - padded
    dest = starts[bucket] + rank
    src = jnp.zeros((max_tiles * tile,), I32).at[dest].set(jnp.arange(s, dtype=I32))
    w_sorted = jnp.zeros((max_tiles * tile, ROUTE_COLS), F32).at[dest].set(route[:, :ROUTE_COLS])
    n_tiles = ends[-1] // tile
    tile_start = jnp.arange(max_tiles, dtype=I32) * tile
    first_row = jnp.minimum(tile_start, ends[-1] - tile)
    tile_bucket = jnp.sum((ends[None, :] <= first_row[:, None]).astype(I32), axis=1)
    nvalid = jnp.where(tile_start < ends[-1],
                       jnp.clip(counts[tile_bucket] - (tile_start - starts[tile_bucket]), 0, tile), 0).astype(I32)
    pair_lo = jnp.array([0, 0, 0, 1, 1, 2], I32)
    pair_hi = jnp.array([1, 2, 3, 2, 3, 3], I32)
    group = tile_bucket // PAIRS_PER_GROUP
    e_lo = group * EXPERTS_PER_GROUP + pair_lo[tile_bucket % PAIRS_PER_GROUP]
    e_hi = group * EXPERTS_PER_GROUP + pair_hi[tile_bucket % PAIRS_PER_GROUP]

    lo_w = lambda shape: pl.BlockSpec(shape, lambda j, src, elo, ehi, nv, nt: (elo[j], 0, 0))
    hi_w = lambda shape: pl.BlockSpec(shape, lambda j, src, elo, ehi, nv, nt: (ehi[j], 0, 0))
    grid_spec = pltpu.PrefetchScalarGridSpec(
        num_scalar_prefetch=5,
        grid=(max_tiles,),
        in_specs=[pl.BlockSpec(memory_space=pl.ANY),
                  pl.BlockSpec((tile, ROUTE_COLS), lambda j, *_: (j, 0)),
                  pl.BlockSpec((1, d), lambda j, *_: (0, 0)),
                  lo_w((1, d, ff)), lo_w((1, d, ff)), lo_w((1, ff, d)),
                  hi_w((1, d, ff)), hi_w((1, d, ff)), hi_w((1, ff, d))],
        out_specs=pl.BlockSpec(memory_space=pl.ANY),
        scratch_shapes=[pltpu.VMEM((2, tile, d), F32), pltpu.VMEM((2, tile, d), F32),
                        pltpu.SemaphoreType.DMA((2,)), pltpu.SemaphoreType.DMA((2,))],
    )
    return pl.pallas_call(
        functools.partial(_moe_kernel, tile=tile),
        grid_spec=grid_spec,
        out_shape=jax.ShapeDtypeStruct((s, d), F32),
        compiler_params=_cparams(("arbitrary",)),
        name="moe_experts",
    )(src, e_lo, e_hi, nvalid, n_tiles.reshape(1), x1, w_sorted, gf, wg, wu, wd, wg, wu, wd)


def _rope_inv_freq():
    def inv(dim):
        return ROPE_THETA ** (-jnp.arange(0, dim, 2, dtype=F32) / dim)
    return jnp.stack([jnp.tile(inv(HEAD_DIM), LANES // (HEAD_DIM // 2)),
                      jnp.tile(inv(IDX_DIM), LANES // (IDX_DIM // 2))])


def _layer(x, mem, pos, g_mix, w_in_all, layer, b_gate, w_pool_grp, pool_scale, q_norm_g, k_norm_g, g_mem, w_mem_kv,
           xq_norm_g, xk_norm_g, w_pool_out, w_attn_out, w_cross_out, w_o, g_ffn, w_router_group,
           w_router_expert, w_e_gate, w_e_up, w_e_down):
    s, d = x.shape
    row2 = lambda v: v.reshape(1, -1)
    k_sel = min(TOPK_MAX, s // 4)

    main, h = _main_proj(x, row2(g_mix), w_in_all, layer, tm=TILES.main_rows)
    n_gate_cols = -(-(GATE_SHIFT + N_BRANCH * d) // GATE_TN) * GATE_TN
    bias_window = jnp.pad(row2(b_gate), ((0, 0), (GATE_SHIFT, n_gate_cols - GATE_SHIFT - N_BRANCH * d)))
    gates = _gate_proj(h, w_in_all, layer, bias_window, tm=TILES.gate_rows)

    q, k, v, xq, qi, ki, wib = _prep(main, pos.reshape(s, 1), _rope_inv_freq(), row2(q_norm_g),
                                     row2(k_norm_g), row2(xq_norm_g), tm=TILES.prep_rows)

    bd = jax.scipy.linalg.block_diag(*[w_pool_grp[g] for g in range(len(POOL_WINDOWS))])
    pool_pre = _pool(main, bd.astype(BF16), row2(pool_scale), tm=TILES.pool_rows)

    k_m, v_m = _memkv(mem, row2(g_mem), w_mem_kv.astype(BF16), row2(xk_norm_g))
    cross = _cross(xq, k_m, v_m, tm=TILES.cross_rows)

    attn = _dsa(qi, wib, q, ki, k, v, tq=TILES.dsa_queries, kc=TILES.dsa_key_chunk, ka=TILES.dsa_key_step,
                k_sel=k_sel)

    w_router = jnp.pad(jnp.concatenate([w_router_group, w_router_expert], axis=1),
                       ((0, 0), (0, LANES - N_GROUPS - N_EXPERTS)))
    x1, route = _merge(x, pool_pre, attn, cross, gates, w_pool_out.astype(BF16), w_attn_out.astype(BF16),
                       w_cross_out.astype(BF16), w_o.astype(BF16), row2(g_ffn), w_router.astype(BF16),
                       tm=TILES.merge_rows)
    return _moe(x1, route, row2(g_ffn), w_e_gate.astype(BF16), w_e_up.astype(BF16), w_e_down.astype(BF16),
                tile=TILES.moe_rows)


def kernel(x, mem, positions, g_mix, w_in, b_gate, w_pool_grp, pool_scale, q_norm_g, k_norm_g, g_mem, w_mem_kv,
           xq_norm_g, xk_norm_g, w_pool_out, w_attn_out, w_cross_out, w_o, g_ffn, w_router_group,
           w_router_expert, w_e_gate, w_e_up, w_e_down):
    depth = g_mix.shape[0]
    w_in_t = jnp.swapaxes(w_in, 1, 2)
    outs = []
    for b in range(x.shape[0]):
        xb = x[b]
        for l in range(depth):
            xb = _layer(xb, mem[b], positions[b], g_mix[l], w_in_t, l, b_gate[l], w_pool_grp[l], pool_scale[l],
                        q_norm_g[l], k_norm_g[l], g_mem[l], w_mem_kv[l], xq_norm_g[l], xk_norm_g[l],
                        w_pool_out[l], w_attn_out[l], w_cross_out[l], w_o[l], g_ffn[l], w_router_group[l],
                        w_router_expert[l], w_e_gate[l], w_e_up[l], w_e_down[l])
        outs.append(xb)
    return jnp.stack(outs)
```
